```python
import math
import jax, jax.numpy as jnp
from jax import lax
import numpy as np

D_MODEL = 4096
BATCH = 2
SEQ = 8192
DEPTH = 2

MIX_WIDTH = D_MODEL
GROUP_WIDTH = MIX_WIDTH // 4
HEAD_DIM = 128
MOBA_HEADS = GROUP_WIDTH // HEAD_DIM
MOBA_BLOCK = 256
MOBA_TOPK = 3
MOBA_QCHUNK = 32
GDN_HEADS = GROUP_WIDTH // HEAD_DIM
GDN_CONV = 4
GDN_CHUNK = 64
SC_CONV = 3
SC_GROUPS = GROUP_WIDTH // HEAD_DIM
SWA_HEAD_DIM = 64
SWA_Q_HEADS = GROUP_WIDTH // SWA_HEAD_DIM
SWA_KV_HEADS = 2
SWA_WINDOW = 128
D_FF = -(-8 * D_MODEL // (3 * 256)) * 256
EPS = 1e-6

IN_SIZES = (
    GROUP_WIDTH, GROUP_WIDTH, GROUP_WIDTH,
    3 * GROUP_WIDTH, GDN_HEADS, GDN_HEADS, GROUP_WIDTH,
    GROUP_WIDTH, GROUP_WIDTH, GROUP_WIDTH,
    SWA_Q_HEADS * SWA_HEAD_DIM, SWA_KV_HEADS * SWA_HEAD_DIM, SWA_KV_HEADS * SWA_HEAD_DIM,
)
IN_WIDTH = sum(IN_SIZES)
SPLIT_POINTS = tuple(int(v) for v in np.cumsum(IN_SIZES)[:-1])

kernel_name = "hymba_style_moba_gdn_shortconv_swa_block"


def rmsnorm(x, g):
    xf = x.astype(jnp.float32)
    y = xf * lax.rsqrt(jnp.mean(xf * xf, axis=-1, keepdims=True) + EPS) * g.astype(jnp.float32)
    return y.astype(x.dtype)


def l2norm(x):
    xf = x.astype(jnp.float32)
    return xf * lax.rsqrt(jnp.sum(xf * xf, axis=-1, keepdims=True) + EPS)


def causal_conv(x, w):
    kw = w.shape[0]
    s = x.shape[1]
    xp = jnp.pad(x, ((0, 0), (kw - 1, 0), (0, 0)))
    out = xp[:, 0:s] * w[0]
    for i in range(1, kw):
        out = out + xp[:, i:i + s] * w[i]
    return out


def moba_attention(q, k, v):
    b, s, h, d = q.shape
    nb = -(-s // MOBA_BLOCK)
    pad = nb * MOBA_BLOCK - s
    qh = q.transpose(0, 2, 1, 3)
    kh = jnp.pad(k.transpose(0, 2, 1, 3), ((0, 0), (0, 0), (0, pad), (0, 0)))
    vh = jnp.pad(v.transpose(0, 2, 1, 3), ((0, 0), (0, 0), (0, pad), (0, 0)))
    kb = kh.reshape(b, h, nb, MOBA_BLOCK, d)
    vb = vh.reshape(b, h, nb, MOBA_BLOCK, d)
    kmean = jnp.mean(kb.astype(jnp.float32), axis=3)
    topk = min(MOBA_TOPK, nb - 1)
    scale = d ** -0.5
    bi = jnp.arange(b)[:, None, None, None]
    hi = jnp.arange(h)[None, :, None, None]
    neg = -jnp.inf

    def chunk(c):
        q0 = c * MOBA_QCHUNK
        qc = lax.dynamic_slice_in_dim(qh, q0, MOBA_QCHUNK, axis=2)
        qpos = q0 + jnp.arange(MOBA_QCHUNK)
        blk = q0 // MOBA_BLOCK
        k_own = lax.dynamic_slice_in_dim(kh, blk * MOBA_BLOCK, MOBA_BLOCK, axis=2)
        v_own = lax.dynamic_slice_in_dim(vh, blk * MOBA_BLOCK, MOBA_BLOCK, axis=2)
        kpos = blk * MOBA_BLOCK + jnp.arange(MOBA_BLOCK)
        s_own = jnp.einsum('bhqd,bhkd->bhqk', qc, k_own).astype(jnp.float32) * scale
        s_own = jnp.where(kpos[None, :] <= qpos[:, None], s_own, neg)
        if topk > 0:
            gate = jnp.einsum('bhqd,bhnd->bhqn', qc.astype(jnp.float32), kmean)
            gate = jnp.where(jnp.arange(nb) < blk, gate, neg)
            gval, gidx = lax.top_k(gate, topk)
            valid = jnp.isfinite(gval)
            k_sel = kb[bi, hi, gidx]
            v_sel = vb[bi, hi, gidx]
            s_sel = jnp.einsum('bhqd,bhqtkd->bhqtk', qc, k_sel).astype(jnp.float32) * scale
            s_sel = jnp.where(valid[..., None], s_sel, neg)
            s_sel = s_sel.reshape(b, h, MOBA_QCHUNK, topk * MOBA_BLOCK)
            p = jax.nn.softmax(jnp.concatenate([s_sel, s_own], axis=-1), axis=-1)
            p_sel = p[..., :topk * MOBA_BLOCK].reshape(b, h, MOBA_QCHUNK, topk, MOBA_BLOCK)
            p_own = p[..., topk * MOBA_BLOCK:]
            o = (jnp.einsum('bhqtk,bhqtkd->bhqd', p_sel.astype(v.dtype), v_sel)
                 + jnp.einsum('bhqk,bhkd->bhqd', p_own.astype(v.dtype), v_own))
        else:
            p = jax.nn.softmax(s_own, axis=-1)
            o = jnp.einsum('bhqk,bhkd->bhqd', p.astype(v.dtype), v_own)
        return o.astype(q.dtype)

    out = lax.map(chunk, jnp.arange(s // MOBA_QCHUNK))
    return out.transpose(1, 0, 3, 2, 4).reshape(b, s, h * d)


def gated_delta_rule(q, k, v, g, beta):
    b, s, h, dk = q.shape
    dv = v.shape[-1]
    C = GDN_CHUNK
    nc = s // C
    f32 = jnp.float32

    def heads_chunks(t):
        t = jnp.moveaxis(t.astype(f32), 2, 1)
        return t.reshape((b, h, nc, C) + t.shape[3:])

    qc = heads_chunks(q) * (dk ** -0.5)
    kc = heads_chunks(k)
    vc = heads_chunks(v)
    gc = heads_chunks(g)
    bc = heads_chunks(beta)
    G = jnp.cumsum(gc, axis=-1)
    idx = jnp.arange(C)
    tril = idx[:, None] >= idx[None, :]
    strict = idx[:, None] > idx[None, :]
    decay = jnp.exp(jnp.where(tril, G[..., :, None] - G[..., None, :], -jnp.inf))
    k_beta = kc * bc[..., None]
    v_beta = vc * bc[..., None]
    kkt = jnp.einsum('bhnik,bhnjk->bhnij', k_beta, kc) * decay
    M = jnp.eye(C, dtype=f32) + jnp.where(strict, kkt, 0.0)
    u = lax.linalg.triangular_solve(M, v_beta, left_side=True, lower=True, unit_diagonal=True)
    w = lax.linalg.triangular_solve(M, k_beta * jnp.exp(G)[..., None], left_side=True,
                                    lower=True, unit_diagonal=True)
    qk = jnp.einsum('bhnik,bhnjk->bhnij', qc, kc) * decay
    q_dec = qc * jnp.exp(G)[..., None]
    k_dec = kc * jnp.exp(G[..., -1:] - G)[..., None]
    g_last = jnp.exp(G[..., -1])

    def step(state, inp):
        qk_i, u_i, w_i, q_i, k_i, gl = inp
        v_new = u_i - jnp.einsum('bhck,bhkv->bhcv', w_i, state)
        o = jnp.einsum('bhck,bhkv->bhcv', q_i, state) + jnp.einsum('bhcj,bhjv->bhcv', qk_i, v_new)
        state = state * gl[..., None, None] + jnp.einsum('bhck,bhcv->bhkv', k_i, v_new)
        return state, o

    xs = tuple(jnp.moveaxis(t, 2, 0) for t in (qk, u, w, q_dec, k_dec, g_last))
    s0 = jnp.zeros((b, h, dk, dv), f32)
    _, o = lax.scan(step, s0, xs)
    o = jnp.moveaxis(o, 0, 2).reshape(b, h, s, dv)
    return o.transpose(0, 2, 1, 3)


def sliding_window_attention(q, k, v, sinks):
    b, s, hq, d = q.shape
    hkv = k.shape[2]
    grp = hq // hkv
    W = SWA_WINDOW
    nb = s // W
    qb = q.reshape(b, nb, W, hkv, grp, d)
    kp = jnp.pad(k, ((0, 0), (W, 0), (0, 0), (0, 0))).reshape(b, nb + 1, W, hkv, d)
    vp = jnp.pad(v, ((0, 0), (W, 0), (0, 0), (0, 0))).reshape(b, nb + 1, W, hkv, d)
    kk = jnp.concatenate([kp[:, :-1], kp[:, 1:]], axis=2)
    vv = jnp.concatenate([vp[:, :-1], vp[:, 1:]], axis=2)
    scores = jnp.einsum('bnqhgd,bnkhd->bhgnqk', qb, kk).astype(jnp.float32) * (d ** -0.5)
    qrel = W + jnp.arange(W)[:, None]
    krel = jnp.arange(2 * W)[None, :]
    nidx = jnp.arange(nb)[:, None, None]
    mask = (krel <= qrel) & (krel > qrel - W) & ((nidx > 0) | (krel >= W))
    scores = jnp.where(mask, scores, -jnp.inf)
    sink = jnp.broadcast_to(sinks.astype(jnp.float32).reshape(hkv, grp)[None, :, :, None, None, None],
                            scores.shape[:-1] + (1,))
    p = jax.nn.softmax(jnp.concatenate([scores, sink], axis=-1), axis=-1)[..., :-1]
    out = jnp.einsum('bhgnqk,bnkhd->bnqhgd', p.astype(v.dtype), vv)
    return out.reshape(b, s, hq * d).astype(v.dtype)


def hybrid_layer(x, norm_mix, w_in, moba_q_norm, moba_k_norm, gdn_conv, gdn_a_log, gdn_dt_bias,
                 gdn_out_norm, sc_conv, swa_q_norm, swa_k_norm, swa_sinks, w_out, norm_ffn,
                 w_gate, w_up, w_down):
    b, s, _ = x.shape
    h = rmsnorm(x, norm_mix)
    proj = h @ w_in
    (mq, mk, mv, gqkv, ga, gb, gz, scb, scc, scx, sq, sk, sv) = jnp.split(proj, SPLIT_POINTS, axis=-1)

    mq = rmsnorm(mq.reshape(b, s, MOBA_HEADS, HEAD_DIM), moba_q_norm)
    mk = rmsnorm(mk.reshape(b, s, MOBA_HEADS, HEAD_DIM), moba_k_norm)
    mv = mv.reshape(b, s, MOBA_HEADS, HEAD_DIM)
    o_a = moba_attention(mq, mk, mv)

    gqkv = jax.nn.silu(causal_conv(gqkv, gdn_conv))
    gq, gk, gv = jnp.split(gqkv, 3, axis=-1)
    gq = l2norm(gq.reshape(b, s, GDN_HEADS, HEAD_DIM))
    gk = l2norm(gk.reshape(b, s, GDN_HEADS, HEAD_DIM))
    gv = gv.reshape(b, s, GDN_HEADS, HEAD_DIM)
    log_decay = -jnp.exp(gdn_a_log.astype(jnp.float32)) * jax.nn.softplus(
        ga.astype(jnp.float32) + gdn_dt_bias.astype(jnp.float32))
    beta = jax.nn.sigmoid(gb.astype(jnp.float32))
    o = gated_delta_rule(gq, gk, gv, log_decay, beta)
    zg = jax.nn.silu(gz.reshape(b, s, GDN_HEADS, HEAD_DIM).astype(jnp.float32))
    o_b = (rmsnorm(o, gdn_out_norm) * zg).reshape(b, s, GROUP_WIDTH).astype(x.dtype)

    o_c = scb * causal_conv(scc * scx, sc_conv)

    sq = rmsnorm(sq.reshape(b, s, SWA_Q_HEADS, SWA_HEAD_DIM), swa_q_norm)
    sk = rmsnorm(sk.reshape(b, s, SWA_KV_HEADS, SWA_HEAD_DIM), swa_k_norm)
    sv = sv.reshape(b, s, SWA_KV_HEADS, SWA_HEAD_DIM)
    o_d = sliding_window_attention(sq, sk, sv, swa_sinks)

    mix = jnp.concatenate([o_a, o_b, o_c.astype(x.dtype), o_d], axis=-1)
    x = x + mix @ w_out

    h = rmsnorm(x, norm_ffn)
    x = x + (jax.nn.silu(h @ w_gate) * (h @ w_up)) @ w_down
    return x


def setup_inputs(seed: int = 0) -> dict:
    key = jax.random.key(seed)
    ks = jax.random.split(key, 20)

    def nrm(k, shape, scale):
        return jax.random.normal(k, shape, jnp.float32) * scale

    dt = jnp.exp(jax.random.uniform(ks[7], (DEPTH, GDN_HEADS), jnp.float32,
                                    minval=math.log(1e-3), maxval=math.log(1e-1)))
    return {
        "x": nrm(ks[0], (BATCH, SEQ, D_MODEL), 1.0),
        "norm_mix": 1.0 + nrm(ks[1], (DEPTH, D_MODEL), 0.02),
        "w_in": nrm(ks[2], (DEPTH, D_MODEL, IN_WIDTH), D_MODEL ** -0.5),
        "moba_q_norm": 1.0 + nrm(ks[3], (DEPTH, HEAD_DIM), 0.02),
        "moba_k_norm": 1.0 + nrm(ks[4], (DEPTH, HEAD_DIM), 0.02),
        "gdn_conv": nrm(ks[5], (DEPTH, GDN_CONV, 3 * GROUP_WIDTH), GDN_CONV ** -0.5),
        "gdn_a_log": jnp.log(jax.random.uniform(ks[6], (DEPTH, GDN_HEADS), jnp.float32,
                                                minval=1.0, maxval=16.0)),
        "gdn_dt_bias": dt + jnp.log(-jnp.expm1(-dt)),
        "gdn_out_norm": 1.0 + nrm(ks[8], (DEPTH, HEAD_DIM), 0.02),
        "sc_conv": nrm(ks[9], (DEPTH, SC_CONV, GROUP_WIDTH), SC_CONV ** -0.5),
        "swa_q_norm": 1.0 + nrm(ks[10], (DEPTH, SWA_HEAD_DIM), 0.02),
        "swa_k_norm": 1.0 + nrm(ks[11], (DEPTH, SWA_HEAD_DIM), 0.02),
        "swa_sinks": nrm(ks[12], (DEPTH, SWA_Q_HEADS), 1.0),
        "w_out": nrm(ks[13], (DEPTH, MIX_WIDTH, D_MODEL), MIX_WIDTH ** -0.5),
        "norm_ffn": 1.0 + nrm(ks[14], (DEPTH, D_MODEL), 0.02),
        "w_gate": nrm(ks[15], (DEPTH, D_MODEL, D_FF), D_MODEL ** -0.5),
        "w_up": nrm(ks[16], (DEPTH, D_MODEL, D_FF), D_MODEL ** -0.5),
        "w_down": nrm(ks[17], (DEPTH, D_FF, D_MODEL), D_FF ** -0.5),
    }


def reference(x, norm_mix, w_in, moba_q_norm, moba_k_norm, gdn_conv, gdn_a_log, gdn_dt_bias,
              gdn_out_norm, sc_conv, swa_q_norm, swa_k_norm, swa_sinks, w_out, norm_ffn,
              w_gate, w_up, w_down):
    for l in range(DEPTH):
        x = hybrid_layer(x, norm_mix[l], w_in[l], moba_q_norm[l], moba_k_norm[l], gdn_conv[l],
                         gdn_a_log[l], gdn_dt_bias[l], gdn_out_norm[l], sc_conv[l], swa_q_norm[l],
                         swa_k_norm[l], swa_sinks[l], w_out[l], norm_ffn[l], w_gate[l], w_up[l],
                         w_down[l])
    return x
```

```python
import functools

import jax
import jax.numpy as jnp
from jax import lax
from jax.experimental import pallas as pl
from jax.experimental.pallas import tpu as pltpu

F32 = jnp.float32
BF16 = jnp.bfloat16

EPS = 1e-6
LANES = 128
GROUP = 1024
HEAD_DIM = 128
N_HEADS = GROUP // HEAD_DIM
MOBA_BLOCK = 256
MOBA_TOPK = 3
GDN_CONV = 4
GDN_CHUNK = 64
SC_CONV = 3
SWA_D = 64
SWA_Q_HEADS = GROUP // SWA_D
SWA_KV_HEADS = 2
SWA_W = 128
NEG = -1e30

CB_MQ, CB_MK, CB_MV = 0, 8, 16
CB_GQ, CB_GK, CB_GV, CB_GZ = 24, 32, 40, 48
CB_SCB, CB_SCC, CB_SCX = 56, 64, 72
CB_SQ, CB_SK, CB_SV, CB_GAB = 80, 88, 89, 90
NP_COLS = 92 * LANES
D_FF_PAD = 11264

VMEM_LIMIT = 56 * 1024 * 1024


def _cparams(sem, vmem=VMEM_LIMIT):
    return pltpu.CompilerParams(dimension_semantics=sem, vmem_limit_bytes=vmem)


def _bdot(a, b):
    return jnp.dot(a.astype(BF16), b.astype(BF16), preferred_element_type=F32)


def _bdot_nt(a, b):
    return lax.dot_general(a.astype(BF16), b.astype(BF16), (((1,), (1,)), ((), ())),
                           preferred_element_type=F32)


def _sigmoid(x):
    return 1.0 / (1.0 + jnp.exp(-x))


def _silu(x):
    return x * _sigmoid(x)


def _rms_rows(x, g):
    ms = jnp.mean(x * x, axis=-1, keepdims=True)
    return x * lax.rsqrt(ms + EPS) * g


def _in_proj_kernel(x_ref, g_ref, w_ref, o_ref, hn_ref):
    @pl.when(pl.program_id(1) == 0)
    def _():
        hn_ref[...] = _rms_rows(x_ref[...], g_ref[...]).astype(BF16)

    o_ref[...] = jnp.dot(hn_ref[...], w_ref[...], preferred_element_type=F32).astype(o_ref.dtype)


def _in_proj(x, g, w, tm, tn):
    t, d = x.shape
    n = w.shape[1]
    tm = min(tm, t)
    return pl.pallas_call(
        _in_proj_kernel,
        grid=(t // tm, n // tn),
        in_specs=[pl.BlockSpec((tm, d), lambda i, j: (i, 0)),
                  pl.BlockSpec((1, d), lambda i, j: (0, 0)),
                  pl.BlockSpec((d, tn), lambda i, j: (0, j))],
        out_specs=pl.BlockSpec((tm, tn), lambda i, j: (i, j)),
        out_shape=jax.ShapeDtypeStruct((t, n), F32),
        scratch_shapes=[pltpu.VMEM((tm, d), BF16)],
        compiler_params=_cparams(("parallel", "arbitrary")),
        name="in_proj",
    )(x, g, w)


def _ffn_up_kernel(x_ref, g_ref, wg_ref, wu_ref, o_ref, hn_ref):
    @pl.when(pl.program_id(1) == 0)
    def _():
        hn_ref[...] = _rms_rows(x_ref[...], g_ref[...]).astype(BF16)

    h = hn_ref[...]
    a = jnp.dot(h, wg_ref[...], preferred_element_type=F32)
    b = jnp.dot(h, wu_ref[...], preferred_element_type=F32)
    o_ref[...] = (_silu(a) * b).astype(o_ref.dtype)


def _ffn_up(x, g, wg, wu, tm, tn):
    t, d = x.shape
    n = wg.shape[1]
    tm = min(tm, t)
    return pl.pallas_call(
        _ffn_up_kernel,
        grid=(t // tm, n // tn),
        in_specs=[pl.BlockSpec((tm, d), lambda i, j: (i, 0)),
                  pl.BlockSpec((1, d), lambda i, j: (0, 0)),
                  pl.BlockSpec((d, tn), lambda i, j: (0, j)),
                  pl.BlockSpec((d, tn), lambda i, j: (0, j))],
        out_specs=pl.BlockSpec((tm, tn), lambda i, j: (i, j)),
        out_shape=jax.ShapeDtypeStruct((t, n), BF16),
        scratch_shapes=[pltpu.VMEM((tm, d), BF16)],
        compiler_params=_cparams(("parallel", "arbitrary")),
        name="ffn_up",
    )(x, g, wg, wu)


def _mm_res_kernel(a_ref, w_ref, r_ref, o_ref, acc_ref):
    k = pl.program_id(2)

    @pl.when(k == 0)
    def _():
        acc_ref[...] = jnp.zeros_like(acc_ref)

    acc_ref[...] += jnp.dot(a_ref[...], w_ref[...], preferred_element_type=F32)

    @pl.when(k == pl.num_programs(2) - 1)
    def _():
        o_ref[...] = r_ref[...] + acc_ref[...]


def _mm_res(a, w, r, tm, tn, tk):
    t, kd = a.shape
    n = w.shape[1]
    tm = min(tm, t)
    return pl.pallas_call(
        _mm_res_kernel,
        grid=(t // tm, n // tn, kd // tk),
        in_specs=[pl.BlockSpec((tm, tk), lambda i, j, k: (i, k)),
                  pl.BlockSpec((tk, tn), lambda i, j, k: (k, j)),
                  pl.BlockSpec((tm, tn), lambda i, j, k: (i, j))],
        out_specs=pl.BlockSpec((tm, tn), lambda i, j, k: (i, j)),
        out_shape=jax.ShapeDtypeStruct((t, n), F32),
        scratch_shapes=[pltpu.VMEM((tm, tn), F32)],
        compiler_params=_cparams(("parallel", "parallel", "arbitrary")),
        name="mm_res",
    )(a, w, r)


def _out_proj_kernel(a0_ref, a1_ref, a2_ref, a3_ref, w_ref, r_ref, o_ref):
    acc = r_ref[...]
    for g, a_ref in enumerate((a0_ref, a1_ref, a2_ref, a3_ref)):
        acc = acc + jnp.dot(a_ref[...], w_ref[g * GROUP:(g + 1) * GROUP, :], preferred_element_type=F32)
    o_ref[...] = acc


def _out_proj(mix, w, r, tm, tn):
    t = r.shape[0]
    n = w.shape[1]
    tm = min(tm, t)
    a_spec = pl.BlockSpec((tm, GROUP), lambda i, j: (i, 0))
    return pl.pallas_call(
        _out_proj_kernel,
        grid=(t // tm, n // tn),
        in_specs=[a_spec, a_spec, a_spec, a_spec,
                  pl.BlockSpec((4 * GROUP, tn), lambda i, j: (0, j)),
                  pl.BlockSpec((tm, tn), lambda i, j: (i, j))],
        out_specs=pl.BlockSpec((tm, tn), lambda i, j: (i, j)),
        out_shape=jax.ShapeDtypeStruct((t, n), F32),
        compiler_params=_cparams(("parallel", "arbitrary")),
        name="out_proj",
    )(*mix, w, r)


def _moba_prep_kernel(k_ref, v_ref, g_ref, kn_ref, km_ref, vt_ref):
    g = g_ref[...]
    for h in range(N_HEADS):
        sl = slice(h * HEAD_DIM, (h + 1) * HEAD_DIM)
        kn = _rms_rows(k_ref[:, sl], g)
        kn_ref[:, sl] = kn.astype(BF16)
        km_ref[0, :, sl] = jnp.mean(kn, axis=0, keepdims=True)
        vt_ref[0, sl, :] = v_ref[:, sl].T.astype(BF16)


def _moba_prep(proj, gk):
    t = proj.shape[0]
    nblk = t // MOBA_BLOCK
    return pl.pallas_call(
        _moba_prep_kernel,
        grid=(nblk,),
        in_specs=[pl.BlockSpec((MOBA_BLOCK, GROUP), lambda i: (i, CB_MK // 8)),
                  pl.BlockSpec((MOBA_BLOCK, GROUP), lambda i: (i, CB_MV // 8)),
                  pl.BlockSpec((1, HEAD_DIM), lambda i: (0, 0))],
        out_specs=[pl.BlockSpec((MOBA_BLOCK, GROUP), lambda i: (i, 0)),
                   pl.BlockSpec((1, 1, GROUP), lambda i: (i, 0, 0)),
                   pl.BlockSpec((1, GROUP, MOBA_BLOCK), lambda i: (i, 0, 0))],
        out_shape=[jax.ShapeDtypeStruct((t, GROUP), BF16),
                   jax.ShapeDtypeStruct((nblk, 1, GROUP), F32),
                   jax.ShapeDtypeStruct((nblk, GROUP, MOBA_BLOCK), BF16)],
        compiler_params=_cparams(("parallel",)),
        name="moba_prep",
    )(proj, proj, gk)


def _moba_kernel(q_ref, k_ref, vt_ref, km_ref, g_ref, o_ref, bias_ref, *, nb):
    i = pl.program_id(2)
    blk = MOBA_BLOCK
    qn = _rms_rows(q_ref[...], g_ref[...])

    gate = lax.dot_general(km_ref[0], qn, (((1,), (1,)), ((), ())),
                           precision=lax.Precision.HIGHEST, preferred_element_type=F32)
    row = lax.broadcasted_iota(jnp.int32, gate.shape, 0)
    rowf = row.astype(F32)
    gate = jnp.where(row < i, gate, -jnp.inf)
    bias = jnp.full(gate.shape, NEG, F32)
    for _ in range(MOBA_TOPK):
        m = jnp.max(gate, axis=0, keepdims=True)
        idx = jnp.min(jnp.where(gate == m, rowf, float(nb)), axis=0, keepdims=True)
        hit = jnp.logical_and(rowf == idx, m > -jnp.inf)
        bias = jnp.where(hit, 0.0, bias)
        gate = jnp.where(hit, -jnp.inf, gate)
    bias_ref[...] = bias

    qst = (qn * (HEAD_DIM ** -0.5)).T.astype(BF16)

    kd = k_ref[pl.ds(pl.multiple_of(i * blk, blk), blk), :]
    s = jnp.dot(kd, qst, preferred_element_type=F32)
    kr = lax.broadcasted_iota(jnp.int32, s.shape, 0)
    qc = lax.broadcasted_iota(jnp.int32, s.shape, 1)
    s = jnp.where(kr <= qc, s, NEG)
    m0 = jnp.max(s, axis=0, keepdims=True)
    p = jnp.exp(s - m0)
    l0 = jnp.sum(p, axis=0, keepdims=True)
    acc0 = jnp.dot(vt_ref[i], p.astype(BF16), preferred_element_type=F32)

    def body(j, carry):
        m, l, acc = carry
        kj = k_ref[pl.ds(pl.multiple_of(j * blk, blk), blk), :]
        s = jnp.dot(kj, qst, preferred_element_type=F32) + bias_ref[pl.ds(j, 1), :]
        m_new = jnp.maximum(m, jnp.max(s, axis=0, keepdims=True))
        alpha = jnp.exp(m - m_new)
        p = jnp.exp(s - m_new)
        l = l * alpha + jnp.sum(p, axis=0, keepdims=True)
        acc = acc * alpha + jnp.dot(vt_ref[j], p.astype(BF16), preferred_element_type=F32)
        return m_new, l, acc

    _, l, acc = lax.fori_loop(0, i, body, (m0, l0, acc0))
    o_ref[...] = (acc / l).T.astype(o_ref.dtype)


def _moba(proj, kn, km, vt, gq, batch):
    t = proj.shape[0]
    s = t // batch
    nb = s // MOBA_BLOCK
    km = km.reshape(batch, nb, GROUP)
    return pl.pallas_call(
        functools.partial(_moba_kernel, nb=nb),
        grid=(batch, N_HEADS, nb),
        in_specs=[pl.BlockSpec((MOBA_BLOCK, HEAD_DIM), lambda b, h, i: (b * nb + i, CB_MQ + h)),
                  pl.BlockSpec((s, HEAD_DIM), lambda b, h, i: (b, h)),
                  pl.BlockSpec((nb, HEAD_DIM, MOBA_BLOCK), lambda b, h, i: (b, h, 0)),
                  pl.BlockSpec((1, nb, HEAD_DIM), lambda b, h, i: (b, 0, h)),
                  pl.BlockSpec((1, HEAD_DIM), lambda b, h, i: (0, 0))],
        out_specs=pl.BlockSpec((MOBA_BLOCK, HEAD_DIM), lambda b, h, i: (b * nb + i, h)),
        out_shape=jax.ShapeDtypeStruct((t, GROUP), BF16),
        scratch_shapes=[pltpu.VMEM((nb, MOBA_BLOCK), F32)],
        compiler_params=_cparams(("parallel", "parallel", "arbitrary")),
        name="moba",
    )(proj, kn, vt, km, gq)


GDN_ROWS = 256


def _conv_silu(x_ref, halo_ref, w_ref, first):
    r = x_ref.shape[0]
    halo = halo_ref[...] * jnp.where(first, 0.0, 1.0)
    xb = jnp.concatenate([halo, x_ref[...]], axis=0)
    w = w_ref[...]
    out = None
    for tap in range(GDN_CONV):
        sh = GDN_CONV - 1 - tap
        xs = xb if sh == 0 else pltpu.roll(xb, sh, axis=0)
        term = xs[8:8 + r] * w[tap:tap + 1, :]
        out = term if out is None else out + term
    return _silu(out)


def _gdn_kernel(q_ref, k_ref, v_ref, qh_ref, kh_ref, vh_ref, wq_ref, wk_ref, wv_ref,
                gab_ref, z_ref, hp_ref, gn_ref, o_ref, s_ref):
    h = pl.program_id(1)
    step = pl.program_id(2)
    first = step == 0
    r = GDN_ROWS
    c = GDN_CHUNK
    pair = 2 * c

    @pl.when(first)
    def _():
        s_ref[...] = jnp.zeros_like(s_ref)

    q = _conv_silu(q_ref, qh_ref, wq_ref, first)
    k = _conv_silu(k_ref, kh_ref, wk_ref, first)
    v = _conv_silu(v_ref, vh_ref, wv_ref, first)
    q = q * lax.rsqrt(jnp.sum(q * q, axis=-1, keepdims=True) + EPS) * (HEAD_DIM ** -0.5)
    k = k * lax.rsqrt(jnp.sum(k * k, axis=-1, keepdims=True) + EPS)

    gab = gab_ref[...]
    lane = lax.broadcasted_iota(jnp.int32, gab.shape, 1)
    ga = jnp.sum(jnp.where(lane == h, gab, 0.0), axis=1, keepdims=True)
    gb = jnp.sum(jnp.where(lane == N_HEADS + h, gab, 0.0), axis=1, keepdims=True)
    a_log = hp_ref[0, 0:1, :]
    dt_bias = hp_ref[0, 1:2, :]
    xg = ga + dt_bias
    softplus = jnp.maximum(xg, 0.0) + jnp.log1p(jnp.exp(-jnp.abs(xg)))
    g = -jnp.exp(a_log) * softplus
    beta = _sigmoid(gb + jnp.zeros_like(xg))

    row = lax.broadcasted_iota(jnp.int32, (pair, pair), 0)
    col = lax.broadcasted_iota(jnp.int32, (pair, pair), 1)
    same = (row // c) == (col // c)
    tril = jnp.logical_and(same, row >= col)
    strict = jnp.logical_and(same, row > col)
    eye = (row == col).astype(F32)
    rin = row % c

    state = s_ref[...]
    for pi in range(r // pair):
        rs = slice(pi * pair, (pi + 1) * pair)
        qp, kp, vp = q[rs], k[rs], v[rs]
        bp = beta[rs]
        gcum = g[rs]
        sh = 1
        while sh < c:
            gcum = gcum + jnp.where(rin >= sh, pltpu.roll(gcum, sh, axis=0), 0.0)
            sh *= 2
        decay = jnp.exp(jnp.where(tril, gcum - gcum.T, -jnp.inf))
        eg = jnp.exp(gcum)
        g_end0 = gcum[c - 1:c, :]
        g_end1 = gcum[pair - 1:pair, :]
        gl = jnp.where(row < c, g_end0, g_end1)
        kb = kp * bp
        vb = vp * bp
        kkt = _bdot_nt(kb, kp) * decay
        lmat = jnp.where(strict, kkt, 0.0)
        tinv = eye - lmat
        lpow = lmat
        span = 1
        while 2 * span < c:
            lpow = _bdot(lpow, lpow)
            tinv = tinv + _bdot(tinv, lpow)
            span *= 2
        uw = _bdot(tinv, jnp.concatenate([vb, kb * eg], axis=1))
        u, w = uw[:, :HEAD_DIM], uw[:, HEAD_DIM:]
        qk = _bdot_nt(qp, kp) * decay
        qd = qp * eg
        kdt = (kp * jnp.exp(gl - gcum)).T

        o_parts, vn_parts = [], []
        for ci in range(2):
            cs = slice(ci * c, (ci + 1) * c)
            ws = _bdot(jnp.concatenate([w[cs], qd[cs]], axis=0), state)
            vn = u[cs] - ws[:c]
            o_parts.append(ws[c:])
            vn_parts.append(vn)
            zero = jnp.zeros_like(vn)
            vn_pad = jnp.concatenate([vn, zero] if ci == 0 else [zero, vn], axis=0)
            g_end = g_end0 if ci == 0 else g_end1
            state = state * jnp.exp(g_end) + _bdot(kdt, vn_pad)
        o = jnp.concatenate(o_parts, axis=0) + _bdot(qk, jnp.concatenate(vn_parts, axis=0))
        on = _rms_rows(o, gn_ref[...])
        o_ref[rs, :] = (on * _silu(z_ref[rs, :])).astype(o_ref.dtype)
    s_ref[...] = state


def _gdn(proj, conv_w, hp, gn, batch):
    t = proj.shape[0]
    s = t // batch
    r = GDN_ROWS
    steps = s // r
    hb = r // 8

    def main(cb):
        return pl.BlockSpec((r, HEAD_DIM), lambda b, h, i: (b * steps + i, cb + h))

    def halo(cb):
        return pl.BlockSpec((8, HEAD_DIM), lambda b, h, i: (jnp.maximum((b * steps + i) * hb - 1, 0), cb + h))

    def wspec(off):
        return pl.BlockSpec((GDN_CONV, HEAD_DIM), lambda b, h, i: (0, off + h))

    return pl.pallas_call(
        _gdn_kernel,
        grid=(batch, N_HEADS, steps),
        in_specs=[main(CB_GQ), main(CB_GK), main(CB_GV), halo(CB_GQ), halo(CB_GK), halo(CB_GV),
                  wspec(0), wspec(N_HEADS), wspec(2 * N_HEADS),
                  pl.BlockSpec((r, LANES), lambda b, h, i: (b * steps + i, CB_GAB)),
                  main(CB_GZ),
                  pl.BlockSpec((1, 2, LANES), lambda b, h, i: (h, 0, 0)),
                  pl.BlockSpec((1, HEAD_DIM), lambda b, h, i: (0, 0))],
        out_specs=pl.BlockSpec((r, HEAD_DIM), lambda b, h, i: (b * steps + i, h)),
        out_shape=jax.ShapeDtypeStruct((t, GROUP), BF16),
        scratch_shapes=[pltpu.VMEM((HEAD_DIM, HEAD_DIM), F32)],
        compiler_params=_cparams(("parallel", "parallel", "arbitrary")),
        name="gdn",
    )(proj, proj, proj, proj, proj, proj, conv_w, conv_w, conv_w, proj, proj, hp, gn)


SC_ROWS = 512


def _sconv_kernel(b_ref, c_ref, x_ref, ch_ref, xh_ref, w_ref, o_ref, *, steps):
    first = pl.program_id(0) % steps == 0
    r = b_ref.shape[0]
    y = c_ref[...] * x_ref[...]
    yh = ch_ref[...] * xh_ref[...] * jnp.where(first, 0.0, 1.0)
    yb = jnp.concatenate([yh, y], axis=0)
    w = w_ref[...]
    out = None
    for tap in range(SC_CONV):
        sh = SC_CONV - 1 - tap
        ys = yb if sh == 0 else pltpu.roll(yb, sh, axis=0)
        term = ys[8:8 + r] * w[tap:tap + 1, :]
        out = term if out is None else out + term
    o_ref[...] = (b_ref[...] * out).astype(o_ref.dtype)


def _sconv(proj, w, batch):
    t = proj.shape[0]
    r = min(SC_ROWS, t // batch)
    steps = (t // batch) // r
    hb = r // 8

    def main(cb):
        return pl.BlockSpec((r, GROUP), lambda i: (i, cb // 8))

    def halo(cb):
        return pl.BlockSpec((8, GROUP), lambda i: (jnp.maximum(i * hb - 1, 0), cb // 8))

    return pl.pallas_call(
        functools.partial(_sconv_kernel, steps=steps),
        grid=(t // r,),
        in_specs=[main(CB_SCB), main(CB_SCC), main(CB_SCX), halo(CB_SCC), halo(CB_SCX),
                  pl.BlockSpec((SC_CONV, GROUP), lambda i: (0, 0))],
        out_specs=pl.BlockSpec((r, GROUP), lambda i: (i, 0)),
        out_shape=jax.ShapeDtypeStruct((t, GROUP), BF16),
        compiler_params=_cparams(("parallel",)),
        name="sconv",
    )(proj, proj, proj, proj, proj, w)


SWA_ROWS = 512


def _half_rms(x, g2):
    lane = lax.broadcasted_iota(jnp.int32, x.shape, 1)
    lo = lane < SWA_D
    x2 = x * x
    ms_lo = jnp.sum(jnp.where(lo, x2, 0.0), axis=-1, keepdims=True) * (1.0 / SWA_D)
    ms_hi = jnp.sum(jnp.where(lo, 0.0, x2), axis=-1, keepdims=True) * (1.0 / SWA_D)
    rs = jnp.where(lo, lax.rsqrt(ms_lo + EPS), lax.rsqrt(ms_hi + EPS))
    return x * rs * g2


def _swa_kernel(q_ref, k_ref, v_ref, kh_ref, vh_ref, gq_ref, gk_ref, sink_ref, o_ref, *, steps):
    first = pl.program_id(0) % steps == 0
    r = q_ref.shape[0]
    w = SWA_W
    nsub = r // w
    pairs = SWA_Q_HEADS // 2
    lane = lax.broadcasted_iota(jnp.int32, (r + w, LANES), 1)

    kn = _half_rms(jnp.concatenate([kh_ref[...], k_ref[...]], axis=0), gk_ref[...])
    kroll = pltpu.roll(kn, SWA_D, axis=1)
    kdup = (jnp.where(lane < SWA_D, kn, kroll).astype(BF16),
            jnp.where(lane < SWA_D, kroll, kn).astype(BF16))
    vt = jnp.concatenate([vh_ref[...], v_ref[...]], axis=0).T.astype(BF16)

    kr = lax.broadcasted_iota(jnp.int32, (2 * w, 2 * w), 0)
    qc = lax.broadcasted_iota(jnp.int32, (2 * w, 2 * w), 1) % w
    band = jnp.logical_and(kr > qc, kr <= qc + w)
    band0 = jnp.logical_and(band, kr >= jnp.where(first, w, 0))
    qlane = lax.broadcasted_iota(jnp.int32, (w, LANES), 1)

    for sub in range(nsub):
        mask = band0 if sub == 0 else band
        ks = slice(sub * w, sub * w + 2 * w)
        for c in range(pairs):
            kv = c // (pairs // SWA_KV_HEADS)
            qn = _half_rms(q_ref[sub * w:(sub + 1) * w, c * LANES:(c + 1) * LANES], gq_ref[...]) * (SWA_D ** -0.5)
            qa = jnp.where(qlane < SWA_D, qn, 0.0)
            qb = jnp.where(qlane < SWA_D, 0.0, qn)
            qt = jnp.concatenate([qa.T, qb.T], axis=1).astype(BF16)
            s = jnp.dot(kdup[kv][ks], qt, preferred_element_type=F32)
            s = jnp.where(mask, s, NEG)
            sink = sink_ref[c:c + 1, :]
            m = jnp.maximum(jnp.max(s, axis=0, keepdims=True), sink)
            p = jnp.exp(s - m)
            l = jnp.sum(p, axis=0, keepdims=True) + jnp.exp(sink - m)
            pn = (p * (1.0 / l)).astype(BF16)
            ot = jnp.dot(vt[kv * SWA_D:(kv + 1) * SWA_D, ks], pn, preferred_element_type=F32)
            o = jnp.concatenate([ot[:, :w], ot[:, w:]], axis=0).T
            o_ref[sub * w:(sub + 1) * w, c * LANES:(c + 1) * LANES] = o.astype(o_ref.dtype)


def _swa(proj, gq2, gk2, sinkrow, batch):
    t = proj.shape[0]
    r = min(SWA_ROWS, t // batch)
    steps = (t // batch) // r
    hb = r // SWA_W

    def halo(cb):
        return pl.BlockSpec((SWA_W, LANES), lambda i: (jnp.maximum(i * hb - 1, 0), cb))

    return pl.pallas_call(
        functools.partial(_swa_kernel, steps=steps),
        grid=(t // r,),
        in_specs=[pl.BlockSpec((r, GROUP), lambda i: (i, CB_SQ // 8)),
                  pl.BlockSpec((r, LANES), lambda i: (i, CB_SK)),
                  pl.BlockSpec((r, LANES), lambda i: (i, CB_SV)),
                  halo(CB_SK), halo(CB_SV),
                  pl.BlockSpec((1, LANES), lambda i: (0, 0)),
                  pl.BlockSpec((1, LANES), lambda i: (0, 0)),
                  pl.BlockSpec((SWA_Q_HEADS // 2, 2 * SWA_W), lambda i: (0, 0))],
        out_specs=pl.BlockSpec((r, GROUP), lambda i: (i, 0)),
        out_shape=jax.ShapeDtypeStruct((t, GROUP), BF16),
        compiler_params=_cparams(("parallel",)),
        name="swa",
    )(proj, proj, proj, proj, proj, gq2, gk2, sinkrow)


def _regroup_w_in(w):
    d = w.shape[0]
    a0 = 6 * GROUP
    a1 = a0 + 2 * N_HEADS
    pad = jnp.zeros((d, NP_COLS - w.shape[1]), w.dtype)
    return jnp.concatenate([w[:, :a0], w[:, a1:], w[:, a0:a1], pad], axis=1).astype(BF16)


def _layer(x, batch, norm_mix, w_in, moba_q_norm, moba_k_norm, gdn_conv, gdn_a_log, gdn_dt_bias,
           gdn_out_norm, sc_conv, swa_q_norm, swa_k_norm, swa_sinks, w_out, norm_ffn, w_gate, w_up, w_down):
    d = x.shape[1]
    row = lambda a: a.reshape(1, -1).astype(F32)

    proj = _in_proj(x, row(norm_mix), _regroup_w_in(w_in), tm=512, tn=512)

    kn, km, vt = _moba_prep(proj, row(moba_k_norm))
    o_a = _moba(proj, kn, km, vt, row(moba_q_norm), batch)

    hp = jnp.broadcast_to(jnp.stack([gdn_a_log, gdn_dt_bias], axis=1)[:, :, None], (N_HEADS, 2, LANES)).astype(F32)
    o_b = _gdn(proj, gdn_conv.astype(F32), hp, row(gdn_out_norm), batch)

    o_c = _sconv(proj, sc_conv.astype(F32), batch)

    sinkrow = jnp.repeat(swa_sinks.astype(F32), SWA_W).reshape(SWA_Q_HEADS // 2, 2 * SWA_W)
    o_d = _swa(proj, row(jnp.tile(swa_q_norm, 2)), row(jnp.tile(swa_k_norm, 2)), sinkrow, batch)

    x = _out_proj((o_a, o_b, o_c, o_d), w_out.astype(BF16), x, tm=1024, tn=512)

    ffpad = D_FF_PAD - w_gate.shape[1]
    wg = jnp.pad(w_gate, ((0, 0), (0, ffpad))).astype(BF16)
    wu = jnp.pad(w_up, ((0, 0), (0, ffpad))).astype(BF16)
    wd = jnp.pad(w_down, ((0, ffpad), (0, 0))).astype(BF16)
    act = _ffn_up(x, row(norm_ffn), wg, wu, tm=512, tn=512)
    return _mm_res(act, wd, x, tm=2048, tn=1024, tk=512)


def kernel(x, norm_mix, w_in, moba_q_norm, moba_k_norm, gdn_conv, gdn_a_log, gdn_dt_bias, gdn_out_norm, sc_conv, swa_q_norm, swa_k_norm, swa_sinks, w_out, norm_ffn, w_gate, w_up, w_down):
    batch, seq, d = x.shape
    h = x.reshape(batch * seq, d)
    for l in range(norm_mix.shape[0]):
        h = _layer(h, batch, norm_mix[l], w_in[l], moba_q_norm[l], moba_k_norm[l], gdn_conv[l], gdn_a_log[l],
                   gdn_dt_bias[l], gdn_out_norm[l], sc_conv[l], swa_q_norm[l], swa_k_norm[l], swa_sinks[l],
                   w_out[l], norm_ffn[l], w_gate[l], w_up[l], w_down[l])
    return h.reshape(batch, seq, d)
```

```python
import functools

import jax
import jax.numpy as jnp
from jax import lax
from jax.experimental import pallas as pl
from jax.experimental.pallas import tpu as pltpu

F32 = jnp.float32
BF16 = jnp.bfloat16

EPS = 1e-6
LANES = 128
GROUP = 1024
HEAD_DIM = 128
N_HEADS = GROUP // HEAD_DIM
MOBA_BLOCK = 256
MOBA_TILE = 2 * MOBA_BLOCK
MOBA_TOPK = 3
GDN_CONV = 4
GDN_CHUNK = 64
SC_CONV = 3
SWA_D = 64
SWA_Q_HEADS = GROUP // SWA_D
SWA_KV_HEADS = 2
SWA_W = 128
NEG = -1e30

CB_MQ, CB_MK, CB_MV = 0, 8, 16
CB_GQ, CB_GK, CB_GV, CB_GZ = 24, 32, 40, 48
CB_SCB, CB_SCC, CB_SCX = 56, 64, 72
CB_SQ, CB_SK, CB_SV, CB_GAB = 80, 88, 89, 90
NP_COLS = 92 * LANES
D_FF_PAD = 11264

VMEM_LIMIT = 56 * 1024 * 1024


def _cparams(sem, vmem=VMEM_LIMIT):
    return pltpu.CompilerParams(dimension_semantics=sem, vmem_limit_bytes=vmem)


def _bdot(a, b):
    return jnp.dot(a.astype(BF16), b.astype(BF16), preferred_element_type=F32)


def _bdot_nt(a, b):
    return lax.dot_general(a.astype(BF16), b.astype(BF16), (((1,), (1,)), ((), ())),
                           preferred_element_type=F32)


def _sigmoid(x):
    return 1.0 / (1.0 + jnp.exp(-x))


def _silu(x):
    return x * _sigmoid(x)


def _rms_rows(x, g):
    ms = jnp.mean(x * x, axis=-1, keepdims=True)
    return x * lax.rsqrt(ms + EPS) * g


def _in_proj_kernel(x_ref, g_ref, w_ref, o_ref, hn_ref):
    @pl.when(pl.program_id(1) == 0)
    def _():
        hn_ref[...] = _rms_rows(x_ref[...], g_ref[...]).astype(BF16)

    o_ref[...] = jnp.dot(hn_ref[...], w_ref[...], preferred_element_type=F32).astype(o_ref.dtype)


def _in_proj(x, g, w, layer, tm, tn):
    t, d = x.shape
    n = w.shape[2]
    tm = min(tm, t)
    return pl.pallas_call(
        _in_proj_kernel,
        grid=(t // tm, n // tn),
        in_specs=[pl.BlockSpec((tm, d), lambda i, j: (i, 0)),
                  pl.BlockSpec((1, d), lambda i, j: (0, 0)),
                  pl.BlockSpec((None, d, tn), lambda i, j: (layer, 0, j))],
        out_specs=pl.BlockSpec((tm, tn), lambda i, j: (i, j)),
        out_shape=jax.ShapeDtypeStruct((t, n), F32),
        scratch_shapes=[pltpu.VMEM((tm, d), BF16)],
        compiler_params=_cparams(("parallel", "arbitrary")),
        name="in_proj",
    )(x, g, w)


def _ffn_up_kernel(x_ref, g_ref, wg_ref, wu_ref, o_ref, hn_ref):
    @pl.when(pl.program_id(1) == 0)
    def _():
        hn_ref[...] = _rms_rows(x_ref[...], g_ref[...]).astype(BF16)

    h = hn_ref[...]
    a = jnp.dot(h, wg_ref[...], preferred_element_type=F32)
    b = jnp.dot(h, wu_ref[...], preferred_element_type=F32)
    o_ref[...] = (_silu(a) * b).astype(o_ref.dtype)


def _ffn_up(x, g, wg, wu, layer, tm, tn):
    t, d = x.shape
    n = wg.shape[2]
    tm = min(tm, t)
    w_spec = pl.BlockSpec((None, d, tn), lambda i, j: (layer, 0, j))
    return pl.pallas_call(
        _ffn_up_kernel,
        grid=(t // tm, n // tn),
        in_specs=[pl.BlockSpec((tm, d), lambda i, j: (i, 0)),
                  pl.BlockSpec((1, d), lambda i, j: (0, 0)),
                  w_spec, w_spec],
        out_specs=pl.BlockSpec((tm, tn), lambda i, j: (i, j)),
        out_shape=jax.ShapeDtypeStruct((t, n), BF16),
        scratch_shapes=[pltpu.VMEM((tm, d), BF16)],
        compiler_params=_cparams(("parallel", "arbitrary")),
        name="ffn_up",
    )(x, g, wg, wu)


def _mm_res_kernel(a_ref, w_ref, r_ref, o_ref, acc_ref):
    k = pl.program_id(2)

    @pl.when(k == 0)
    def _():
        acc_ref[...] = jnp.zeros_like(acc_ref)

    acc_ref[...] += jnp.dot(a_ref[...], w_ref[...], preferred_element_type=F32)

    @pl.when(k == pl.num_programs(2) - 1)
    def _():
        o_ref[...] = r_ref[...] + acc_ref[...]


def _mm_res(a, w, r, layer, tm, tn, tk):
    t, kd = a.shape
    n = w.shape[2]
    tm = min(tm, t)
    return pl.pallas_call(
        _mm_res_kernel,
        grid=(t // tm, n // tn, kd // tk),
        in_specs=[pl.BlockSpec((tm, tk), lambda i, j, k: (i, k)),
                  pl.BlockSpec((None, tk, tn), lambda i, j, k: (layer, k, j)),
                  pl.BlockSpec((tm, tn), lambda i, j, k: (i, j))],
        out_specs=pl.BlockSpec((tm, tn), lambda i, j, k: (i, j)),
        out_shape=jax.ShapeDtypeStruct((t, n), F32),
        scratch_shapes=[pltpu.VMEM((tm, tn), F32)],
        compiler_params=_cparams(("parallel", "parallel", "arbitrary")),
        name="mm_res",
    )(a, w, r)


def _out_proj_kernel(a0_ref, a1_ref, a2_ref, a3_ref, w_ref, r_ref, o_ref):
    acc = r_ref[...]
    for g, a_ref in enumerate((a0_ref, a1_ref, a2_ref, a3_ref)):
        acc = acc + jnp.dot(a_ref[...], w_ref[g * GROUP:(g + 1) * GROUP, :], preferred_element_type=F32)
    o_ref[...] = acc


def _out_proj(mix, w, r, layer, tm, tn):
    t = r.shape[0]
    n = w.shape[2]
    tm = min(tm, t)
    a_spec = pl.BlockSpec((tm, GROUP), lambda i, j: (i, 0))
    return pl.pallas_call(
        _out_proj_kernel,
        grid=(t // tm, n // tn),
        in_specs=[a_spec, a_spec, a_spec, a_spec,
                  pl.BlockSpec((None, 4 * GROUP, tn), lambda i, j: (layer, 0, j)),
                  pl.BlockSpec((tm, tn), lambda i, j: (i, j))],
        out_specs=pl.BlockSpec((tm, tn), lambda i, j: (i, j)),
        out_shape=jax.ShapeDtypeStruct((t, n), F32),
        compiler_params=_cparams(("parallel", "arbitrary")),
        name="out_proj",
    )(*mix, w, r)


def _moba_prep_kernel(k_ref, v_ref, g_ref, kn_ref, km_ref, vt_ref):
    g = g_ref[...]
    for h in range(N_HEADS):
        sl = slice(h * HEAD_DIM, (h + 1) * HEAD_DIM)
        kn = _rms_rows(k_ref[:, sl], g)
        kn_ref[:, sl] = kn.astype(BF16)
        for half in range(MOBA_TILE // MOBA_BLOCK):
            km_ref[half, :, sl] = jnp.mean(kn[half * MOBA_BLOCK:(half + 1) * MOBA_BLOCK], axis=0, keepdims=True)
        vt_ref[0, sl, :] = v_ref[:, sl].T.astype(BF16)


def _moba_prep(proj, gk):
    t = proj.shape[0]
    ntile = t // MOBA_TILE
    per = MOBA_TILE // MOBA_BLOCK
    return pl.pallas_call(
        _moba_prep_kernel,
        grid=(ntile,),
        in_specs=[pl.BlockSpec((MOBA_TILE, GROUP), lambda i: (i, CB_MK // 8)),
                  pl.BlockSpec((MOBA_TILE, GROUP), lambda i: (i, CB_MV // 8)),
                  pl.BlockSpec((1, HEAD_DIM), lambda i: (0, 0))],
        out_specs=[pl.BlockSpec((MOBA_TILE, GROUP), lambda i: (i, 0)),
                   pl.BlockSpec((per, 1, GROUP), lambda i: (i, 0, 0)),
                   pl.BlockSpec((1, GROUP, MOBA_TILE), lambda i: (i, 0, 0))],
        out_shape=[jax.ShapeDtypeStruct((t, GROUP), BF16),
                   jax.ShapeDtypeStruct((ntile * per, 1, GROUP), F32),
                   jax.ShapeDtypeStruct((ntile, GROUP, MOBA_TILE), BF16)],
        compiler_params=_cparams(("parallel",)),
        name="moba_prep",
    )(proj, proj, gk)


def _moba_kernel(q_ref, k_ref, vt_ref, km_ref, g_ref, o_ref, bias_ref, acc_ref, *, nb):
    ti = pl.program_id(2)
    blk, tile = MOBA_BLOCK, MOBA_TILE
    qn = _rms_rows(q_ref[...], g_ref[...])

    gate = lax.dot_general(km_ref[0], qn, (((1,), (1,)), ((), ())),
                           precision=lax.Precision.HIGHEST, preferred_element_type=F32)
    row = lax.broadcasted_iota(jnp.int32, gate.shape, 0)
    own = 2 * ti + (lax.broadcasted_iota(jnp.int32, gate.shape, 1) >= blk).astype(jnp.int32)
    rowf = row.astype(F32)
    gate = jnp.where(row < own, gate, -jnp.inf)
    bias = jnp.where(row == own, 0.0, NEG)
    for _ in range(MOBA_TOPK):
        m = jnp.max(gate, axis=0, keepdims=True)
        idx = jnp.min(jnp.where(gate == m, rowf, float(nb)), axis=0, keepdims=True)
        hit = jnp.logical_and(rowf == idx, m > -jnp.inf)
        bias = jnp.where(hit, 0.0, bias)
        gate = jnp.where(hit, -jnp.inf, gate)
    bias_ref[...] = bias

    qst = (qn * (HEAD_DIM ** -0.5)).T.astype(BF16)

    def scores(t):
        kt = k_ref[pl.ds(pl.multiple_of(t * tile, tile), tile), :]
        s = jnp.dot(kt, qst, preferred_element_type=F32)
        b0 = bias_ref[pl.ds(2 * t, 1), :]
        b1 = bias_ref[pl.ds(2 * t + 1, 1), :]
        return jnp.concatenate([s[:blk] + b0, s[blk:] + b1], axis=0)

    s = scores(ti)
    kr = lax.broadcasted_iota(jnp.int32, s.shape, 0)
    qc = lax.broadcasted_iota(jnp.int32, s.shape, 1)
    s = jnp.where(kr <= qc, s, NEG)
    m0 = jnp.max(s, axis=0, keepdims=True)
    p = jnp.exp(s - m0)
    l0 = jnp.sum(p, axis=0, keepdims=True)
    acc_ref[...] = jnp.dot(vt_ref[ti], p.astype(BF16), preferred_element_type=F32)

    def body(t, carry):
        m, l = carry
        s = scores(t)
        m_new = jnp.maximum(m, jnp.max(s, axis=0, keepdims=True))
        alpha = jnp.exp(m - m_new)
        p = jnp.exp(s - m_new)
        l = l * alpha + jnp.sum(p, axis=0, keepdims=True)
        acc_ref[...] = acc_ref[...] * alpha + jnp.dot(vt_ref[t], p.astype(BF16), preferred_element_type=F32)
        return m_new, l

    _, l = lax.fori_loop(0, ti, body, (m0, l0))
    o_ref[...] = (acc_ref[...] / l).T.astype(o_ref.dtype)


def _moba(proj, kn, km, vt, gq, batch):
    t = proj.shape[0]
    s = t // batch
    nb = s // MOBA_BLOCK
    nt = s // MOBA_TILE
    km = km.reshape(batch, nb, GROUP)
    return pl.pallas_call(
        functools.partial(_moba_kernel, nb=nb),
        grid=(batch, N_HEADS, nt),
        in_specs=[pl.BlockSpec((MOBA_TILE, HEAD_DIM), lambda b, h, i: (b * nt + i, CB_MQ + h)),
                  pl.BlockSpec((s, HEAD_DIM), lambda b, h, i: (b, h)),
                  pl.BlockSpec((nt, HEAD_DIM, MOBA_TILE), lambda b, h, i: (b, h, 0)),
                  pl.BlockSpec((1, nb, HEAD_DIM), lambda b, h, i: (b, 0, h)),
                  pl.BlockSpec((1, HEAD_DIM), lambda b, h, i: (0, 0))],
        out_specs=pl.BlockSpec((MOBA_TILE, HEAD_DIM), lambda b, h, i: (b * nt + i, h)),
        out_shape=jax.ShapeDtypeStruct((t, GROUP), BF16),
        scratch_shapes=[pltpu.VMEM((nb, MOBA_TILE), F32),
                        pltpu.VMEM((HEAD_DIM, MOBA_TILE), F32)],
        compiler_params=_cparams(("parallel", "parallel", "arbitrary")),
        name="moba",
    )(proj, kn, vt, km, gq)


GDN_ROWS = 256
GDN_HPS = 4


def _conv_silu(x_ref, halo_ref, w_ref, first):
    r = x_ref.shape[0]
    halo = halo_ref[...] * jnp.where(first, 0.0, 1.0)
    xb = jnp.concatenate([halo, x_ref[...]], axis=0)
    w = w_ref[...]
    out = None
    for tap in range(GDN_CONV):
        sh = GDN_CONV - 1 - tap
        xs = xb if sh == 0 else pltpu.roll(xb, sh, axis=0)
        term = xs[8:8 + r] * w[tap:tap + 1, :]
        out = term if out is None else out + term
    return _silu(out)


def _gdn_heads(qs, ks, vs, gs, betas, states):
    r = qs[0].shape[0]
    c = GDN_CHUNK
    pair = 2 * c
    npair = r // pair
    nh = len(qs)
    row = lax.broadcasted_iota(jnp.int32, (pair, pair), 0)
    col = lax.broadcasted_iota(jnp.int32, (pair, pair), 1)
    same = (row // c) == (col // c)
    tril = jnp.logical_and(same, row >= col)
    strict = jnp.logical_and(same, row > col)
    eye = (row == col).astype(F32)
    rin = row % c
    units = [(h, pi) for pi in range(npair) for h in range(nh)]

    def rows(x, u):
        return x[u[0]][u[1] * pair:(u[1] + 1) * pair]

    gcum = {u: rows(gs, u) for u in units}
    sh = 1
    while sh < c:
        gcum = {u: gcum[u] + jnp.where(rin >= sh, pltpu.roll(gcum[u], sh, axis=0), 0.0) for u in units}
        sh *= 2
    decay = {u: jnp.exp(jnp.where(tril, gcum[u] - gcum[u].T, -jnp.inf)) for u in units}
    eg = {u: jnp.exp(gcum[u]) for u in units}
    g_end = {u: (gcum[u][c - 1:c, :], gcum[u][pair - 1:pair, :]) for u in units}
    kb = {u: rows(ks, u) * rows(betas, u) for u in units}
    vb = {u: rows(vs, u) * rows(betas, u) for u in units}
    lmat = {u: jnp.where(strict, _bdot_nt(kb[u], rows(ks, u)) * decay[u], 0.0) for u in units}
    qk = {u: _bdot_nt(rows(qs, u), rows(ks, u)) * decay[u] for u in units}
    tinv = {u: eye - lmat[u] for u in units}
    lpow = lmat
    span = 1
    while 2 * span < c:
        lpow = {u: _bdot(lpow[u], lpow[u]) for u in units}
        tinv = {u: tinv[u] + _bdot(tinv[u], lpow[u]) for u in units}
        span *= 2
    uw = {u: _bdot(tinv[u], jnp.concatenate([vb[u], kb[u] * eg[u]], axis=1)) for u in units}
    qd = {u: rows(qs, u) * eg[u] for u in units}
    kdt = {u: (rows(ks, u) * jnp.exp(jnp.where(row < c, g_end[u][0], g_end[u][1]) - gcum[u])).T for u in units}

    states = list(states)
    vns = {u: [] for u in units}
    o_st = {u: [] for u in units}
    for pi in range(npair):
        for ci in range(2):
            cs = slice(ci * c, (ci + 1) * c)
            for h in range(nh):
                u = (h, pi)
                ws = _bdot(jnp.concatenate([uw[u][cs, HEAD_DIM:], qd[u][cs]], axis=0), states[h])
                vn = uw[u][cs, :HEAD_DIM] - ws[:c]
                o_st[u].append(ws[c:])
                vns[u].append(vn)
                zero = jnp.zeros_like(vn)
                vn_pad = jnp.concatenate([vn, zero] if ci == 0 else [zero, vn], axis=0)
                states[h] = states[h] * jnp.exp(g_end[u][ci]) + _bdot(kdt[u], vn_pad)
    outs = []
    for h in range(nh):
        parts = [jnp.concatenate(o_st[(h, pi)], axis=0) + _bdot(qk[(h, pi)], jnp.concatenate(vns[(h, pi)], axis=0))
                 for pi in range(npair)]
        outs.append(jnp.concatenate(parts, axis=0))
    return outs, states


def _gdn_kernel(q_ref, k_ref, v_ref, qh_ref, kh_ref, vh_ref, wq_ref, wk_ref, wv_ref,
                gab_ref, z_ref, hp_ref, gn_ref, o_ref, s_ref):
    hg = pl.program_id(1)
    first = pl.program_id(2) == 0

    @pl.when(first)
    def _():
        s_ref[...] = jnp.zeros_like(s_ref)

    q2 = _conv_silu(q_ref, qh_ref, wq_ref, first)
    k2 = _conv_silu(k_ref, kh_ref, wk_ref, first)
    v2 = _conv_silu(v_ref, vh_ref, wv_ref, first)
    gab = gab_ref[...]
    lane = lax.broadcasted_iota(jnp.int32, gab.shape, 1)

    qs, ks, vs, gs, betas = [], [], [], [], []
    for hh in range(GDN_HPS):
        h = hg * GDN_HPS + hh
        sl = slice(hh * HEAD_DIM, (hh + 1) * HEAD_DIM)
        q, k = q2[:, sl], k2[:, sl]
        qs.append(q * lax.rsqrt(jnp.sum(q * q, axis=-1, keepdims=True) + EPS) * (HEAD_DIM ** -0.5))
        ks.append(k * lax.rsqrt(jnp.sum(k * k, axis=-1, keepdims=True) + EPS))
        vs.append(v2[:, sl])
        ga = jnp.sum(jnp.where(lane == h, gab, 0.0), axis=1, keepdims=True)
        gb = jnp.sum(jnp.where(lane == N_HEADS + h, gab, 0.0), axis=1, keepdims=True)
        a_log = hp_ref[hh, 0:1, :]
        dt_bias = hp_ref[hh, 1:2, :]
        xg = ga + dt_bias
        softplus = jnp.maximum(xg, 0.0) + jnp.log1p(jnp.exp(-jnp.abs(xg)))
        gs.append(-jnp.exp(a_log) * softplus)
        betas.append(_sigmoid(gb + jnp.zeros_like(xg)))

    outs, states = _gdn_heads(qs, ks, vs, gs, betas, [s_ref[hh] for hh in range(GDN_HPS)])
    for hh in range(GDN_HPS):
        sl = slice(hh * HEAD_DIM, (hh + 1) * HEAD_DIM)
        s_ref[hh] = states[hh]
        on = _rms_rows(outs[hh], gn_ref[...])
        o_ref[:, sl] = (on * _silu(z_ref[:, sl])).astype(o_ref.dtype)


def _gdn(proj, conv_w, hp, gn, batch):
    t = proj.shape[0]
    s = t // batch
    r = GDN_ROWS
    steps = s // r
    hb = r // 8
    wide = GDN_HPS * HEAD_DIM
    cpb = GDN_HPS

    def main(cb):
        return pl.BlockSpec((r, wide), lambda b, h, i: (b * steps + i, cb // cpb + h))

    def halo(cb):
        return pl.BlockSpec((8, wide), lambda b, h, i: (jnp.maximum((b * steps + i) * hb - 1, 0), cb // cpb + h))

    def wspec(off):
        return pl.BlockSpec((GDN_CONV, wide), lambda b, h, i: (0, off // cpb + h))

    return pl.pallas_call(
        _gdn_kernel,
        grid=(batch, N_HEADS // GDN_HPS, steps),
        in_specs=[main(CB_GQ), main(CB_GK), main(CB_GV), halo(CB_GQ), halo(CB_GK), halo(CB_GV),
                  wspec(0), wspec(N_HEADS), wspec(2 * N_HEADS),
                  pl.BlockSpec((r, LANES), lambda b, h, i: (b * steps + i, CB_GAB)),
                  main(CB_GZ),
                  pl.BlockSpec((GDN_HPS, 2, LANES), lambda b, h, i: (h, 0, 0)),
                  pl.BlockSpec((1, HEAD_DIM), lambda b, h, i: (0, 0))],
        out_specs=pl.BlockSpec((r, wide), lambda b, h, i: (b * steps + i, h)),
        out_shape=jax.ShapeDtypeStruct((t, GROUP), BF16),
        scratch_shapes=[pltpu.VMEM((GDN_HPS, HEAD_DIM, HEAD_DIM), F32)],
        compiler_params=_cparams(("parallel", "parallel", "arbitrary")),
        name="gdn",
    )(proj, proj, proj, proj, proj, proj, conv_w, conv_w, conv_w, proj, proj, hp, gn)


SC_ROWS = 512


def _sconv_kernel(b_ref, c_ref, x_ref, ch_ref, xh_ref, w_ref, o_ref, *, steps):
    first = pl.program_id(0) % steps == 0
    r = b_ref.shape[0]
    y = c_ref[...] * x_ref[...]
    yh = ch_ref[...] * xh_ref[...] * jnp.where(first, 0.0, 1.0)
    yb = jnp.concatenate([yh, y], axis=0)
    w = w_ref[...]
    out = None
    for tap in range(SC_CONV):
        sh = SC_CONV - 1 - tap
        ys = yb if sh == 0 else pltpu.roll(yb, sh, axis=0)
        term = ys[8:8 + r] * w[tap:tap + 1, :]
        out = term if out is None else out + term
    o_ref[...] = (b_ref[...] * out).astype(o_ref.dtype)


def _sconv(proj, w, batch):
    t = proj.shape[0]
    r = min(SC_ROWS, t // batch)
    steps = (t // batch) // r
    hb = r // 8

    def main(cb):
        return pl.BlockSpec((r, GROUP), lambda i: (i, cb // 8))

    def halo(cb):
        return pl.BlockSpec((8, GROUP), lambda i: (jnp.maximum(i * hb - 1, 0), cb // 8))

    return pl.pallas_call(
        functools.partial(_sconv_kernel, steps=steps),
        grid=(t // r,),
        in_specs=[main(CB_SCB), main(CB_SCC), main(CB_SCX), halo(CB_SCC), halo(CB_SCX),
                  pl.BlockSpec((SC_CONV, GROUP), lambda i: (0, 0))],
        out_specs=pl.BlockSpec((r, GROUP), lambda i: (i, 0)),
        out_shape=jax.ShapeDtypeStruct((t, GROUP), BF16),
        compiler_params=_cparams(("parallel",)),
        name="sconv",
    )(proj, proj, proj, proj, proj, w)


SWA_ROWS = 512


def _half_rms(x, g2):
    lane = lax.broadcasted_iota(jnp.int32, x.shape, 1)
    lo = lane < SWA_D
    x2 = x * x
    ms_lo = jnp.sum(jnp.where(lo, x2, 0.0), axis=-1, keepdims=True) * (1.0 / SWA_D)
    ms_hi = jnp.sum(jnp.where(lo, 0.0, x2), axis=-1, keepdims=True) * (1.0 / SWA_D)
    rs = jnp.where(lo, lax.rsqrt(ms_lo + EPS), lax.rsqrt(ms_hi + EPS))
    return x * rs * g2


def _swa_kernel(q_ref, k_ref, v_ref, kh_ref, vh_ref, gq_ref, gk_ref, sink_ref, o_ref, *, steps):
    first = pl.program_id(0) % steps == 0
    r = q_ref.shape[0]
    w = SWA_W
    nsub = r // w
    pairs = SWA_Q_HEADS // 2
    lane = lax.broadcasted_iota(jnp.int32, (r + w, LANES), 1)

    kn = _half_rms(jnp.concatenate([kh_ref[...], k_ref[...]], axis=0), gk_ref[...])
    kroll = pltpu.roll(kn, SWA_D, axis=1)
    kdup = (jnp.where(lane < SWA_D, kn, kroll).astype(BF16),
            jnp.where(lane < SWA_D, kroll, kn).astype(BF16))
    vt = jnp.concatenate([vh_ref[...], v_ref[...]], axis=0).T.astype(BF16)

    kr = lax.broadcasted_iota(jnp.int32, (2 * w, 2 * w), 0)
    qc = lax.broadcasted_iota(jnp.int32, (2 * w, 2 * w), 1) % w
    band = jnp.logical_and(kr > qc, kr <= qc + w)
    band0 = jnp.logical_and(band, kr >= jnp.where(first, w, 0))
    qlane = lax.broadcasted_iota(jnp.int32, (w, LANES), 1)

    for sub in range(nsub):
        mask = band0 if sub == 0 else band
        ks = slice(sub * w, sub * w + 2 * w)
        for c in range(pairs):
            kv = c // (pairs // SWA_KV_HEADS)
            qn = _half_rms(q_ref[sub * w:(sub + 1) * w, c * LANES:(c + 1) * LANES], gq_ref[...]) * (SWA_D ** -0.5)
            qa = jnp.where(qlane < SWA_D, qn, 0.0)
            qb = jnp.where(qlane < SWA_D, 0.0, qn)
            qt = jnp.concatenate([qa.T, qb.T], axis=1).astype(BF16)
            s = jnp.dot(kdup[kv][ks], qt, preferred_element_type=F32)
            s = jnp.where(mask, s, NEG)
            sink = sink_ref[c:c + 1, :]
            m = jnp.maximum(jnp.max(s, axis=0, keepdims=True), sink)
            p = jnp.exp(s - m)
            l = jnp.sum(p, axis=0, keepdims=True) + jnp.exp(sink - m)
            pn = (p * (1.0 / l)).astype(BF16)
            ot = jnp.dot(vt[kv * SWA_D:(kv + 1) * SWA_D, ks], pn, preferred_element_type=F32)
            o = jnp.concatenate([ot[:, :w], ot[:, w:]], axis=0).T
            o_ref[sub * w:(sub + 1) * w, c * LANES:(c + 1) * LANES] = o.astype(o_ref.dtype)


def _swa(proj, gq2, gk2, sinkrow, batch):
    t = proj.shape[0]
    r = min(SWA_ROWS, t // batch)
    steps = (t // batch) // r
    hb = r // SWA_W

    def halo(cb):
        return pl.BlockSpec((SWA_W, LANES), lambda i: (jnp.maximum(i * hb - 1, 0), cb))

    return pl.pallas_call(
        functools.partial(_swa_kernel, steps=steps),
        grid=(t // r,),
        in_specs=[pl.BlockSpec((r, GROUP), lambda i: (i, CB_SQ // 8)),
                  pl.BlockSpec((r, LANES), lambda i: (i, CB_SK)),
                  pl.BlockSpec((r, LANES), lambda i: (i, CB_SV)),
                  halo(CB_SK), halo(CB_SV),
                  pl.BlockSpec((1, LANES), lambda i: (0, 0)),
                  pl.BlockSpec((1, LANES), lambda i: (0, 0)),
                  pl.BlockSpec((SWA_Q_HEADS // 2, 2 * SWA_W), lambda i: (0, 0))],
        out_specs=pl.BlockSpec((r, GROUP), lambda i: (i, 0)),
        out_shape=jax.ShapeDtypeStruct((t, GROUP), BF16),
        compiler_params=_cparams(("parallel",)),
        name="swa",
    )(proj, proj, proj, proj, proj, gq2, gk2, sinkrow)


def _regroup_w_in(w):
    w = w.astype(BF16)
    a0 = 6 * GROUP
    a1 = a0 + 2 * N_HEADS
    pad = jnp.zeros(w.shape[:2] + (NP_COLS - w.shape[2],), BF16)
    return jnp.concatenate([w[:, :, :a0], w[:, :, a1:], w[:, :, a0:a1], pad], axis=2)


def _layer(x, batch, layer, norm_mix, w_in, moba_q_norm, moba_k_norm, gdn_conv, gdn_a_log, gdn_dt_bias,
           gdn_out_norm, sc_conv, swa_q_norm, swa_k_norm, swa_sinks, w_out, norm_ffn, w_gate, w_up, w_down):
    row = lambda a: a.reshape(1, -1).astype(F32)

    proj = _in_proj(x, row(norm_mix), w_in, layer, tm=512, tn=512)

    kn, km, vt = _moba_prep(proj, row(moba_k_norm))
    o_a = _moba(proj, kn, km, vt, row(moba_q_norm), batch)

    hp = jnp.broadcast_to(jnp.stack([gdn_a_log, gdn_dt_bias], axis=1)[:, :, None], (N_HEADS, 2, LANES)).astype(F32)
    o_b = _gdn(proj, gdn_conv.astype(F32), hp, row(gdn_out_norm), batch)

    o_c = _sconv(proj, sc_conv.astype(F32), batch)

    sinkrow = jnp.repeat(swa_sinks.astype(F32), SWA_W).reshape(SWA_Q_HEADS // 2, 2 * SWA_W)
    o_d = _swa(proj, row(jnp.tile(swa_q_norm, 2)), row(jnp.tile(swa_k_norm, 2)), sinkrow, batch)

    x = _out_proj((o_a, o_b, o_c, o_d), w_out, x, layer, tm=1024, tn=512)

    act = _ffn_up(x, row(norm_ffn), w_gate, w_up, layer, tm=512, tn=512)
    return _mm_res(act, w_down, x, layer, tm=2048, tn=1024, tk=512)


def kernel(x, norm_mix, w_in, moba_q_norm, moba_k_norm, gdn_conv, gdn_a_log, gdn_dt_bias, gdn_out_norm, sc_conv, swa_q_norm, swa_k_norm, swa_sinks, w_out, norm_ffn, w_gate, w_up, w_down):
    batch, seq, d = x.shape
    ffpad = D_FF_PAD - w_gate.shape[2]
    w_in_b = _regroup_w_in(w_in)
    w_out_b = w_out.astype(BF16)
    w_gate_b = jnp.pad(w_gate.astype(BF16), ((0, 0), (0, 0), (0, ffpad)))
    w_up_b = jnp.pad(w_up.astype(BF16), ((0, 0), (0, 0), (0, ffpad)))
    w_down_b = jnp.pad(w_down.astype(BF16), ((0, 0), (0, ffpad), (0, 0)))
    h = x.reshape(batch * seq, d)
    for l in range(norm_mix.shape[0]):
        h = _layer(h, batch, l, norm_mix[l], w_in_b, moba_q_norm[l], moba_k_norm[l], gdn_conv[l], gdn_a_log[l],
                   gdn_dt_bias[l], gdn_out_norm[l], sc_conv[l], swa_q_norm[l], swa_k_norm[l], swa_sinks[l],
                   w_out_b, norm_ffn[l], w_gate_b, w_up_b, w_down_b)
    return h.reshape(batch, seq, d)
```

```python
import functools

import jax
import jax.numpy as jnp
from jax import lax
from jax.experimental import pallas as pl
from jax.experimental.pallas import tpu as pltpu

F32 = jnp.float32
BF16 = jnp.bfloat16

EPS = 1e-6
LANES = 128
GROUP = 1024
HEAD_DIM = 128
N_HEADS = GROUP // HEAD_DIM
MOBA_BLOCK = 256
MOBA_TILE = 2 * MOBA_BLOCK
MOBA_TOPK = 3
GDN_CONV = 4
GDN_CHUNK = 64
SC_CONV = 3
SWA_D = 64
SWA_Q_HEADS = GROUP // SWA_D
SWA_KV_HEADS = 2
SWA_W = 128
NEG = -1e30

CB_MQ, CB_MK, CB_MV = 0, 8, 16
CB_GQ, CB_GK, CB_GV, CB_GZ = 24, 32, 40, 48
CB_SCB, CB_SCC, CB_SCX = 56, 64, 72
CB_SQ, CB_SK, CB_SV, CB_GAB = 80, 88, 89, 90
NP_COLS = 92 * LANES
A_COLS = 6 * GROUP
D_FF_PAD = 11264

VMEM_LIMIT = 56 * 1024 * 1024


def _cparams(sem, vmem=VMEM_LIMIT):
    return pltpu.CompilerParams(dimension_semantics=sem, vmem_limit_bytes=vmem)


def _bdot(a, b):
    return jnp.dot(a.astype(BF16), b.astype(BF16), preferred_element_type=F32)


def _bdot_nt(a, b):
    return lax.dot_general(a.astype(BF16), b.astype(BF16), (((1,), (1,)), ((), ())),
                           preferred_element_type=F32)


def _sigmoid(x):
    return 1.0 / (1.0 + jnp.exp(-x))


def _silu(x):
    return x * _sigmoid(x)


def _rms_rows(x, g):
    ms = jnp.mean(x * x, axis=-1, keepdims=True)
    return x * lax.rsqrt(ms + EPS) * g


NORM_CHUNK = 256


def _norm_rows_to(x_ref, g_ref, hn_ref):
    chunk = min(NORM_CHUNK, x_ref.shape[0])

    def body(c, carry):
        rs = pl.ds(pl.multiple_of(c * chunk, chunk), chunk)
        hn_ref[rs, :] = _rms_rows(x_ref[rs, :], g_ref[...]).astype(BF16)
        return carry

    lax.fori_loop(0, x_ref.shape[0] // chunk, body, 0)


def _x_row_spec(tm, d):
    return pl.BlockSpec((tm, d), lambda i, j: (i, 0), pipeline_mode=pl.Buffered(1))


def _in_proj_kernel(x_ref, g_ref, wa_ref, wb_ref, o_ref, hn_ref, *, na):
    j = pl.program_id(1)

    @pl.when(j == 0)
    def _():
        _norm_rows_to(x_ref, g_ref, hn_ref)

    @pl.when(j < na)
    def _():
        o_ref[...] = jnp.dot(hn_ref[...], wa_ref[...], preferred_element_type=F32).astype(o_ref.dtype)

    @pl.when(j >= na)
    def _():
        o_ref[...] = jnp.dot(hn_ref[...], wb_ref[...], preferred_element_type=F32).astype(o_ref.dtype)


def _in_proj(x, g, wa, wb, layer, tm, tn):
    t, d = x.shape
    na = A_COLS // tn
    nbt = wb.shape[2] // tn
    tm = min(tm, t)
    return pl.pallas_call(
        functools.partial(_in_proj_kernel, na=na),
        grid=(t // tm, na + nbt),
        in_specs=[_x_row_spec(tm, d),
                  pl.BlockSpec((1, d), lambda i, j: (0, 0)),
                  pl.BlockSpec((None, d, tn), lambda i, j: (layer, 0, jnp.minimum(j, na - 1))),
                  pl.BlockSpec((None, d, tn), lambda i, j: (layer, 0, jnp.maximum(j - na, 0)))],
        out_specs=pl.BlockSpec((tm, tn), lambda i, j: (i, j)),
        out_shape=jax.ShapeDtypeStruct((t, (na + nbt) * tn), F32),
        scratch_shapes=[pltpu.VMEM((tm, d), BF16)],
        compiler_params=_cparams(("parallel", "arbitrary")),
        name="in_proj",
    )(x, g, wa, wb)


def _ffn_up_kernel(x_ref, g_ref, wg_ref, wu_ref, o_ref, hn_ref):
    @pl.when(pl.program_id(1) == 0)
    def _():
        _norm_rows_to(x_ref, g_ref, hn_ref)

    h = hn_ref[...]
    a = jnp.dot(h, wg_ref[...], preferred_element_type=F32)
    b = jnp.dot(h, wu_ref[...], preferred_element_type=F32)
    o_ref[...] = (_silu(a) * b).astype(o_ref.dtype)


def _ffn_up(x, g, wg, wu, layer, tm, tn):
    t, d = x.shape
    n = wg.shape[2]
    tm = min(tm, t)
    w_spec = pl.BlockSpec((None, d, tn), lambda i, j: (layer, 0, j))
    return pl.pallas_call(
        _ffn_up_kernel,
        grid=(t // tm, n // tn),
        in_specs=[_x_row_spec(tm, d),
                  pl.BlockSpec((1, d), lambda i, j: (0, 0)),
                  w_spec, w_spec],
        out_specs=pl.BlockSpec((tm, tn), lambda i, j: (i, j)),
        out_shape=jax.ShapeDtypeStruct((t, n), BF16),
        scratch_shapes=[pltpu.VMEM((tm, d), BF16)],
        compiler_params=_cparams(("parallel", "arbitrary")),
        name="ffn_up",
    )(x, g, wg, wu)


def _mm_res_kernel(a_ref, w_ref, r_ref, o_ref):
    k = pl.program_id(2)

    @pl.when(k == 0)
    def _():
        o_ref[...] = r_ref[...] + jnp.dot(a_ref[...], w_ref[...], preferred_element_type=F32)

    @pl.when(k > 0)
    def _():
        o_ref[...] += jnp.dot(a_ref[...], w_ref[...], preferred_element_type=F32)


def _mm_res(a, w, r, layer, tm, tn, tk):
    t, kd = a.shape
    n = w.shape[2]
    tm = min(tm, t)
    return pl.pallas_call(
        _mm_res_kernel,
        grid=(t // tm, n // tn, kd // tk),
        in_specs=[pl.BlockSpec((tm, tk), lambda i, j, k: (i, k)),
                  pl.BlockSpec((None, tk, tn), lambda i, j, k: (layer, k, j)),
                  pl.BlockSpec((tm, tn), lambda i, j, k: (i, j))],
        out_specs=pl.BlockSpec((tm, tn), lambda i, j, k: (i, j)),
        out_shape=jax.ShapeDtypeStruct((t, n), F32),
        compiler_params=_cparams(("parallel", "parallel", "arbitrary")),
        name="mm_res",
    )(a, w, r)


def _out_proj_kernel(a0_ref, a1_ref, a2_ref, a3_ref, w_ref, r_ref, o_ref):
    acc = r_ref[...]
    for g, a_ref in enumerate((a0_ref, a1_ref, a2_ref, a3_ref)):
        acc = acc + jnp.dot(a_ref[...], w_ref[g * GROUP:(g + 1) * GROUP, :], preferred_element_type=F32)
    o_ref[...] = acc


def _out_proj(mix, w, r, layer, tm, tn):
    t = r.shape[0]
    n = w.shape[2]
    tm = min(tm, t)
    a_spec = pl.BlockSpec((tm, GROUP), lambda i, j: (i, 0))
    return pl.pallas_call(
        _out_proj_kernel,
        grid=(t // tm, n // tn),
        in_specs=[a_spec, a_spec, a_spec, a_spec,
                  pl.BlockSpec((None, 4 * GROUP, tn), lambda i, j: (layer, 0, j)),
                  pl.BlockSpec((tm, tn), lambda i, j: (i, j))],
        out_specs=pl.BlockSpec((tm, tn), lambda i, j: (i, j)),
        out_shape=jax.ShapeDtypeStruct((t, n), F32),
        compiler_params=_cparams(("parallel", "arbitrary")),
        name="out_proj",
    )(*mix, w, r)


def _moba_prep_kernel(k_ref, v_ref, g_ref, kn_ref, km_ref, vt_ref):
    g = g_ref[...]
    for h in range(N_HEADS):
        sl = slice(h * HEAD_DIM, (h + 1) * HEAD_DIM)
        kn = _rms_rows(k_ref[:, sl], g)
        kn_ref[:, sl] = kn.astype(BF16)
        for half in range(MOBA_TILE // MOBA_BLOCK):
            km_ref[half, :, sl] = jnp.mean(kn[half * MOBA_BLOCK:(half + 1) * MOBA_BLOCK], axis=0, keepdims=True)
        vt_ref[0, sl, :] = v_ref[:, sl].T.astype(BF16)


def _moba_prep(proj, gk):
    t = proj.shape[0]
    ntile = t // MOBA_TILE
    per = MOBA_TILE // MOBA_BLOCK
    return pl.pallas_call(
        _moba_prep_kernel,
        grid=(ntile,),
        in_specs=[pl.BlockSpec((MOBA_TILE, GROUP), lambda i: (i, CB_MK // 8)),
                  pl.BlockSpec((MOBA_TILE, GROUP), lambda i: (i, CB_MV // 8)),
                  pl.BlockSpec((1, HEAD_DIM), lambda i: (0, 0))],
        out_specs=[pl.BlockSpec((MOBA_TILE, GROUP), lambda i: (i, 0)),
                   pl.BlockSpec((per, 1, GROUP), lambda i: (i, 0, 0)),
                   pl.BlockSpec((1, GROUP, MOBA_TILE), lambda i: (i, 0, 0))],
        out_shape=[jax.ShapeDtypeStruct((t, GROUP), BF16),
                   jax.ShapeDtypeStruct((ntile * per, 1, GROUP), F32),
                   jax.ShapeDtypeStruct((ntile, GROUP, MOBA_TILE), BF16)],
        compiler_params=_cparams(("parallel",)),
        name="moba_prep",
    )(proj, proj, gk)


def _moba_kernel(q_ref, k_ref, vt_ref, km_ref, g_ref, o_ref, bias_ref, acc_ref, *, nb):
    ti = pl.program_id(2)
    blk, tile = MOBA_BLOCK, MOBA_TILE
    qn = _rms_rows(q_ref[...], g_ref[...])

    gate = lax.dot_general(km_ref[0], qn, (((1,), (1,)), ((), ())),
                           precision=lax.Precision.HIGHEST, preferred_element_type=F32)
    row = lax.broadcasted_iota(jnp.int32, gate.shape, 0)
    own = 2 * ti + (lax.broadcasted_iota(jnp.int32, gate.shape, 1) >= blk).astype(jnp.int32)
    rowf = row.astype(F32)
    gate = jnp.where(row < own, gate, -jnp.inf)
    bias = jnp.where(row == own, 0.0, NEG)
    for _ in range(MOBA_TOPK):
        m = jnp.max(gate, axis=0, keepdims=True)
        idx = jnp.min(jnp.where(gate == m, rowf, float(nb)), axis=0, keepdims=True)
        hit = jnp.logical_and(rowf == idx, m > -jnp.inf)
        bias = jnp.where(hit, 0.0, bias)
        gate = jnp.where(hit, -jnp.inf, gate)
    bias_ref[...] = bias

    qst = (qn * (HEAD_DIM ** -0.5)).T.astype(BF16)

    halves = (slice(0, blk), slice(blk, tile))

    def scores(t, qs):
        kt = k_ref[pl.ds(pl.multiple_of(t * tile, tile), tile), :]
        s = jnp.dot(kt, qst[:, qs], preferred_element_type=F32)
        b0 = bias_ref[pl.ds(2 * t, 1), qs]
        b1 = bias_ref[pl.ds(2 * t + 1, 1), qs]
        return jnp.concatenate([s[:blk] + b0, s[blk:] + b1], axis=0)

    s_own = [scores(ti, qs) for qs in halves]
    kr = lax.broadcasted_iota(jnp.int32, (tile, blk), 0)
    qc = lax.broadcasted_iota(jnp.int32, (tile, blk), 1)
    m0, l0 = [], []
    for hi, qs in enumerate(halves):
        s = jnp.where(kr <= qc + hi * blk, s_own[hi], NEG)
        m = jnp.max(s, axis=0, keepdims=True)
        p = jnp.exp(s - m)
        m0.append(m)
        l0.append(jnp.sum(p, axis=0, keepdims=True))
        acc_ref[:, qs] = jnp.dot(vt_ref[ti], p.astype(BF16), preferred_element_type=F32)

    def body(t, carry):
        ms, ls = carry
        ss = [scores(t, qs) for qs in halves]
        vt_t = vt_ref[t]
        ms_new, ls_new = [], []
        for hi, qs in enumerate(halves):
            m_new = jnp.maximum(ms[hi], jnp.max(ss[hi], axis=0, keepdims=True))
            alpha = jnp.exp(ms[hi] - m_new)
            p = jnp.exp(ss[hi] - m_new)
            ls_new.append(ls[hi] * alpha + jnp.sum(p, axis=0, keepdims=True))
            ms_new.append(m_new)
            acc_ref[:, qs] = acc_ref[:, qs] * alpha + jnp.dot(vt_t, p.astype(BF16), preferred_element_type=F32)
        return tuple(ms_new), tuple(ls_new)

    _, ls = lax.fori_loop(0, ti, body, (tuple(m0), tuple(l0)))
    for hi, qs in enumerate(halves):
        o_ref[qs, :] = (acc_ref[:, qs] / ls[hi]).T.astype(o_ref.dtype)


def _moba(proj, kn, km, vt, gq, batch):
    t = proj.shape[0]
    s = t // batch
    nb = s // MOBA_BLOCK
    nt = s // MOBA_TILE
    km = km.reshape(batch, nb, GROUP)
    return pl.pallas_call(
        functools.partial(_moba_kernel, nb=nb),
        grid=(batch, N_HEADS, nt),
        in_specs=[pl.BlockSpec((MOBA_TILE, HEAD_DIM), lambda b, h, i: (b * nt + i, CB_MQ + h)),
                  pl.BlockSpec((s, HEAD_DIM), lambda b, h, i: (b, h)),
                  pl.BlockSpec((nt, HEAD_DIM, MOBA_TILE), lambda b, h, i: (b, h, 0)),
                  pl.BlockSpec((1, nb, HEAD_DIM), lambda b, h, i: (b, 0, h)),
                  pl.BlockSpec((1, HEAD_DIM), lambda b, h, i: (0, 0))],
        out_specs=pl.BlockSpec((MOBA_TILE, HEAD_DIM), lambda b, h, i: (b * nt + i, h)),
        out_shape=jax.ShapeDtypeStruct((t, GROUP), BF16),
        scratch_shapes=[pltpu.VMEM((nb, MOBA_TILE), F32),
                        pltpu.VMEM((HEAD_DIM, MOBA_TILE), F32)],
        compiler_params=_cparams(("parallel", "parallel", "arbitrary")),
        name="moba",
    )(proj, kn, vt, km, gq)


GDN_ROWS = 256
GDN_HPS = 4


def _conv_silu(x_ref, halo_ref, w_ref, first):
    r = x_ref.shape[0]
    halo = halo_ref[...] * jnp.where(first, 0.0, 1.0)
    xb = jnp.concatenate([halo, x_ref[...]], axis=0)
    w = w_ref[...]
    out = None
    for tap in range(GDN_CONV):
        sh = GDN_CONV - 1 - tap
        xs = xb if sh == 0 else pltpu.roll(xb, sh, axis=0)
        term = xs[8:8 + r] * w[tap:tap + 1, :]
        out = term if out is None else out + term
    return _silu(out)


def _gdn_heads(qs, ks, vs, gs, betas, states):
    r = qs[0].shape[0]
    c = GDN_CHUNK
    pair = 2 * c
    npair = r // pair
    nh = len(qs)
    row = lax.broadcasted_iota(jnp.int32, (pair, pair), 0)
    col = lax.broadcasted_iota(jnp.int32, (pair, pair), 1)
    same = (row // c) == (col // c)
    tril = jnp.logical_and(same, row >= col)
    strict = jnp.logical_and(same, row > col)
    eye = (row == col).astype(F32)
    rin = row % c
    units = [(h, pi) for pi in range(npair) for h in range(nh)]

    def rows(x, u):
        return x[u[0]][u[1] * pair:(u[1] + 1) * pair]

    gcum = {u: rows(gs, u) for u in units}
    sh = 1
    while sh < c:
        gcum = {u: gcum[u] + jnp.where(rin >= sh, pltpu.roll(gcum[u], sh, axis=0), 0.0) for u in units}
        sh *= 2
    decay = {u: jnp.exp(jnp.where(tril, gcum[u] - gcum[u].T, -jnp.inf)) for u in units}
    eg = {u: jnp.exp(gcum[u]) for u in units}
    g_end = {u: (gcum[u][c - 1:c, :], gcum[u][pair - 1:pair, :]) for u in units}
    kb = {u: rows(ks, u) * rows(betas, u) for u in units}
    vb = {u: rows(vs, u) * rows(betas, u) for u in units}
    lmat = {u: jnp.where(strict, _bdot_nt(kb[u], rows(ks, u)) * decay[u], 0.0) for u in units}
    qk = {u: _bdot_nt(rows(qs, u), rows(ks, u)) * decay[u] for u in units}
    tinv = {u: eye - lmat[u] for u in units}
    lpow = lmat
    span = 1
    while 2 * span < c:
        lpow = {u: _bdot(lpow[u], lpow[u]) for u in units}
        tinv = {u: tinv[u] + _bdot(tinv[u], lpow[u]) for u in units}
        span *= 2
    uw = {u: _bdot(tinv[u], jnp.concatenate([vb[u], kb[u] * eg[u]], axis=1)) for u in units}
    qd = {u: rows(qs, u) * eg[u] for u in units}
    kdt = {u: (rows(ks, u) * jnp.exp(jnp.where(row < c, g_end[u][0], g_end[u][1]) - gcum[u])).T for u in units}

    states = list(states)
    vns = {u: [] for u in units}
    o_st = {u: [] for u in units}
    for pi in range(npair):
        for ci in range(2):
            cs = slice(ci * c, (ci + 1) * c)
            for h in range(nh):
                u = (h, pi)
                ws = _bdot(jnp.concatenate([uw[u][cs, HEAD_DIM:], qd[u][cs]], axis=0), states[h])
                vn = uw[u][cs, :HEAD_DIM] - ws[:c]
                o_st[u].append(ws[c:])
                vns[u].append(vn)
                zero = jnp.zeros_like(vn)
                vn_pad = jnp.concatenate([vn, zero] if ci == 0 else [zero, vn], axis=0)
                states[h] = states[h] * jnp.exp(g_end[u][ci]) + _bdot(kdt[u], vn_pad)
    outs = []
    for h in range(nh):
        parts = [jnp.concatenate(o_st[(h, pi)], axis=0) + _bdot(qk[(h, pi)], jnp.concatenate(vns[(h, pi)], axis=0))
                 for pi in range(npair)]
        outs.append(jnp.concatenate(parts, axis=0))
    return outs, states


def _gdn_kernel(q_ref, k_ref, v_ref, qh_ref, kh_ref, vh_ref, wq_ref, wk_ref, wv_ref,
                gab_ref, z_ref, hp_ref, gn_ref, o_ref, s_ref):
    hg = pl.program_id(1)
    first = pl.program_id(2) == 0

    @pl.when(first)
    def _():
        s_ref[...] = jnp.zeros_like(s_ref)

    q2 = _conv_silu(q_ref, qh_ref, wq_ref, first)
    k2 = _conv_silu(k_ref, kh_ref, wk_ref, first)
    v2 = _conv_silu(v_ref, vh_ref, wv_ref, first)
    gab = gab_ref[...]
    lane = lax.broadcasted_iota(jnp.int32, gab.shape, 1)

    qs, ks, vs, gs, betas = [], [], [], [], []
    for hh in range(GDN_HPS):
        h = hg * GDN_HPS + hh
        sl = slice(hh * HEAD_DIM, (hh + 1) * HEAD_DIM)
        q, k = q2[:, sl], k2[:, sl]
        qs.append(q * lax.rsqrt(jnp.sum(q * q, axis=-1, keepdims=True) + EPS) * (HEAD_DIM ** -0.5))
        ks.append(k * lax.rsqrt(jnp.sum(k * k, axis=-1, keepdims=True) + EPS))
        vs.append(v2[:, sl])
        ga = jnp.sum(jnp.where(lane == h, gab, 0.0), axis=1, keepdims=True)
        gb = jnp.sum(jnp.where(lane == N_HEADS + h, gab, 0.0), axis=1, keepdims=True)
        a_log = hp_ref[hh, 0:1, :]
        dt_bias = hp_ref[hh, 1:2, :]
        xg = ga + dt_bias
        softplus = jnp.maximum(xg, 0.0) + jnp.log1p(jnp.exp(-jnp.abs(xg)))
        gs.append(-jnp.exp(a_log) * softplus)
        betas.append(_sigmoid(gb + jnp.zeros_like(xg)))

    outs, states = _gdn_heads(qs, ks, vs, gs, betas, [s_ref[hh] for hh in range(GDN_HPS)])
    for hh in range(GDN_HPS):
        sl = slice(hh * HEAD_DIM, (hh + 1) * HEAD_DIM)
        s_ref[hh] = states[hh]
        on = _rms_rows(outs[hh], gn_ref[...])
        o_ref[:, sl] = (on * _silu(z_ref[:, sl])).astype(o_ref.dtype)


def _gdn(proj, conv_w, hp, gn, batch):
    t = proj.shape[0]
    s = t // batch
    r = GDN_ROWS
    steps = s // r
    hb = r // 8
    wide = GDN_HPS * HEAD_DIM
    cpb = GDN_HPS

    def main(cb):
        return pl.BlockSpec((r, wide), lambda b, h, i: (b * steps + i, cb // cpb + h))

    def halo(cb):
        return pl.BlockSpec((8, wide), lambda b, h, i: (jnp.maximum((b * steps + i) * hb - 1, 0), cb // cpb + h))

    def wspec(off):
        return pl.BlockSpec((GDN_CONV, wide), lambda b, h, i: (0, off // cpb + h))

    return pl.pallas_call(
        _gdn_kernel,
        grid=(batch, N_HEADS // GDN_HPS, steps),
        in_specs=[main(CB_GQ), main(CB_GK), main(CB_GV), halo(CB_GQ), halo(CB_GK), halo(CB_GV),
                  wspec(0), wspec(N_HEADS), wspec(2 * N_HEADS),
                  pl.BlockSpec((r, LANES), lambda b, h, i: (b * steps + i, CB_GAB)),
                  main(CB_GZ),
                  pl.BlockSpec((GDN_HPS, 2, LANES), lambda b, h, i: (h, 0, 0)),
                  pl.BlockSpec((1, HEAD_DIM), lambda b, h, i: (0, 0))],
        out_specs=pl.BlockSpec((r, wide), lambda b, h, i: (b * steps + i, h)),
        out_shape=jax.ShapeDtypeStruct((t, GROUP), BF16),
        scratch_shapes=[pltpu.VMEM((GDN_HPS, HEAD_DIM, HEAD_DIM), F32)],
        compiler_params=_cparams(("parallel", "parallel", "arbitrary")),
        name="gdn",
    )(proj, proj, proj, proj, proj, proj, conv_w, conv_w, conv_w, proj, proj, hp, gn)


SC_ROWS = 512


def _sconv_kernel(b_ref, c_ref, x_ref, ch_ref, xh_ref, w_ref, o_ref, *, steps):
    first = pl.program_id(0) % steps == 0
    r = b_ref.shape[0]
    y = c_ref[...] * x_ref[...]
    yh = ch_ref[...] * xh_ref[...] * jnp.where(first, 0.0, 1.0)
    yb = jnp.concatenate([yh, y], axis=0)
    w = w_ref[...]
    out = None
    for tap in range(SC_CONV):
        sh = SC_CONV - 1 - tap
        ys = yb if sh == 0 else pltpu.roll(yb, sh, axis=0)
        term = ys[8:8 + r] * w[tap:tap + 1, :]
        out = term if out is None else out + term
    o_ref[...] = (b_ref[...] * out).astype(o_ref.dtype)


def _sconv(proj, w, batch):
    t = proj.shape[0]
    r = min(SC_ROWS, t // batch)
    steps = (t // batch) // r
    hb = r // 8

    def main(cb):
        return pl.BlockSpec((r, GROUP), lambda i: (i, cb // 8))

    def halo(cb):
        return pl.BlockSpec((8, GROUP), lambda i: (jnp.maximum(i * hb - 1, 0), cb // 8))

    return pl.pallas_call(
        functools.partial(_sconv_kernel, steps=steps),
        grid=(t // r,),
        in_specs=[main(CB_SCB), main(CB_SCC), main(CB_SCX), halo(CB_SCC), halo(CB_SCX),
                  pl.BlockSpec((SC_CONV, GROUP), lambda i: (0, 0))],
        out_specs=pl.BlockSpec((r, GROUP), lambda i: (i, 0)),
        out_shape=jax.ShapeDtypeStruct((t, GROUP), BF16),
        compiler_params=_cparams(("parallel",)),
        name="sconv",
    )(proj, proj, proj, proj, proj, w)


SWA_ROWS = 512


def _half_rms(x, g2):
    lane = lax.broadcasted_iota(jnp.int32, x.shape, 1)
    lo = lane < SWA_D
    x2 = x * x
    ms_lo = jnp.sum(jnp.where(lo, x2, 0.0), axis=-1, keepdims=True) * (1.0 / SWA_D)
    ms_hi = jnp.sum(jnp.where(lo, 0.0, x2), axis=-1, keepdims=True) * (1.0 / SWA_D)
    rs = jnp.where(lo, lax.rsqrt(ms_lo + EPS), lax.rsqrt(ms_hi + EPS))
    return x * rs * g2


def _swa_kernel(q_ref, k_ref, v_ref, kh_ref, vh_ref, gq_ref, gk_ref, sink_ref, o_ref, *, steps):
    first = pl.program_id(0) % steps == 0
    r = q_ref.shape[0]
    w = SWA_W
    nsub = r // w
    pairs = SWA_Q_HEADS // 2
    lane = lax.broadcasted_iota(jnp.int32, (r + w, LANES), 1)

    kn = _half_rms(jnp.concatenate([kh_ref[...], k_ref[...]], axis=0), gk_ref[...])
    kroll = pltpu.roll(kn, SWA_D, axis=1)
    kdup = (jnp.where(lane < SWA_D, kn, kroll).astype(BF16),
            jnp.where(lane < SWA_D, kroll, kn).astype(BF16))
    vt = jnp.concatenate([vh_ref[...], v_ref[...]], axis=0).T.astype(BF16)

    kr = lax.broadcasted_iota(jnp.int32, (2 * w, 2 * w), 0)
    qc = lax.broadcasted_iota(jnp.int32, (2 * w, 2 * w), 1) % w
    band = jnp.logical_and(kr > qc, kr <= qc + w)
    band0 = jnp.logical_and(band, kr >= jnp.where(first, w, 0))
    qlane = lax.broadcasted_iota(jnp.int32, (w, LANES), 1)

    for sub in range(nsub):
        mask = band0 if sub == 0 else band
        ks = slice(sub * w, sub * w + 2 * w)
        for c in range(pairs):
            kv = c // (pairs // SWA_KV_HEADS)
            qn = _half_rms(q_ref[sub * w:(sub + 1) * w, c * LANES:(c + 1) * LANES], gq_ref[...]) * (SWA_D ** -0.5)
            qa = jnp.where(qlane < SWA_D, qn, 0.0)
            qb = jnp.where(qlane < SWA_D, 0.0, qn)
            qt = jnp.concatenate([qa.T, qb.T], axis=1).astype(BF16)
            s = jnp.dot(kdup[kv][ks], qt, preferred_element_type=F32)
            s = jnp.where(mask, s, NEG)
            sink = sink_ref[c:c + 1, :]
            m = jnp.maximum(jnp.max(s, axis=0, keepdims=True), sink)
            p = jnp.exp(s - m)
            l = jnp.sum(p, axis=0, keepdims=True) + jnp.exp(sink - m)
            pn = (p * (1.0 / l)).astype(BF16)
            ot = jnp.dot(vt[kv * SWA_D:(kv + 1) * SWA_D, ks], pn, preferred_element_type=F32)
            o = jnp.concatenate([ot[:, :w], ot[:, w:]], axis=0).T
            o_ref[sub * w:(sub + 1) * w, c * LANES:(c + 1) * LANES] = o.astype(o_ref.dtype)


def _swa(proj, gq2, gk2, sinkrow, batch):
    t = proj.shape[0]
    r = min(SWA_ROWS, t // batch)
    steps = (t // batch) // r
    hb = r // SWA_W

    def halo(cb):
        return pl.BlockSpec((SWA_W, LANES), lambda i: (jnp.maximum(i * hb - 1, 0), cb))

    return pl.pallas_call(
        functools.partial(_swa_kernel, steps=steps),
        grid=(t // r,),
        in_specs=[pl.BlockSpec((r, GROUP), lambda i: (i, CB_SQ // 8)),
                  pl.BlockSpec((r, LANES), lambda i: (i, CB_SK)),
                  pl.BlockSpec((r, LANES), lambda i: (i, CB_SV)),
                  halo(CB_SK), halo(CB_SV),
                  pl.BlockSpec((1, LANES), lambda i: (0, 0)),
                  pl.BlockSpec((1, LANES), lambda i: (0, 0)),
                  pl.BlockSpec((SWA_Q_HEADS // 2, 2 * SWA_W), lambda i: (0, 0))],
        out_specs=pl.BlockSpec((r, GROUP), lambda i: (i, 0)),
        out_shape=jax.ShapeDtypeStruct((t, GROUP), BF16),
        compiler_params=_cparams(("parallel",)),
        name="swa",
    )(proj, proj, proj, proj, proj, gq2, gk2, sinkrow)


def _w_in_tail(w):
    a1 = A_COLS + 2 * N_HEADS
    pad = jnp.zeros(w.shape[:2] + (NP_COLS - w.shape[2],), BF16)
    return jnp.concatenate([w[:, :, a1:], w[:, :, A_COLS:a1], pad], axis=2)


def _layer(x, batch, layer, norm_mix, w_in, w_in_tail, moba_q_norm, moba_k_norm, gdn_conv, gdn_a_log, gdn_dt_bias,
           gdn_out_norm, sc_conv, swa_q_norm, swa_k_norm, swa_sinks, w_out, norm_ffn, w_gate, w_up, w_down):
    row = lambda a: a.reshape(1, -1).astype(F32)

    proj = _in_proj(x, row(norm_mix), w_in, w_in_tail, layer, tm=1024, tn=512)

    kn, km, vt = _moba_prep(proj, row(moba_k_norm))
    o_a = _moba(proj, kn, km, vt, row(moba_q_norm), batch)

    hp = jnp.broadcast_to(jnp.stack([gdn_a_log, gdn_dt_bias], axis=1)[:, :, None], (N_HEADS, 2, LANES)).astype(F32)
    o_b = _gdn(proj, gdn_conv.astype(F32), hp, row(gdn_out_norm), batch)

    o_c = _sconv(proj, sc_conv.astype(F32), batch)

    sinkrow = jnp.repeat(swa_sinks.astype(F32), SWA_W).reshape(SWA_Q_HEADS // 2, 2 * SWA_W)
    o_d = _swa(proj, row(jnp.tile(swa_q_norm, 2)), row(jnp.tile(swa_k_norm, 2)), sinkrow, batch)

    x = _out_proj((o_a, o_b, o_c, o_d), w_out, x, layer, tm=1024, tn=512)

    act = _ffn_up(x, row(norm_ffn), w_gate, w_up, layer, tm=1024, tn=512)
    return _mm_res(act, w_down, x, layer, tm=2048, tn=1024, tk=1024)


def kernel(x, norm_mix, w_in, moba_q_norm, moba_k_norm, gdn_conv, gdn_a_log, gdn_dt_bias, gdn_out_norm, sc_conv, swa_q_norm, swa_k_norm, swa_sinks, w_out, norm_ffn, w_gate, w_up, w_down):
    batch, seq, d = x.shape
    ffpad = D_FF_PAD - w_gate.shape[2]
    w_in_b = w_in.astype(BF16)
    w_in_t = _w_in_tail(w_in_b)
    w_out_b = w_out.astype(BF16)
    w_gate_b = jnp.pad(w_gate.astype(BF16), ((0, 0), (0, 0), (0, ffpad)))
    w_up_b = jnp.pad(w_up.astype(BF16), ((0, 0), (0, 0), (0, ffpad)))
    w_down_b = jnp.pad(w_down.astype(BF16), ((0, 0), (0, ffpad), (0, 0)))
    h = x.reshape(batch * seq, d)
    for l in range(norm_mix.shape[0]):
        h = _layer(h, batch, l, norm_mix[l], w_in_b, w_in_t, moba_q_norm[l], moba_k_norm[l], gdn_conv[l], gdn_a_log[l],
                   gdn_dt_bias[l], gdn_out_norm[l], sc_conv[l], swa_q_norm[l], swa_k_norm[l], swa_sinks[l],
                   w_out_b, norm_ffn[l], w_gate_b, w_up_b, w_down_b)
    return h.reshape(batch, seq, d)
```

```python
import functools

import jax
import jax.numpy as jnp
from jax import lax
from jax.experimental import pallas as pl
from jax.experimental.pallas import tpu as pltpu

F32 = jnp.float32
BF16 = jnp.bfloat16

EPS = 1e-6
LANES = 128
GROUP = 1024
HEAD_DIM = 128
N_HEADS = GROUP // HEAD_DIM
MOBA_BLOCK = 256
MOBA_TILE = 2 * MOBA_BLOCK
MOBA_TOPK = 3
GDN_CONV = 4
GDN_CHUNK = 64
SC_CONV = 3
SWA_D = 64
SWA_Q_HEADS = GROUP // SWA_D
SWA_KV_HEADS = 2
SWA_W = 128
NEG = -1e30

CB_MQ, CB_MK, CB_MV = 0, 8, 16
CB_GQ, CB_GK, CB_GV, CB_GZ = 24, 32, 40, 48
CB_SCB, CB_SCC, CB_SCX = 56, 64, 72
CB_SQ, CB_SK, CB_SV, CB_GAB = 80, 88, 89, 90
NP_COLS = 92 * LANES
A_COLS = 6 * GROUP
D_FF_PAD = 11264

VMEM_LIMIT = 56 * 1024 * 1024


def _cparams(sem, vmem=VMEM_LIMIT):
    return pltpu.CompilerParams(dimension_semantics=sem, vmem_limit_bytes=vmem)


def _bdot(a, b):
    return jnp.dot(a.astype(BF16), b.astype(BF16), preferred_element_type=F32)


def _bdot_nt(a, b):
    return lax.dot_general(a.astype(BF16), b.astype(BF16), (((1,), (1,)), ((), ())),
                           preferred_element_type=F32)


def _sigmoid(x):
    return 1.0 / (1.0 + jnp.exp(-x))


def _silu(x):
    return x * _sigmoid(x)


def _rms_rows(x, g):
    ms = jnp.mean(x * x, axis=-1, keepdims=True)
    return x * lax.rsqrt(ms + EPS) * g


NORM_CHUNK = 256


def _norm_rows_to(x_ref, g_ref, hn_ref):
    chunk = min(NORM_CHUNK, x_ref.shape[0])

    def body(c, carry):
        rs = pl.ds(pl.multiple_of(c * chunk, chunk), chunk)
        hn_ref[rs, :] = _rms_rows(x_ref[rs, :], g_ref[...]).astype(BF16)
        return carry

    lax.fori_loop(0, x_ref.shape[0] // chunk, body, 0)


def _x_row_spec(tm, d):
    return pl.BlockSpec((tm, d), lambda i, j: (i, 0), pipeline_mode=pl.Buffered(1))


def _in_proj_kernel(x_ref, g_ref, wa_ref, wb_ref, o_ref, hn_ref, *, na):
    j = pl.program_id(1)

    @pl.when(j == 0)
    def _():
        _norm_rows_to(x_ref, g_ref, hn_ref)

    @pl.when(j < na)
    def _():
        o_ref[...] = jnp.dot(hn_ref[...], wa_ref[...], preferred_element_type=F32).astype(o_ref.dtype)

    @pl.when(j >= na)
    def _():
        o_ref[...] = jnp.dot(hn_ref[...], wb_ref[...], preferred_element_type=F32).astype(o_ref.dtype)


def _in_proj(x, g, wa, wb, layer, tm, tn):
    t, d = x.shape
    na = A_COLS // tn
    nbt = wb.shape[2] // tn
    tm = min(tm, t)
    return pl.pallas_call(
        functools.partial(_in_proj_kernel, na=na),
        grid=(t // tm, na + nbt),
        in_specs=[_x_row_spec(tm, d),
                  pl.BlockSpec((1, d), lambda i, j: (0, 0)),
                  pl.BlockSpec((None, d, tn), lambda i, j: (layer, 0, jnp.minimum(j, na - 1))),
                  pl.BlockSpec((None, d, tn), lambda i, j: (layer, 0, jnp.maximum(j - na, 0)))],
        out_specs=pl.BlockSpec((tm, tn), lambda i, j: (i, j)),
        out_shape=jax.ShapeDtypeStruct((t, (na + nbt) * tn), F32),
        scratch_shapes=[pltpu.VMEM((tm, d), BF16)],
        compiler_params=_cparams(("parallel", "arbitrary")),
        name="in_proj",
    )(x, g, wa, wb)


def _ffn_up_kernel(x_ref, g_ref, wg_ref, wu_ref, o_ref, hn_ref):
    @pl.when(pl.program_id(1) == 0)
    def _():
        _norm_rows_to(x_ref, g_ref, hn_ref)

    h = hn_ref[...]
    a = jnp.dot(h, wg_ref[...], preferred_element_type=F32)
    b = jnp.dot(h, wu_ref[...], preferred_element_type=F32)
    o_ref[...] = (_silu(a) * b).astype(o_ref.dtype)


def _ffn_up(x, g, wg, wu, layer, tm, tn):
    t, d = x.shape
    n = wg.shape[2]
    tm = min(tm, t)
    w_spec = pl.BlockSpec((None, d, tn), lambda i, j: (layer, 0, j))
    return pl.pallas_call(
        _ffn_up_kernel,
        grid=(t // tm, n // tn),
        in_specs=[_x_row_spec(tm, d),
                  pl.BlockSpec((1, d), lambda i, j: (0, 0)),
                  w_spec, w_spec],
        out_specs=pl.BlockSpec((tm, tn), lambda i, j: (i, j)),
        out_shape=jax.ShapeDtypeStruct((t, n), BF16),
        scratch_shapes=[pltpu.VMEM((tm, d), BF16)],
        compiler_params=_cparams(("parallel", "arbitrary")),
        name="ffn_up",
    )(x, g, wg, wu)


def _mm_res_kernel(a_ref, w_ref, r_ref, o_ref):
    k = pl.program_id(2)

    @pl.when(k == 0)
    def _():
        o_ref[...] = r_ref[...] + jnp.dot(a_ref[...], w_ref[...], preferred_element_type=F32)

    @pl.when(k > 0)
    def _():
        o_ref[...] += jnp.dot(a_ref[...], w_ref[...], preferred_element_type=F32)


def _mm_res(a, w, r, layer, tm, tn, tk):
    t, kd = a.shape
    n = w.shape[2]
    tm = min(tm, t)
    return pl.pallas_call(
        _mm_res_kernel,
        grid=(t // tm, n // tn, kd // tk),
        in_specs=[pl.BlockSpec((tm, tk), lambda i, j, k: (i, k)),
                  pl.BlockSpec((None, tk, tn), lambda i, j, k: (layer, k, j)),
                  pl.BlockSpec((tm, tn), lambda i, j, k: (i, j))],
        out_specs=pl.BlockSpec((tm, tn), lambda i, j, k: (i, j)),
        out_shape=jax.ShapeDtypeStruct((t, n), F32),
        compiler_params=_cparams(("parallel", "parallel", "arbitrary")),
        name="mm_res",
    )(a, w, r)


def _out_proj_kernel(a0_ref, a1_ref, a2_ref, a3_ref, w_ref, r_ref, o_ref):
    acc = r_ref[...]
    for g, a_ref in enumerate((a0_ref, a1_ref, a2_ref, a3_ref)):
        acc = acc + jnp.dot(a_ref[...], w_ref[g * GROUP:(g + 1) * GROUP, :], preferred_element_type=F32)
    o_ref[...] = acc


def _out_proj(mix, w, r, layer, tm, tn):
    t = r.shape[0]
    n = w.shape[2]
    tm = min(tm, t)
    a_spec = pl.BlockSpec((tm, GROUP), lambda i, j: (i, 0))
    return pl.pallas_call(
        _out_proj_kernel,
        grid=(t // tm, n // tn),
        in_specs=[a_spec, a_spec, a_spec, a_spec,
                  pl.BlockSpec((None, 4 * GROUP, tn), lambda i, j: (layer, 0, j)),
                  pl.BlockSpec((tm, tn), lambda i, j: (i, j))],
        out_specs=pl.BlockSpec((tm, tn), lambda i, j: (i, j)),
        out_shape=jax.ShapeDtypeStruct((t, n), F32),
        compiler_params=_cparams(("parallel", "arbitrary")),
        name="out_proj",
    )(*mix, w, r)


def _moba_prep_kernel(k_ref, v_ref, g_ref, kn_ref, km_ref, vt_ref):
    g = g_ref[...]
    for h in range(N_HEADS):
        sl = slice(h * HEAD_DIM, (h + 1) * HEAD_DIM)
        kn = _rms_rows(k_ref[:, sl], g)
        kn_ref[:, sl] = kn.astype(BF16)
        for half in range(MOBA_TILE // MOBA_BLOCK):
            km_ref[half, :, sl] = jnp.mean(kn[half * MOBA_BLOCK:(half + 1) * MOBA_BLOCK], axis=0, keepdims=True)
        vt_ref[0, sl, :] = v_ref[:, sl].T.astype(BF16)


def _moba_prep(proj, gk):
    t = proj.shape[0]
    ntile = t // MOBA_TILE
    per = MOBA_TILE // MOBA_BLOCK
    return pl.pallas_call(
        _moba_prep_kernel,
        grid=(ntile,),
        in_specs=[pl.BlockSpec((MOBA_TILE, GROUP), lambda i: (i, CB_MK // 8)),
                  pl.BlockSpec((MOBA_TILE, GROUP), lambda i: (i, CB_MV // 8)),
                  pl.BlockSpec((1, HEAD_DIM), lambda i: (0, 0))],
        out_specs=[pl.BlockSpec((MOBA_TILE, GROUP), lambda i: (i, 0)),
                   pl.BlockSpec((per, 1, GROUP), lambda i: (i, 0, 0)),
                   pl.BlockSpec((1, GROUP, MOBA_TILE), lambda i: (i, 0, 0))],
        out_shape=[jax.ShapeDtypeStruct((t, GROUP), BF16),
                   jax.ShapeDtypeStruct((ntile * per, 1, GROUP), F32),
                   jax.ShapeDtypeStruct((ntile, GROUP, MOBA_TILE), BF16)],
        compiler_params=_cparams(("parallel",)),
        name="moba_prep",
    )(proj, proj, gk)


def _moba_kernel(q_ref, k_ref, vt_ref, km_ref, g_ref, o_ref, bias_ref, acc_ref, s0_ref, s1_ref, p0_ref, p1_ref,
                 *, nb):
    ti = pl.program_id(2)
    blk, tile = MOBA_BLOCK, MOBA_TILE
    qn = _rms_rows(q_ref[...], g_ref[...])

    gate = lax.dot_general(km_ref[0], qn, (((1,), (1,)), ((), ())),
                           precision=lax.Precision.HIGHEST, preferred_element_type=F32)
    row = lax.broadcasted_iota(jnp.int32, gate.shape, 0)
    own = 2 * ti + (lax.broadcasted_iota(jnp.int32, gate.shape, 1) >= blk).astype(jnp.int32)
    rowf = row.astype(F32)
    gate = jnp.where(row < own, gate, -jnp.inf)
    bias = jnp.where(row == own, 0.0, NEG)
    for _ in range(MOBA_TOPK):
        m = jnp.max(gate, axis=0, keepdims=True)
        idx = jnp.min(jnp.where(gate == m, rowf, float(nb)), axis=0, keepdims=True)
        hit = jnp.logical_and(rowf == idx, m > -jnp.inf)
        bias = jnp.where(hit, 0.0, bias)
        gate = jnp.where(hit, -jnp.inf, gate)
    bias_ref[...] = bias

    qst = (qn * (HEAD_DIM ** -0.5)).T.astype(BF16)

    def scores(t, valid):
        kt = k_ref[pl.ds(pl.multiple_of(t * tile, tile), tile), :]
        s = jnp.dot(kt, qst, preferred_element_type=F32)
        b0 = jnp.where(valid, bias_ref[pl.ds(2 * t, 1), :], -jnp.inf)
        b1 = jnp.where(valid, bias_ref[pl.ds(2 * t + 1, 1), :], -jnp.inf)
        return jnp.concatenate([s[:blk] + b0, s[blk:] + b1], axis=0)

    s_refs, p_refs = (s0_ref, s1_ref), (p0_ref, p1_ref)
    kr = lax.broadcasted_iota(jnp.int32, (tile, tile), 0)
    qc = lax.broadcasted_iota(jnp.int32, (tile, tile), 1)
    s_refs[0][...] = jnp.where(kr <= qc, scores(ti, True), NEG)
    p_refs[1][...] = jnp.zeros((tile, tile), BF16)
    acc_ref[...] = jnp.zeros_like(acc_ref)

    def trip(n, par, carry):
        m, l, alpha_prev = carry
        tc = jnp.where(n == 1, ti, jnp.clip(n - 2, 0, ti))
        pv = jnp.dot(vt_ref[tc], p_refs[1 - par][...], preferred_element_type=F32)
        s_refs[1 - par][...] = scores(jnp.minimum(n, ti), n < ti)
        s = s_refs[par][...]
        m_new = jnp.maximum(m, jnp.max(s, axis=0, keepdims=True))
        alpha = jnp.exp(m - m_new)
        p = jnp.exp(s - m_new)
        l = l * alpha + jnp.sum(p, axis=0, keepdims=True)
        p_refs[par][...] = p.astype(BF16)
        acc_ref[...] = acc_ref[...] * alpha_prev + pv
        return m_new, l, alpha

    def body(j, carry):
        return trip(2 * j + 1, 1, trip(2 * j, 0, carry))

    init = (jnp.full((1, tile), NEG, F32), jnp.zeros((1, tile), F32), jnp.ones((1, tile), F32))
    _, l, _ = lax.fori_loop(0, (ti + 3) // 2, body, init)
    o_ref[...] = (acc_ref[...] / l).T.astype(o_ref.dtype)


def _moba(proj, kn, km, vt, gq, batch):
    t = proj.shape[0]
    s = t // batch
    nb = s // MOBA_BLOCK
    nt = s // MOBA_TILE
    km = km.reshape(batch, nb, GROUP)
    return pl.pallas_call(
        functools.partial(_moba_kernel, nb=nb),
        grid=(batch, N_HEADS, nt),
        in_specs=[pl.BlockSpec((MOBA_TILE, HEAD_DIM), lambda b, h, i: (b * nt + i, CB_MQ + h)),
                  pl.BlockSpec((s, HEAD_DIM), lambda b, h, i: (b, h)),
                  pl.BlockSpec((nt, HEAD_DIM, MOBA_TILE), lambda b, h, i: (b, h, 0)),
                  pl.BlockSpec((1, nb, HEAD_DIM), lambda b, h, i: (b, 0, h)),
                  pl.BlockSpec((1, HEAD_DIM), lambda b, h, i: (0, 0))],
        out_specs=pl.BlockSpec((MOBA_TILE, HEAD_DIM), lambda b, h, i: (b * nt + i, h)),
        out_shape=jax.ShapeDtypeStruct((t, GROUP), BF16),
        scratch_shapes=[pltpu.VMEM((nb, MOBA_TILE), F32),
                        pltpu.VMEM((HEAD_DIM, MOBA_TILE), F32),
                        pltpu.VMEM((MOBA_TILE, MOBA_TILE), F32), pltpu.VMEM((MOBA_TILE, MOBA_TILE), F32),
                        pltpu.VMEM((MOBA_TILE, MOBA_TILE), BF16), pltpu.VMEM((MOBA_TILE, MOBA_TILE), BF16)],
        compiler_params=_cparams(("parallel", "parallel", "arbitrary")),
        name="moba",
    )(proj, kn, vt, km, gq)


GDN_ROWS = 256
GDN_HPS = 4


def _conv_silu(x_ref, halo_ref, w_ref, first):
    r = x_ref.shape[0]
    halo = halo_ref[...] * jnp.where(first, 0.0, 1.0)
    xb = jnp.concatenate([halo, x_ref[...]], axis=0)
    w = w_ref[...]
    out = None
    for tap in range(GDN_CONV):
        sh = GDN_CONV - 1 - tap
        xs = xb if sh == 0 else pltpu.roll(xb, sh, axis=0)
        term = xs[8:8 + r] * w[tap:tap + 1, :]
        out = term if out is None else out + term
    return _silu(out)


def _gdn_heads(qs, ks, vs, gs, betas, states):
    r = qs[0].shape[0]
    c = GDN_CHUNK
    pair = 2 * c
    npair = r // pair
    nh = len(qs)
    row = lax.broadcasted_iota(jnp.int32, (pair, pair), 0)
    col = lax.broadcasted_iota(jnp.int32, (pair, pair), 1)
    same = (row // c) == (col // c)
    tril = jnp.logical_and(same, row >= col)
    strict = jnp.logical_and(same, row > col)
    eye = (row == col).astype(F32)
    rin = row % c
    units = [(h, pi) for pi in range(npair) for h in range(nh)]

    def rows(x, u):
        return x[u[0]][u[1] * pair:(u[1] + 1) * pair]

    gcum = {u: rows(gs, u) for u in units}
    sh = 1
    while sh < c:
        gcum = {u: gcum[u] + jnp.where(rin >= sh, pltpu.roll(gcum[u], sh, axis=0), 0.0) for u in units}
        sh *= 2
    decay = {u: jnp.exp(jnp.where(tril, gcum[u] - gcum[u].T, -jnp.inf)) for u in units}
    eg = {u: jnp.exp(gcum[u]) for u in units}
    g_end = {u: (gcum[u][c - 1:c, :], gcum[u][pair - 1:pair, :]) for u in units}
    kb = {u: rows(ks, u) * rows(betas, u) for u in units}
    vb = {u: rows(vs, u) * rows(betas, u) for u in units}
    lmat = {u: jnp.where(strict, _bdot_nt(kb[u], rows(ks, u)) * decay[u], 0.0) for u in units}
    qk = {u: _bdot_nt(rows(qs, u), rows(ks, u)) * decay[u] for u in units}
    tinv = {u: eye - lmat[u] for u in units}
    lpow = lmat
    span = 1
    while 2 * span < c:
        lpow = {u: _bdot(lpow[u], lpow[u]) for u in units}
        tinv = {u: tinv[u] + _bdot(tinv[u], lpow[u]) for u in units}
        span *= 2
    uw = {u: _bdot(tinv[u], jnp.concatenate([vb[u], kb[u] * eg[u]], axis=1)) for u in units}
    qd = {u: rows(qs, u) * eg[u] for u in units}
    kdt = {u: (rows(ks, u) * jnp.exp(jnp.where(row < c, g_end[u][0], g_end[u][1]) - gcum[u])).T for u in units}

    states = list(states)
    vns = {u: [] for u in units}
    o_st = {u: [] for u in units}
    for pi in range(npair):
        for ci in range(2):
            cs = slice(ci * c, (ci + 1) * c)
            for h in range(nh):
                u = (h, pi)
                ws = _bdot(jnp.concatenate([uw[u][cs, HEAD_DIM:], qd[u][cs]], axis=0), states[h])
                vn = uw[u][cs, :HEAD_DIM] - ws[:c]
                o_st[u].append(ws[c:])
                vns[u].append(vn)
                zero = jnp.zeros_like(vn)
                vn_pad = jnp.concatenate([vn, zero] if ci == 0 else [zero, vn], axis=0)
                states[h] = states[h] * jnp.exp(g_end[u][ci]) + _bdot(kdt[u], vn_pad)
    outs = []
    for h in range(nh):
        parts = [jnp.concatenate(o_st[(h, pi)], axis=0) + _bdot(qk[(h, pi)], jnp.concatenate(vns[(h, pi)], axis=0))
                 for pi in range(npair)]
        outs.append(jnp.concatenate(parts, axis=0))
    return outs, states


def _gdn_kernel(q_ref, k_ref, v_ref, qh_ref, kh_ref, vh_ref, wq_ref, wk_ref, wv_ref,
                gab_ref, z_ref, hp_ref, gn_ref, o_ref, s_ref):
    hg = pl.program_id(1)
    first = pl.program_id(2) == 0

    @pl.when(first)
    def _():
        s_ref[...] = jnp.zeros_like(s_ref)

    q2 = _conv_silu(q_ref, qh_ref, wq_ref, first)
    k2 = _conv_silu(k_ref, kh_ref, wk_ref, first)
    v2 = _conv_silu(v_ref, vh_ref, wv_ref, first)
    gab = gab_ref[...]
    lane = lax.broadcasted_iota(jnp.int32, gab.shape, 1)

    qs, ks, vs, gs, betas = [], [], [], [], []
    for hh in range(GDN_HPS):
        h = hg * GDN_HPS + hh
        sl = slice(hh * HEAD_DIM, (hh + 1) * HEAD_DIM)
        q, k = q2[:, sl], k2[:, sl]
        qs.append(q * lax.rsqrt(jnp.sum(q * q, axis=-1, keepdims=True) + EPS) * (HEAD_DIM ** -0.5))
        ks.append(k * lax.rsqrt(jnp.sum(k * k, axis=-1, keepdims=True) + EPS))
        vs.append(v2[:, sl])
        ga = jnp.sum(jnp.where(lane == h, gab, 0.0), axis=1, keepdims=True)
        gb = jnp.sum(jnp.where(lane == N_HEADS + h, gab, 0.0), axis=1, keepdims=True)
        a_log = hp_ref[hh, 0:1, :]
        dt_bias = hp_ref[hh, 1:2, :]
        xg = ga + dt_bias
        softplus = jnp.maximum(xg, 0.0) + jnp.log1p(jnp.exp(-jnp.abs(xg)))
        gs.append(-jnp.exp(a_log) * softplus)
        betas.append(_sigmoid(gb + jnp.zeros_like(xg)))

    outs, states = _gdn_heads(qs, ks, vs, gs, betas, [s_ref[hh] for hh in range(GDN_HPS)])
    for hh in range(GDN_HPS):
        sl = slice(hh * HEAD_DIM, (hh + 1) * HEAD_DIM)
        s_ref[hh] = states[hh]
        on = _rms_rows(outs[hh], gn_ref[...])
        o_ref[:, sl] = (on * _silu(z_ref[:, sl])).astype(o_ref.dtype)


def _gdn(proj, conv_w, hp, gn, batch):
    t = proj.shape[0]
    s = t // batch
    r = GDN_ROWS
    steps = s // r
    hb = r // 8
    wide = GDN_HPS * HEAD_DIM
    cpb = GDN_HPS

    def main(cb):
        return pl.BlockSpec((r, wide), lambda b, h, i: (b * steps + i, cb // cpb + h))

    def halo(cb):
        return pl.BlockSpec((8, wide), lambda b, h, i: (jnp.maximum((b * steps + i) * hb - 1, 0), cb // cpb + h))

    def wspec(off):
        return pl.BlockSpec((GDN_CONV, wide), lambda b, h, i: (0, off // cpb + h))

    return pl.pallas_call(
        _gdn_kernel,
        grid=(batch, N_HEADS // GDN_HPS, steps),
        in_specs=[main(CB_GQ), main(CB_GK), main(CB_GV), halo(CB_GQ), halo(CB_GK), halo(CB_GV),
                  wspec(0), wspec(N_HEADS), wspec(2 * N_HEADS),
                  pl.BlockSpec((r, LANES), lambda b, h, i: (b * steps + i, CB_GAB)),
                  main(CB_GZ),
                  pl.BlockSpec((GDN_HPS, 2, LANES), lambda b, h, i: (h, 0, 0)),
                  pl.BlockSpec((1, HEAD_DIM), lambda b, h, i: (0, 0))],
        out_specs=pl.BlockSpec((r, wide), lambda b, h, i: (b * steps + i, h)),
        out_shape=jax.ShapeDtypeStruct((t, GROUP), BF16),
        scratch_shapes=[pltpu.VMEM((GDN_HPS, HEAD_DIM, HEAD_DIM), F32)],
        compiler_params=_cparams(("parallel", "parallel", "arbitrary")),
        name="gdn",
    )(proj, proj, proj, proj, proj, proj, conv_w, conv_w, conv_w, proj, proj, hp, gn)


SC_ROWS = 512


def _sconv_kernel(b_ref, c_ref, x_ref, ch_ref, xh_ref, w_ref, o_ref, *, steps):
    first = pl.program_id(0) % steps == 0
    r = b_ref.shape[0]
    y = c_ref[...] * x_ref[...]
    yh = ch_ref[...] * xh_ref[...] * jnp.where(first, 0.0, 1.0)
    yb = jnp.concatenate([yh, y], axis=0)
    w = w_ref[...]
    out = None
    for tap in range(SC_CONV):
        sh = SC_CONV - 1 - tap
        ys = yb if sh == 0 else pltpu.roll(yb, sh, axis=0)
        term = ys[8:8 + r] * w[tap:tap + 1, :]
        out = term if out is None else out + term
    o_ref[...] = (b_ref[...] * out).astype(o_ref.dtype)


def _sconv(proj, w, batch):
    t = proj.shape[0]
    r = min(SC_ROWS, t // batch)
    steps = (t // batch) // r
    hb = r // 8

    def main(cb):
        return pl.BlockSpec((r, GROUP), lambda i: (i, cb // 8))

    def halo(cb):
        return pl.BlockSpec((8, GROUP), lambda i: (jnp.maximum(i * hb - 1, 0), cb // 8))

    return pl.pallas_call(
        functools.partial(_sconv_kernel, steps=steps),
        grid=(t // r,),
        in_specs=[main(CB_SCB), main(CB_SCC), main(CB_SCX), halo(CB_SCC), halo(CB_SCX),
                  pl.BlockSpec((SC_CONV, GROUP), lambda i: (0, 0))],
        out_specs=pl.BlockSpec((r, GROUP), lambda i: (i, 0)),
        out_shape=jax.ShapeDtypeStruct((t, GROUP), BF16),
        compiler_params=_cparams(("parallel",)),
        name="sconv",
    )(proj, proj, proj, proj, proj, w)


SWA_ROWS = 512


def _half_rms(x, g2):
    lane = lax.broadcasted_iota(jnp.int32, x.shape, 1)
    lo = lane < SWA_D
    x2 = x * x
    ms_lo = jnp.sum(jnp.where(lo, x2, 0.0), axis=-1, keepdims=True) * (1.0 / SWA_D)
    ms_hi = jnp.sum(jnp.where(lo, 0.0, x2), axis=-1, keepdims=True) * (1.0 / SWA_D)
    rs = jnp.where(lo, lax.rsqrt(ms_lo + EPS), lax.rsqrt(ms_hi + EPS))
    return x * rs * g2


def _swa_kernel(q_ref, k_ref, v_ref, kh_ref, vh_ref, gq_ref, gk_ref, sink_ref, o_ref, *, steps):
    first = pl.program_id(0) % steps == 0
    r = q_ref.shape[0]
    w = SWA_W
    nsub = r // w
    pairs = SWA_Q_HEADS // 2
    lane = lax.broadcasted_iota(jnp.int32, (r + w, LANES), 1)

    kn = _half_rms(jnp.concatenate([kh_ref[...], k_ref[...]], axis=0), gk_ref[...])
    kroll = pltpu.roll(kn, SWA_D, axis=1)
    kdup = (jnp.where(lane < SWA_D, kn, kroll).astype(BF16),
            jnp.where(lane < SWA_D, kroll, kn).astype(BF16))
    vt = jnp.concatenate([vh_ref[...], v_ref[...]], axis=0).T.astype(BF16)

    kr = lax.broadcasted_iota(jnp.int32, (2 * w, 2 * w), 0)
    qc = lax.broadcasted_iota(jnp.int32, (2 * w, 2 * w), 1) % w
    band = jnp.logical_and(kr > qc, kr <= qc + w)
    band0 = jnp.logical_and(band, kr >= jnp.where(first, w, 0))
    qlane = lax.broadcasted_iota(jnp.int32, (w, LANES), 1)

    for sub in range(nsub):
        mask = band0 if sub == 0 else band
        ks = slice(sub * w, sub * w + 2 * w)
        for c in range(pairs):
            kv = c // (pairs // SWA_KV_HEADS)
            qn = _half_rms(q_ref[sub * w:(sub + 1) * w, c * LANES:(c + 1) * LANES], gq_ref[...]) * (SWA_D ** -0.5)
            qa = jnp.where(qlane < SWA_D, qn, 0.0)
            qb = jnp.where(qlane < SWA_D, 0.0, qn)
            qt = jnp.concatenate([qa.T, qb.T], axis=1).astype(BF16)
            s = jnp.dot(kdup[kv][ks], qt, preferred_element_type=F32)
            s = jnp.where(mask, s, NEG)
            sink = sink_ref[c:c + 1, :]
            m = jnp.maximum(jnp.max(s, axis=0, keepdims=True), sink)
            p = jnp.exp(s - m)
            l = jnp.sum(p, axis=0, keepdims=True) + jnp.exp(sink - m)
            pn = (p * (1.0 / l)).astype(BF16)
            ot = jnp.dot(vt[kv * SWA_D:(kv + 1) * SWA_D, ks], pn, preferred_element_type=F32)
            o = jnp.concatenate([ot[:, :w], ot[:, w:]], axis=0).T
            o_ref[sub * w:(sub + 1) * w, c * LANES:(c + 1) * LANES] = o.astype(o_ref.dtype)


def _swa(proj, gq2, gk2, sinkrow, batch):
    t = proj.shape[0]
    r = min(SWA_ROWS, t // batch)
    steps = (t // batch) // r
    hb = r // SWA_W

    def halo(cb):
        return pl.BlockSpec((SWA_W, LANES), lambda i: (jnp.maximum(i * hb - 1, 0), cb))

    return pl.pallas_call(
        functools.partial(_swa_kernel, steps=steps),
        grid=(t // r,),
        in_specs=[pl.BlockSpec((r, GROUP), lambda i: (i, CB_SQ // 8)),
                  pl.BlockSpec((r, LANES), lambda i: (i, CB_SK)),
                  pl.BlockSpec((r, LANES), lambda i: (i, CB_SV)),
                  halo(CB_SK), halo(CB_SV),
                  pl.BlockSpec((1, LANES), lambda i: (0, 0)),
                  pl.BlockSpec((1, LANES), lambda i: (0, 0)),
                  pl.BlockSpec((SWA_Q_HEADS // 2, 2 * SWA_W), lambda i: (0, 0))],
        out_specs=pl.BlockSpec((r, GROUP), lambda i: (i, 0)),
        out_shape=jax.ShapeDtypeStruct((t, GROUP), BF16),
        compiler_params=_cparams(("parallel",)),
        name="swa",
    )(proj, proj, proj, proj, proj, gq2, gk2, sinkrow)


def _w_in_tail(w):
    a1 = A_COLS + 2 * N_HEADS
    pad = jnp.zeros(w.shape[:2] + (NP_COLS - w.shape[2],), BF16)
    return jnp.concatenate([w[:, :, a1:].astype(BF16), w[:, :, A_COLS:a1].astype(BF16), pad], axis=2)


def _bf16_zero_extend(w, axis, size):
    shape = list(w.shape)
    shape[axis] = size - shape[axis]
    return jnp.concatenate([w.astype(BF16), jnp.zeros(shape, BF16)], axis=axis)


def _layer(x, batch, layer, norm_mix, w_in, w_in_tail, moba_q_norm, moba_k_norm, gdn_conv, gdn_a_log, gdn_dt_bias,
           gdn_out_norm, sc_conv, swa_q_norm, swa_k_norm, swa_sinks, w_out, norm_ffn, w_gate, w_up, w_down):
    row = lambda a: a.reshape(1, -1).astype(F32)

    proj = _in_proj(x, row(norm_mix), w_in, w_in_tail, layer, tm=1024, tn=512)

    kn, km, vt = _moba_prep(proj, row(moba_k_norm))
    o_a = _moba(proj, kn, km, vt, row(moba_q_norm), batch)

    hp = jnp.broadcast_to(jnp.stack([gdn_a_log, gdn_dt_bias], axis=1)[:, :, None], (N_HEADS, 2, LANES)).astype(F32)
    o_b = _gdn(proj, gdn_conv.astype(F32), hp, row(gdn_out_norm), batch)

    o_c = _sconv(proj, sc_conv.astype(F32), batch)

    sinkrow = jnp.repeat(swa_sinks.astype(F32), SWA_W).reshape(SWA_Q_HEADS // 2, 2 * SWA_W)
    o_d = _swa(proj, row(jnp.tile(swa_q_norm, 2)), row(jnp.tile(swa_k_norm, 2)), sinkrow, batch)

    x = _out_proj((o_a, o_b, o_c, o_d), w_out, x, layer, tm=1024, tn=512)

    act = _ffn_up(x, row(norm_ffn), w_gate, w_up, layer, tm=1024, tn=512)
    return _mm_res(act, w_down, x, layer, tm=2048, tn=1024, tk=1024)


def kernel(x, norm_mix, w_in, moba_q_norm, moba_k_norm, gdn_conv, gdn_a_log, gdn_dt_bias, gdn_out_norm, sc_conv, swa_q_norm, swa_k_norm, swa_sinks, w_out, norm_ffn, w_gate, w_up, w_down):
    batch, seq, d = x.shape
    w_in_b = w_in[:, :, :A_COLS].astype(BF16)
    w_in_t = _w_in_tail(w_in)
    w_out_b = w_out.astype(BF16)
    w_gate_b = _bf16_zero_extend(w_gate, 2, D_FF_PAD)
    w_up_b = _bf16_zero_extend(w_up, 2, D_FF_PAD)
    w_down_b = _bf16_zero_extend(w_down, 1, D_FF_PAD)
    h = x.reshape(batch * seq, d)
    for l in range(norm_mix.shape[0]):
        h = _layer(h, batch, l, norm_mix[l], w_in_b, w_in_t, moba_q_norm[l], moba_k_norm[l], gdn_conv[l], gdn_a_log[l],
                   gdn_dt_bias[l], gdn_out_norm[l], sc_conv[l], swa_q_norm[l], swa_k_norm[l], swa_sinks[l],
                   w_out_b, norm_ffn[l], w_gate_b, w_up_b, w_down_b)
    return h.reshape(batch, seq, d)
```

```python
import functools

import jax
import jax.numpy as jnp
from jax import lax
from jax.experimental import pallas as pl
from jax.experimental.pallas import tpu as pltpu

F32 = jnp.float32
BF16 = jnp.bfloat16

EPS = 1e-6
LANES = 128
GROUP = 1024
HEAD_DIM = 128
N_HEADS = GROUP // HEAD_DIM
MOBA_BLOCK = 256
MOBA_TILE = 2 * MOBA_BLOCK
MOBA_TOPK = 3
GDN_CONV = 4
GDN_CHUNK = 64
SC_CONV = 3
SWA_D = 64
SWA_Q_HEADS = GROUP // SWA_D
SWA_KV_HEADS = 2
SWA_W = 128
NEG = -1e30
LOG2E = 1.4426950408889634
BIAS_ROWS = 16

CB_MQ, CB_MK, CB_MV = 0, 8, 16
CB_GQ, CB_GK, CB_GV, CB_GZ = 24, 32, 40, 48
CB_SCB, CB_SCC, CB_SCX = 56, 64, 72
CB_SQ, CB_SK, CB_SV, CB_GAB = 80, 88, 89, 90
NP_COLS = 92 * LANES
A_COLS = 6 * GROUP
D_FF_PAD = 11264

VMEM_LIMIT = 56 * 1024 * 1024


def _cparams(sem, vmem=VMEM_LIMIT):
    return pltpu.CompilerParams(dimension_semantics=sem, vmem_limit_bytes=vmem)


def _bdot(a, b):
    return jnp.dot(a.astype(BF16), b.astype(BF16), preferred_element_type=F32)


def _bdot_nt(a, b):
    return lax.dot_general(a.astype(BF16), b.astype(BF16), (((1,), (1,)), ((), ())),
                           preferred_element_type=F32)


def _sigmoid(x):
    return 1.0 / (1.0 + jnp.exp(-x))


def _silu(x):
    return x * _sigmoid(x)


def _rms_rows(x, g):
    ms = jnp.mean(x * x, axis=-1, keepdims=True)
    return x * lax.rsqrt(ms + EPS) * g


NORM_CHUNK = 256


def _norm_rows_to(x_ref, g_ref, hn_ref):
    chunk = min(NORM_CHUNK, x_ref.shape[0])

    def body(c, carry):
        rs = pl.ds(pl.multiple_of(c * chunk, chunk), chunk)
        hn_ref[rs, :] = _rms_rows(x_ref[rs, :], g_ref[...]).astype(BF16)
        return carry

    lax.fori_loop(0, x_ref.shape[0] // chunk, body, 0)


def _x_row_spec(tm, d):
    return pl.BlockSpec((tm, d), lambda i, j: (i, 0), pipeline_mode=pl.Buffered(1))


def _in_proj_kernel(x_ref, g_ref, wa_ref, wb_ref, o_ref, hn_ref, *, na):
    j = pl.program_id(1)

    @pl.when(j == 0)
    def _():
        _norm_rows_to(x_ref, g_ref, hn_ref)

    @pl.when(j < na)
    def _():
        o_ref[...] = jnp.dot(hn_ref[...], wa_ref[...], preferred_element_type=F32).astype(o_ref.dtype)

    @pl.when(j >= na)
    def _():
        o_ref[...] = jnp.dot(hn_ref[...], wb_ref[...], preferred_element_type=F32).astype(o_ref.dtype)


def _in_proj(x, g, wa, wb, layer, tm, tn):
    t, d = x.shape
    na = A_COLS // tn
    nbt = wb.shape[2] // tn
    tm = min(tm, t)
    return pl.pallas_call(
        functools.partial(_in_proj_kernel, na=na),
        grid=(t // tm, na + nbt),
        in_specs=[_x_row_spec(tm, d),
                  pl.BlockSpec((1, d), lambda i, j: (0, 0)),
                  pl.BlockSpec((None, d, tn), lambda i, j: (layer, 0, jnp.minimum(j, na - 1))),
                  pl.BlockSpec((None, d, tn), lambda i, j: (layer, 0, jnp.maximum(j - na, 0)))],
        out_specs=pl.BlockSpec((tm, tn), lambda i, j: (i, j)),
        out_shape=jax.ShapeDtypeStruct((t, (na + nbt) * tn), F32),
        scratch_shapes=[pltpu.VMEM((tm, d), BF16)],
        compiler_params=_cparams(("parallel", "arbitrary")),
        name="in_proj",
    )(x, g, wa, wb)


def _ffn_up_kernel(x_ref, g_ref, wg_ref, wu_ref, o_ref, hn_ref):
    @pl.when(pl.program_id(1) == 0)
    def _():
        _norm_rows_to(x_ref, g_ref, hn_ref)

    h = hn_ref[...]
    a = jnp.dot(h, wg_ref[...], preferred_element_type=F32)
    b = jnp.dot(h, wu_ref[...], preferred_element_type=F32)
    o_ref[...] = (_silu(a) * b).astype(o_ref.dtype)


def _ffn_up(x, g, wg, wu, layer, tm, tn):
    t, d = x.shape
    n = wg.shape[2]
    tm = min(tm, t)
    w_spec = pl.BlockSpec((None, d, tn), lambda i, j: (layer, 0, j))
    return pl.pallas_call(
        _ffn_up_kernel,
        grid=(t // tm, n // tn),
        in_specs=[_x_row_spec(tm, d),
                  pl.BlockSpec((1, d), lambda i, j: (0, 0)),
                  w_spec, w_spec],
        out_specs=pl.BlockSpec((tm, tn), lambda i, j: (i, j)),
        out_shape=jax.ShapeDtypeStruct((t, n), BF16),
        scratch_shapes=[pltpu.VMEM((tm, d), BF16)],
        compiler_params=_cparams(("parallel", "arbitrary")),
        name="ffn_up",
    )(x, g, wg, wu)


def _mm_res_kernel(a_ref, w_ref, r_ref, o_ref):
    k = pl.program_id(2)

    @pl.when(k == 0)
    def _():
        o_ref[...] = r_ref[...] + jnp.dot(a_ref[...], w_ref[...], preferred_element_type=F32)

    @pl.when(k > 0)
    def _():
        o_ref[...] += jnp.dot(a_ref[...], w_ref[...], preferred_element_type=F32)


def _mm_res(a, w, r, layer, tm, tn, tk):
    t, kd = a.shape
    n = w.shape[2]
    tm = min(tm, t)
    return pl.pallas_call(
        _mm_res_kernel,
        grid=(t // tm, n // tn, kd // tk),
        in_specs=[pl.BlockSpec((tm, tk), lambda i, j, k: (i, k)),
                  pl.BlockSpec((None, tk, tn), lambda i, j, k: (layer, k, j)),
                  pl.BlockSpec((tm, tn), lambda i, j, k: (i, j))],
        out_specs=pl.BlockSpec((tm, tn), lambda i, j, k: (i, j)),
        out_shape=jax.ShapeDtypeStruct((t, n), F32),
        compiler_params=_cparams(("parallel", "parallel", "arbitrary")),
        name="mm_res",
    )(a, w, r)


def _out_proj_kernel(a0_ref, a1_ref, a2_ref, a3_ref, w_ref, r_ref, o_ref):
    acc = r_ref[...]
    for g, a_ref in enumerate((a0_ref, a1_ref, a2_ref, a3_ref)):
        acc = acc + jnp.dot(a_ref[...], w_ref[g * GROUP:(g + 1) * GROUP, :], preferred_element_type=F32)
    o_ref[...] = acc


def _out_proj(mix, w, r, layer, tm, tn):
    t = r.shape[0]
    n = w.shape[2]
    tm = min(tm, t)
    a_spec = pl.BlockSpec((tm, GROUP), lambda i, j: (i, 0))
    return pl.pallas_call(
        _out_proj_kernel,
        grid=(t // tm, n // tn),
        in_specs=[a_spec, a_spec, a_spec, a_spec,
                  pl.BlockSpec((None, 4 * GROUP, tn), lambda i, j: (layer, 0, j)),
                  pl.BlockSpec((tm, tn), lambda i, j: (i, j))],
        out_specs=pl.BlockSpec((tm, tn), lambda i, j: (i, j)),
        out_shape=jax.ShapeDtypeStruct((t, n), F32),
        compiler_params=_cparams(("parallel", "arbitrary")),
        name="out_proj",
    )(*mix, w, r)


def _moba_prep_kernel(k_ref, v_ref, g_ref, kn_ref, km_ref, vt_ref):
    g = g_ref[...]
    for h in range(N_HEADS):
        sl = slice(h * HEAD_DIM, (h + 1) * HEAD_DIM)
        kn = _rms_rows(k_ref[:, sl], g)
        kn_ref[:, sl] = kn.astype(BF16)
        for half in range(MOBA_TILE // MOBA_BLOCK):
            km_ref[half, :, sl] = jnp.mean(kn[half * MOBA_BLOCK:(half + 1) * MOBA_BLOCK], axis=0, keepdims=True)
        vt_ref[0, sl, :] = v_ref[:, sl].T.astype(BF16)


def _moba_prep(proj, gk):
    t = proj.shape[0]
    ntile = t // MOBA_TILE
    per = MOBA_TILE // MOBA_BLOCK
    return pl.pallas_call(
        _moba_prep_kernel,
        grid=(ntile,),
        in_specs=[pl.BlockSpec((MOBA_TILE, GROUP), lambda i: (i, CB_MK // 8)),
                  pl.BlockSpec((MOBA_TILE, GROUP), lambda i: (i, CB_MV // 8)),
                  pl.BlockSpec((1, HEAD_DIM), lambda i: (0, 0))],
        out_specs=[pl.BlockSpec((MOBA_TILE, GROUP), lambda i: (i, 0)),
                   pl.BlockSpec((per, 1, GROUP), lambda i: (i, 0, 0)),
                   pl.BlockSpec((1, GROUP, MOBA_TILE), lambda i: (i, 0, 0))],
        out_shape=[jax.ShapeDtypeStruct((t, GROUP), BF16),
                   jax.ShapeDtypeStruct((ntile * per, 1, GROUP), F32),
                   jax.ShapeDtypeStruct((ntile, GROUP, MOBA_TILE), BF16)],
        compiler_params=_cparams(("parallel",)),
        name="moba_prep",
    )(proj, proj, gk)


def _moba_kernel(q_ref, k_ref, vt_ref, km_ref, g_ref, o_ref, bias_ref, acc_ref, qaug_ref, s0_ref, s1_ref,
                 p0_ref, p1_ref, *, nb):
    ti = pl.program_id(2)
    blk, tile = MOBA_BLOCK, MOBA_TILE
    qn = _rms_rows(q_ref[...], g_ref[...])

    gate = lax.dot_general(km_ref[0], qn, (((1,), (1,)), ((), ())),
                           precision=lax.Precision.HIGHEST, preferred_element_type=F32)
    row = lax.broadcasted_iota(jnp.int32, gate.shape, 0)
    own = 2 * ti + (lax.broadcasted_iota(jnp.int32, gate.shape, 1) >= blk).astype(jnp.int32)
    rowf = row.astype(F32)
    gate = jnp.where(row < own, gate, -jnp.inf)
    bias = jnp.where(row == own, 0.0, NEG)
    for _ in range(MOBA_TOPK):
        m = jnp.max(gate, axis=0, keepdims=True)
        idx = jnp.min(jnp.where(gate == m, rowf, float(nb)), axis=0, keepdims=True)
        hit = jnp.logical_and(rowf == idx, m > -jnp.inf)
        bias = jnp.where(hit, 0.0, bias)
        gate = jnp.where(hit, -jnp.inf, gate)
    bias_ref[...] = bias

    qaug_ref[0:HEAD_DIM, :] = (qn * (HEAD_DIM ** -0.5 * LOG2E)).T.astype(BF16)
    qaug_ref[HEAD_DIM + BIAS_ROWS:, :] = jnp.zeros((HEAD_DIM - BIAS_ROWS, tile), BF16)
    er = lax.broadcasted_iota(jnp.int32, (tile, HEAD_DIM), 0)
    ec = lax.broadcasted_iota(jnp.int32, (tile, HEAD_DIM), 1)
    onehot = jnp.where(ec == er // blk, 1.0, 0.0).astype(BF16)
    brow = lax.broadcasted_iota(jnp.int32, (BIAS_ROWS, tile), 0)
    ones_rows = jnp.ones((BIAS_ROWS, tile), BF16)

    def scores(t, valid):
        b0 = jnp.where(valid, bias_ref[pl.ds(2 * t, 1), :], NEG)
        b1 = jnp.where(valid, bias_ref[pl.ds(2 * t + 1, 1), :], NEG)
        qaug_ref[HEAD_DIM:HEAD_DIM + BIAS_ROWS, :] = jnp.where(
            brow == 0, b0, jnp.where(brow == 1, b1, 0.0)).astype(BF16)
        kt = k_ref[pl.ds(pl.multiple_of(t * tile, tile), tile), :]
        return jnp.dot(jnp.concatenate([kt, onehot], axis=1), qaug_ref[...],
                       preferred_element_type=F32)

    s_refs, p_refs = (s0_ref, s1_ref), (p0_ref, p1_ref)
    kr = lax.broadcasted_iota(jnp.int32, (tile, tile), 0)
    qc = lax.broadcasted_iota(jnp.int32, (tile, tile), 1)
    s_refs[0][...] = jnp.where(kr <= qc, scores(ti, True), NEG)
    p_refs[1][...] = jnp.zeros((tile, tile), BF16)
    acc_ref[...] = jnp.zeros_like(acc_ref)

    def trip(n, par, carry):
        m, alpha_prev = carry
        tc = jnp.where(n == 1, ti, jnp.clip(n - 2, 0, ti))
        pv = jnp.dot(jnp.concatenate([vt_ref[tc], ones_rows], axis=0), p_refs[1 - par][...],
                     preferred_element_type=F32)
        s_refs[1 - par][...] = scores(jnp.minimum(n, ti), n < ti)
        s = s_refs[par][...]
        m_new = jnp.maximum(m, jnp.max(s, axis=0, keepdims=True))
        alpha = jnp.exp2(m - m_new)
        p_refs[par][...] = jnp.exp2(s - m_new).astype(BF16)
        acc_ref[...] = acc_ref[...] * alpha_prev + pv
        return m_new, alpha

    def body(j, carry):
        return trip(2 * j + 1, 1, trip(2 * j, 0, carry))

    init = (jnp.full((1, tile), NEG, F32), jnp.ones((1, tile), F32))
    lax.fori_loop(0, (ti + 3) // 2, body, init)
    o_ref[...] = (acc_ref[0:HEAD_DIM, :] / acc_ref[HEAD_DIM:HEAD_DIM + 1, :]).T.astype(o_ref.dtype)


def _moba(proj, kn, km, vt, gq, batch):
    t = proj.shape[0]
    s = t // batch
    nb = s // MOBA_BLOCK
    nt = s // MOBA_TILE
    km = km.reshape(batch, nb, GROUP)
    return pl.pallas_call(
        functools.partial(_moba_kernel, nb=nb),
        grid=(batch, N_HEADS, nt),
        in_specs=[pl.BlockSpec((MOBA_TILE, HEAD_DIM), lambda b, h, i: (b * nt + i, CB_MQ + h)),
                  pl.BlockSpec((s, HEAD_DIM), lambda b, h, i: (b, h)),
                  pl.BlockSpec((nt, HEAD_DIM, MOBA_TILE), lambda b, h, i: (b, h, 0)),
                  pl.BlockSpec((1, nb, HEAD_DIM), lambda b, h, i: (b, 0, h)),
                  pl.BlockSpec((1, HEAD_DIM), lambda b, h, i: (0, 0))],
        out_specs=pl.BlockSpec((MOBA_TILE, HEAD_DIM), lambda b, h, i: (b * nt + i, h)),
        out_shape=jax.ShapeDtypeStruct((t, GROUP), BF16),
        scratch_shapes=[pltpu.VMEM((nb, MOBA_TILE), F32),
                        pltpu.VMEM((HEAD_DIM + BIAS_ROWS, MOBA_TILE), F32),
                        pltpu.VMEM((2 * HEAD_DIM, MOBA_TILE), BF16),
                        pltpu.VMEM((MOBA_TILE, MOBA_TILE), F32), pltpu.VMEM((MOBA_TILE, MOBA_TILE), F32),
                        pltpu.VMEM((MOBA_TILE, MOBA_TILE), BF16), pltpu.VMEM((MOBA_TILE, MOBA_TILE), BF16)],
        compiler_params=_cparams(("parallel", "parallel", "arbitrary")),
        name="moba",
    )(proj, kn, vt, km, gq)


GDN_ROWS = 256
GDN_HPS = 4


def _conv_silu(x_ref, halo_ref, w_ref, first):
    r = x_ref.shape[0]
    halo = halo_ref[...] * jnp.where(first, 0.0, 1.0)
    xb = jnp.concatenate([halo, x_ref[...]], axis=0)
    w = w_ref[...]
    out = None
    for tap in range(GDN_CONV):
        sh = GDN_CONV - 1 - tap
        xs = xb if sh == 0 else pltpu.roll(xb, sh, axis=0)
        term = xs[8:8 + r] * w[tap:tap + 1, :]
        out = term if out is None else out + term
    return _silu(out)


def _gdn_heads(qs, ks, vs, gs, betas, states):
    r = qs[0].shape[0]
    c = GDN_CHUNK
    pair = 2 * c
    npair = r // pair
    nh = len(qs)
    row = lax.broadcasted_iota(jnp.int32, (pair, pair), 0)
    col = lax.broadcasted_iota(jnp.int32, (pair, pair), 1)
    same = (row // c) == (col // c)
    tril = jnp.logical_and(same, row >= col)
    strict = jnp.logical_and(same, row > col)
    eye = (row == col).astype(F32)
    rin = row % c
    units = [(h, pi) for pi in range(npair) for h in range(nh)]

    def rows(x, u):
        return x[u[0]][u[1] * pair:(u[1] + 1) * pair]

    gcum = {u: rows(gs, u) for u in units}
    sh = 1
    while sh < c:
        gcum = {u: gcum[u] + jnp.where(rin >= sh, pltpu.roll(gcum[u], sh, axis=0), 0.0) for u in units}
        sh *= 2
    decay = {u: jnp.exp(jnp.where(tril, gcum[u] - gcum[u].T, -jnp.inf)) for u in units}
    eg = {u: jnp.exp(gcum[u]) for u in units}
    g_end = {u: (gcum[u][c - 1:c, :], gcum[u][pair - 1:pair, :]) for u in units}
    kb = {u: rows(ks, u) * rows(betas, u) for u in units}
    vb = {u: rows(vs, u) * rows(betas, u) for u in units}
    lmat = {u: jnp.where(strict, _bdot_nt(kb[u], rows(ks, u)) * decay[u], 0.0) for u in units}
    qk = {u: _bdot_nt(rows(qs, u), rows(ks, u)) * decay[u] for u in units}
    tinv = {u: eye - lmat[u] for u in units}
    lpow = lmat
    span = 1
    while 2 * span < c:
        lpow = {u: _bdot(lpow[u], lpow[u]) for u in units}
        tinv = {u: tinv[u] + _bdot(tinv[u], lpow[u]) for u in units}
        span *= 2
    uw = {u: _bdot(tinv[u], jnp.concatenate([vb[u], kb[u] * eg[u]], axis=1)) for u in units}
    qd = {u: rows(qs, u) * eg[u] for u in units}
    kdt = {u: (rows(ks, u) * jnp.exp(jnp.where(row < c, g_end[u][0], g_end[u][1]) - gcum[u])).T for u in units}

    states = list(states)
    vns = {u: [] for u in units}
    o_st = {u: [] for u in units}
    for pi in range(npair):
        for ci in range(2):
            cs = slice(ci * c, (ci + 1) * c)
            for h in range(nh):
                u = (h, pi)
                ws = _bdot(jnp.concatenate([uw[u][cs, HEAD_DIM:], qd[u][cs]], axis=0), states[h])
                vn = uw[u][cs, :HEAD_DIM] - ws[:c]
                o_st[u].append(ws[c:])
                vns[u].append(vn)
                zero = jnp.zeros_like(vn)
                vn_pad = jnp.concatenate([vn, zero] if ci == 0 else [zero, vn], axis=0)
                states[h] = states[h] * jnp.exp(g_end[u][ci]) + _bdot(kdt[u], vn_pad)
    outs = []
    for h in range(nh):
        parts = [jnp.concatenate(o_st[(h, pi)], axis=0) + _bdot(qk[(h, pi)], jnp.concatenate(vns[(h, pi)], axis=0))
                 for pi in range(npair)]
        outs.append(jnp.concatenate(parts, axis=0))
    return outs, states


def _gdn_kernel(q_ref, k_ref, v_ref, qh_ref, kh_ref, vh_ref, wq_ref, wk_ref, wv_ref,
                gab_ref, z_ref, hp_ref, gn_ref, o_ref, s_ref):
    hg = pl.program_id(1)
    first = pl.program_id(2) == 0

    @pl.when(first)
    def _():
        s_ref[...] = jnp.zeros_like(s_ref)

    q2 = _conv_silu(q_ref, qh_ref, wq_ref, first)
    k2 = _conv_silu(k_ref, kh_ref, wk_ref, first)
    v2 = _conv_silu(v_ref, vh_ref, wv_ref, first)
    gab = gab_ref[...]
    lane = lax.broadcasted_iota(jnp.int32, gab.shape, 1)

    qs, ks, vs, gs, betas = [], [], [], [], []
    for hh in range(GDN_HPS):
        h = hg * GDN_HPS + hh
        sl = slice(hh * HEAD_DIM, (hh + 1) * HEAD_DIM)
        q, k = q2[:, sl], k2[:, sl]
        qs.append(q * lax.rsqrt(jnp.sum(q * q, axis=-1, keepdims=True) + EPS) * (HEAD_DIM ** -0.5))
        ks.append(k * lax.rsqrt(jnp.sum(k * k, axis=-1, keepdims=True) + EPS))
        vs.append(v2[:, sl])
        ga = jnp.sum(jnp.where(lane == h, gab, 0.0), axis=1, keepdims=True)
        gb = jnp.sum(jnp.where(lane == N_HEADS + h, gab, 0.0), axis=1, keepdims=True)
        a_log = hp_ref[hh, 0:1, :]
        dt_bias = hp_ref[hh, 1:2, :]
        xg = ga + dt_bias
        softplus = jnp.maximum(xg, 0.0) + jnp.log1p(jnp.exp(-jnp.abs(xg)))
        gs.append(-jnp.exp(a_log) * softplus)
        betas.append(_sigmoid(gb + jnp.zeros_like(xg)))

    outs, states = _gdn_heads(qs, ks, vs, gs, betas, [s_ref[hh] for hh in range(GDN_HPS)])
    for hh in range(GDN_HPS):
        sl = slice(hh * HEAD_DIM, (hh + 1) * HEAD_DIM)
        s_ref[hh] = states[hh]
        on = _rms_rows(outs[hh], gn_ref[...])
        o_ref[:, sl] = (on * _silu(z_ref[:, sl])).astype(o_ref.dtype)


def _gdn(proj, conv_w, hp, gn, batch):
    t = proj.shape[0]
    s = t // batch
    r = GDN_ROWS
    steps = s // r
    hb = r // 8
    wide = GDN_HPS * HEAD_DIM
    cpb = GDN_HPS

    def main(cb):
        return pl.BlockSpec((r, wide), lambda b, h, i: (b * steps + i, cb // cpb + h))

    def halo(cb):
        return pl.BlockSpec((8, wide), lambda b, h, i: (jnp.maximum((b * steps + i) * hb - 1, 0), cb // cpb + h))

    def wspec(off):
        return pl.BlockSpec((GDN_CONV, wide), lambda b, h, i: (0, off // cpb + h))

    return pl.pallas_call(
        _gdn_kernel,
        grid=(batch, N_HEADS // GDN_HPS, steps),
        in_specs=[main(CB_GQ), main(CB_GK), main(CB_GV), halo(CB_GQ), halo(CB_GK), halo(CB_GV),
                  wspec(0), wspec(N_HEADS), wspec(2 * N_HEADS),
                  pl.BlockSpec((r, LANES), lambda b, h, i: (b * steps + i, CB_GAB)),
                  main(CB_GZ),
                  pl.BlockSpec((GDN_HPS, 2, LANES), lambda b, h, i: (h, 0, 0)),
                  pl.BlockSpec((1, HEAD_DIM), lambda b, h, i: (0, 0))],
        out_specs=pl.BlockSpec((r, wide), lambda b, h, i: (b * steps + i, h)),
        out_shape=jax.ShapeDtypeStruct((t, GROUP), BF16),
        scratch_shapes=[pltpu.VMEM((GDN_HPS, HEAD_DIM, HEAD_DIM), F32)],
        compiler_params=_cparams(("parallel", "parallel", "arbitrary")),
        name="gdn",
    )(proj, proj, proj, proj, proj, proj, conv_w, conv_w, conv_w, proj, proj, hp, gn)


SC_ROWS = 512


def _sconv_kernel(b_ref, c_ref, x_ref, ch_ref, xh_ref, w_ref, o_ref, *, steps):
    first = pl.program_id(0) % steps == 0
    r = b_ref.shape[0]
    y = c_ref[...] * x_ref[...]
    yh = ch_ref[...] * xh_ref[...] * jnp.where(first, 0.0, 1.0)
    yb = jnp.concatenate([yh, y], axis=0)
    w = w_ref[...]
    out = None
    for tap in range(SC_CONV):
        sh = SC_CONV - 1 - tap
        ys = yb if sh == 0 else pltpu.roll(yb, sh, axis=0)
        term = ys[8:8 + r] * w[tap:tap + 1, :]
        out = term if out is None else out + term
    o_ref[...] = (b_ref[...] * out).astype(o_ref.dtype)


def _sconv(proj, w, batch):
    t = proj.shape[0]
    r = min(SC_ROWS, t // batch)
    steps = (t // batch) // r
    hb = r // 8

    def main(cb):
        return pl.BlockSpec((r, GROUP), lambda i: (i, cb // 8))

    def halo(cb):
        return pl.BlockSpec((8, GROUP), lambda i: (jnp.maximum(i * hb - 1, 0), cb // 8))

    return pl.pallas_call(
        functools.partial(_sconv_kernel, steps=steps),
        grid=(t // r,),
        in_specs=[main(CB_SCB), main(CB_SCC), main(CB_SCX), halo(CB_SCC), halo(CB_SCX),
                  pl.BlockSpec((SC_CONV, GROUP), lambda i: (0, 0))],
        out_specs=pl.BlockSpec((r, GROUP), lambda i: (i, 0)),
        out_shape=jax.ShapeDtypeStruct((t, GROUP), BF16),
        compiler_params=_cparams(("parallel",)),
        name="sconv",
    )(proj, proj, proj, proj, proj, w)


SWA_ROWS = 512


def _half_rms(x, g2):
    lane = lax.broadcasted_iota(jnp.int32, x.shape, 1)
    lo = lane < SWA_D
    x2 = x * x
    ms_lo = jnp.sum(jnp.where(lo, x2, 0.0), axis=-1, keepdims=True) * (1.0 / SWA_D)
    ms_hi = jnp.sum(jnp.where(lo, 0.0, x2), axis=-1, keepdims=True) * (1.0 / SWA_D)
    rs = jnp.where(lo, lax.rsqrt(ms_lo + EPS), lax.rsqrt(ms_hi + EPS))
    return x * rs * g2


def _swa_kernel(q_ref, k_ref, v_ref, kh_ref, vh_ref, gq_ref, gk_ref, sink_ref, o_ref, *, steps):
    first = pl.program_id(0) % steps == 0
    r = q_ref.shape[0]
    w = SWA_W
    nsub = r // w
    pairs = SWA_Q_HEADS // 2
    lane = lax.broadcasted_iota(jnp.int32, (r + w, LANES), 1)

    kn = _half_rms(jnp.concatenate([kh_ref[...], k_ref[...]], axis=0), gk_ref[...])
    kroll = pltpu.roll(kn, SWA_D, axis=1)
    kdup = (jnp.where(lane < SWA_D, kn, kroll).astype(BF16),
            jnp.where(lane < SWA_D, kroll, kn).astype(BF16))
    vt = jnp.concatenate([vh_ref[...], v_ref[...]], axis=0).T.astype(BF16)

    kr = lax.broadcasted_iota(jnp.int32, (2 * w, 2 * w), 0)
    qc = lax.broadcasted_iota(jnp.int32, (2 * w, 2 * w), 1) % w
    band = jnp.logical_and(kr > qc, kr <= qc + w)
    band0 = jnp.logical_and(band, kr >= jnp.where(first, w, 0))
    qlane = lax.broadcasted_iota(jnp.int32, (w, LANES), 1)

    for sub in range(nsub):
        mask = band0 if sub == 0 else band
        ks = slice(sub * w, sub * w + 2 * w)
        for c in range(pairs):
            kv = c // (pairs // SWA_KV_HEADS)
            qn = _half_rms(q_ref[sub * w:(sub + 1) * w, c * LANES:(c + 1) * LANES], gq_ref[...]) * (SWA_D ** -0.5)
            qa = jnp.where(qlane < SWA_D, qn, 0.0)
            qb = jnp.where(qlane < SWA_D, 0.0, qn)
            qt = jnp.concatenate([qa.T, qb.T], axis=1).astype(BF16)
            s = jnp.dot(kdup[kv][ks], qt, preferred_element_type=F32)
            s = jnp.where(mask, s, NEG)
            sink = sink_ref[c:c + 1, :]
            m = jnp.maximum(jnp.max(s, axis=0, keepdims=True), sink)
            p = jnp.exp(s - m)
            l = jnp.sum(p, axis=0, keepdims=True) + jnp.exp(sink - m)
            pn = (p * (1.0 / l)).astype(BF16)
            ot = jnp.dot(vt[kv * SWA_D:(kv + 1) * SWA_D, ks], pn, preferred_element_type=F32)
            o = jnp.concatenate([ot[:, :w], ot[:, w:]], axis=0).T
            o_ref[sub * w:(sub + 1) * w, c * LANES:(c + 1) * LANES] = o.astype(o_ref.dtype)


def _swa(proj, gq2, gk2, sinkrow, batch):
    t = proj.shape[0]
    r = min(SWA_ROWS, t // batch)
    steps = (t // batch) // r
    hb = r // SWA_W

    def halo(cb):
        return pl.BlockSpec((SWA_W, LANES), lambda i: (jnp.maximum(i * hb - 1, 0), cb))

    return pl.pallas_call(
        functools.partial(_swa_kernel, steps=steps),
        grid=(t // r,),
        in_specs=[pl.BlockSpec((r, GROUP), lambda i: (i, CB_SQ // 8)),
                  pl.BlockSpec((r, LANES), lambda i: (i, CB_SK)),
                  pl.BlockSpec((r, LANES), lambda i: (i, CB_SV)),
                  halo(CB_SK), halo(CB_SV),
                  pl.BlockSpec((1, LANES), lambda i: (0, 0)),
                  pl.BlockSpec((1, LANES), lambda i: (0, 0)),
                  pl.BlockSpec((SWA_Q_HEADS // 2, 2 * SWA_W), lambda i: (0, 0))],
        out_specs=pl.BlockSpec((r, GROUP), lambda i: (i, 0)),
        out_shape=jax.ShapeDtypeStruct((t, GROUP), BF16),
        compiler_params=_cparams(("parallel",)),
        name="swa",
    )(proj, proj, proj, proj, proj, gq2, gk2, sinkrow)


def _w_in_tail(w, n_real):
    a1 = A_COLS + 2 * N_HEADS
    return jnp.concatenate([w[:, :, a1:n_real], w[:, :, A_COLS:a1], w[:, :, n_real:]], axis=2)


def _cast_extend_kernel(x_ref, o_ref, *, axis, valid):
    idx = pl.program_id(axis) * x_ref.shape[axis - 1] + lax.broadcasted_iota(jnp.int32, x_ref.shape, axis - 1)
    o_ref[...] = jnp.where(idx < valid, x_ref[...], 0.0).astype(o_ref.dtype)


def _bf16_zero_extend(w, axis, size, tr=1024, tc=1024):
    nl, r, c = w.shape
    out_shape = (nl, size, c) if axis == 1 else (nl, r, size)
    spec = pl.BlockSpec((None, tr, tc), lambda l, i, j: (l, i, j))
    return pl.pallas_call(
        functools.partial(_cast_extend_kernel, axis=axis, valid=w.shape[axis]),
        grid=(nl, out_shape[1] // tr, out_shape[2] // tc),
        in_specs=[spec],
        out_specs=spec,
        out_shape=jax.ShapeDtypeStruct(out_shape, BF16),
        compiler_params=_cparams(("parallel", "parallel", "parallel")),
        name="cast_extend",
    )(w)


def _layer(x, batch, layer, norm_mix, w_in, w_in_tail, moba_q_norm, moba_k_norm, gdn_conv, gdn_a_log, gdn_dt_bias,
           gdn_out_norm, sc_conv, swa_q_norm, swa_k_norm, swa_sinks, w_out, norm_ffn, w_gate, w_up, w_down):
    row = lambda a: a.reshape(1, -1).astype(F32)

    proj = _in_proj(x, row(norm_mix), w_in, w_in_tail, layer, tm=1024, tn=512)

    kn, km, vt = _moba_prep(proj, row(moba_k_norm))
    o_a = _moba(proj, kn, km, vt, row(moba_q_norm), batch)

    hp = jnp.broadcast_to(jnp.stack([gdn_a_log, gdn_dt_bias], axis=1)[:, :, None], (N_HEADS, 2, LANES)).astype(F32)
    o_b = _gdn(proj, gdn_conv.astype(F32), hp, row(gdn_out_norm), batch)

    o_c = _sconv(proj, sc_conv.astype(F32), batch)

    sinkrow = jnp.repeat(swa_sinks.astype(F32), SWA_W).reshape(SWA_Q_HEADS // 2, 2 * SWA_W)
    o_d = _swa(proj, row(jnp.tile(swa_q_norm, 2)), row(jnp.tile(swa_k_norm, 2)), sinkrow, batch)

    x = _out_proj((o_a, o_b, o_c, o_d), w_out, x, layer, tm=1024, tn=512)

    act = _ffn_up(x, row(norm_ffn), w_gate, w_up, layer, tm=1024, tn=512)
    return _mm_res(act, w_down, x, layer, tm=2048, tn=1024, tk=1024)


def kernel(x, norm_mix, w_in, moba_q_norm, moba_k_norm, gdn_conv, gdn_a_log, gdn_dt_bias, gdn_out_norm, sc_conv, swa_q_norm, swa_k_norm, swa_sinks, w_out, norm_ffn, w_gate, w_up, w_down):
    batch, seq, d = x.shape
    w_in_b = _bf16_zero_extend(w_in, 2, NP_COLS, tc=512)
    w_in_t = _w_in_tail(w_in_b, w_in.shape[2])
    w_out_b = w_out.astype(BF16)
    w_gate_b = _bf16_zero_extend(w_gate, 2, D_FF_PAD)
    w_up_b = _bf16_zero_extend(w_up, 2, D_FF_PAD)
    w_down_b = _bf16_zero_extend(w_down, 1, D_FF_PAD)
    h = x.reshape(batch * seq, d)
    for l in range(norm_mix.shape[0]):
        h = _layer(h, batch, l, norm_mix[l], w_in_b, w_in_t, moba_q_norm[l], moba_k_norm[l], gdn_conv[l], gdn_a_log[l],
                   gdn_dt_bias[l], gdn_out_norm[l], sc_conv[l], swa_q_norm[l], swa_k_norm[l], swa_sinks[l],
                   w_out_b, norm_ffn[l], w_gate_b, w_up_b, w_down_b)
    return h.reshape(batch, seq, d)
```

```python
import functools

import jax
import jax.numpy as jnp
from jax import lax
from jax.experimental import pallas as pl
from jax.experimental.pallas import tpu as pltpu

F32 = jnp.float32
BF16 = jnp.bfloat16

EPS = 1e-6
LANES = 128
GROUP = 1024
HEAD_DIM = 128
N_HEADS = GROUP // HEAD_DIM
MOBA_BLOCK = 256
MOBA_TILE = 2 * MOBA_BLOCK
MOBA_TOPK = 3
GDN_CONV = 4
GDN_CHUNK = 64
SC_CONV = 3
SWA_D = 64
SWA_Q_HEADS = GROUP // SWA_D
SWA_KV_HEADS = 2
SWA_W = 128
NEG = -1e30
LOG2E = 1.4426950408889634
BIAS_ROWS = 16

CB_MQ, CB_MK, CB_MV = 0, 8, 16
CB_GQ, CB_GK, CB_GV, CB_GZ = 24, 32, 40, 48
CB_SCB, CB_SCC, CB_SCX = 56, 64, 72
CB_SQ, CB_SK, CB_SV, CB_GAB = 80, 88, 89, 90
NP_COLS = 92 * LANES
A_COLS = 6 * GROUP
D_FF_PAD = 11264

VMEM_LIMIT = 56 * 1024 * 1024


def _cparams(sem, vmem=VMEM_LIMIT):
    return pltpu.CompilerParams(dimension_semantics=sem, vmem_limit_bytes=vmem)


def _bdot(a, b):
    return jnp.dot(a.astype(BF16), b.astype(BF16), preferred_element_type=F32)


def _bdot_nt(a, b):
    return lax.dot_general(a.astype(BF16), b.astype(BF16), (((1,), (1,)), ((), ())),
                           preferred_element_type=F32)


def _sigmoid(x):
    return 1.0 / (1.0 + jnp.exp(-x))


def _silu(x):
    return x * _sigmoid(x)


def _rms_rows(x, g):
    ms = jnp.mean(x * x, axis=-1, keepdims=True)
    return x * lax.rsqrt(ms + EPS) * g


NORM_CHUNK = 256


def _norm_rows_to(x_ref, g_ref, hn_ref):
    chunk = min(NORM_CHUNK, x_ref.shape[0])

    def body(c, carry):
        rs = pl.ds(pl.multiple_of(c * chunk, chunk), chunk)
        hn_ref[rs, :] = _rms_rows(x_ref[rs, :], g_ref[...]).astype(BF16)
        return carry

    lax.fori_loop(0, x_ref.shape[0] // chunk, body, 0)


def _x_row_spec(tm, d):
    return pl.BlockSpec((tm, d), lambda i, j: (i, 0), pipeline_mode=pl.Buffered(1))


def _in_proj_kernel(x_ref, g_ref, wa_ref, wb_ref, o_ref, hn_ref, *, na):
    j = pl.program_id(1)

    @pl.when(j == 0)
    def _():
        _norm_rows_to(x_ref, g_ref, hn_ref)

    @pl.when(j < na)
    def _():
        o_ref[...] = jnp.dot(hn_ref[...], wa_ref[...], preferred_element_type=F32).astype(o_ref.dtype)

    @pl.when(j >= na)
    def _():
        o_ref[...] = jnp.dot(hn_ref[...], wb_ref[...], preferred_element_type=F32).astype(o_ref.dtype)


def _in_proj(x, g, wa, wb, layer, tm, tn):
    t, d = x.shape
    na = A_COLS // tn
    nbt = wb.shape[2] // tn
    tm = min(tm, t)
    return pl.pallas_call(
        functools.partial(_in_proj_kernel, na=na),
        grid=(t // tm, na + nbt),
        in_specs=[_x_row_spec(tm, d),
                  pl.BlockSpec((1, d), lambda i, j: (0, 0)),
                  pl.BlockSpec((None, d, tn), lambda i, j: (layer, 0, jnp.minimum(j, na - 1))),
                  pl.BlockSpec((None, d, tn), lambda i, j: (layer, 0, jnp.maximum(j - na, 0)))],
        out_specs=pl.BlockSpec((tm, tn), lambda i, j: (i, j)),
        out_shape=jax.ShapeDtypeStruct((t, (na + nbt) * tn), F32),
        scratch_shapes=[pltpu.VMEM((tm, d), BF16)],
        compiler_params=_cparams(("parallel", "arbitrary")),
        name="in_proj",
    )(x, g, wa, wb)


def _ffn_up_kernel(x_ref, g_ref, wg_ref, wu_ref, o_ref, hn_ref):
    @pl.when(pl.program_id(1) == 0)
    def _():
        _norm_rows_to(x_ref, g_ref, hn_ref)

    h = hn_ref[...]
    a = jnp.dot(h, wg_ref[...], preferred_element_type=F32)
    b = jnp.dot(h, wu_ref[...], preferred_element_type=F32)
    o_ref[...] = (_silu(a) * b).astype(o_ref.dtype)


def _ffn_up(x, g, wg, wu, layer, tm, tn):
    t, d = x.shape
    n = wg.shape[2]
    tm = min(tm, t)
    w_spec = pl.BlockSpec((None, d, tn), lambda i, j: (layer, 0, j))
    return pl.pallas_call(
        _ffn_up_kernel,
        grid=(t // tm, n // tn),
        in_specs=[_x_row_spec(tm, d),
                  pl.BlockSpec((1, d), lambda i, j: (0, 0)),
                  w_spec, w_spec],
        out_specs=pl.BlockSpec((tm, tn), lambda i, j: (i, j)),
        out_shape=jax.ShapeDtypeStruct((t, n), BF16),
        scratch_shapes=[pltpu.VMEM((tm, d), BF16)],
        compiler_params=_cparams(("parallel", "arbitrary")),
        name="ffn_up",
    )(x, g, wg, wu)


def _mm_res_kernel(a_ref, w_ref, r_ref, o_ref):
    k = pl.program_id(2)

    @pl.when(k == 0)
    def _():
        o_ref[...] = r_ref[...] + jnp.dot(a_ref[...], w_ref[...], preferred_element_type=F32)

    @pl.when(k > 0)
    def _():
        o_ref[...] += jnp.dot(a_ref[...], w_ref[...], preferred_element_type=F32)


def _mm_res(a, w, r, layer, tm, tn, tk):
    t, kd = a.shape
    n = w.shape[2]
    tm = min(tm, t)
    return pl.pallas_call(
        _mm_res_kernel,
        grid=(t // tm, n // tn, kd // tk),
        in_specs=[pl.BlockSpec((tm, tk), lambda i, j, k: (i, k)),
                  pl.BlockSpec((None, tk, tn), lambda i, j, k: (layer, k, j)),
                  pl.BlockSpec((tm, tn), lambda i, j, k: (i, j))],
        out_specs=pl.BlockSpec((tm, tn), lambda i, j, k: (i, j)),
        out_shape=jax.ShapeDtypeStruct((t, n), F32),
        compiler_params=_cparams(("parallel", "parallel", "arbitrary")),
        name="mm_res",
    )(a, w, r)


def _out_proj_kernel(a0_ref, a1_ref, a2_ref, a3_ref, w_ref, r_ref, o_ref):
    acc = r_ref[...]
    for g, a_ref in enumerate((a0_ref, a1_ref, a2_ref, a3_ref)):
        acc = acc + jnp.dot(a_ref[...], w_ref[g * GROUP:(g + 1) * GROUP, :], preferred_element_type=F32)
    o_ref[...] = acc


def _out_proj(mix, w, r, layer, tm, tn):
    t = r.shape[0]
    n = w.shape[2]
    tm = min(tm, t)
    a_spec = pl.BlockSpec((tm, GROUP), lambda i, j: (i, 0))
    return pl.pallas_call(
        _out_proj_kernel,
        grid=(t // tm, n // tn),
        in_specs=[a_spec, a_spec, a_spec, a_spec,
                  pl.BlockSpec((None, 4 * GROUP, tn), lambda i, j: (layer, 0, j)),
                  pl.BlockSpec((tm, tn), lambda i, j: (i, j))],
        out_specs=pl.BlockSpec((tm, tn), lambda i, j: (i, j)),
        out_shape=jax.ShapeDtypeStruct((t, n), F32),
        compiler_params=_cparams(("parallel", "arbitrary")),
        name="out_proj",
    )(*mix, w, r)


def _moba_prep_kernel(k_ref, v_ref, g_ref, kn_ref, km_ref, vt_ref):
    g = g_ref[...]
    for h in range(N_HEADS):
        sl = slice(h * HEAD_DIM, (h + 1) * HEAD_DIM)
        kn = _rms_rows(k_ref[:, sl], g)
        kn_ref[:, sl] = kn.astype(BF16)
        for half in range(MOBA_TILE // MOBA_BLOCK):
            km_ref[half, :, sl] = jnp.mean(kn[half * MOBA_BLOCK:(half + 1) * MOBA_BLOCK], axis=0, keepdims=True)
        vt_ref[0, sl, :] = v_ref[:, sl].T.astype(BF16)


def _moba_prep(proj, gk):
    t = proj.shape[0]
    ntile = t // MOBA_TILE
    per = MOBA_TILE // MOBA_BLOCK
    return pl.pallas_call(
        _moba_prep_kernel,
        grid=(ntile,),
        in_specs=[pl.BlockSpec((MOBA_TILE, GROUP), lambda i: (i, CB_MK // 8)),
                  pl.BlockSpec((MOBA_TILE, GROUP), lambda i: (i, CB_MV // 8)),
                  pl.BlockSpec((1, HEAD_DIM), lambda i: (0, 0))],
        out_specs=[pl.BlockSpec((MOBA_TILE, GROUP), lambda i: (i, 0)),
                   pl.BlockSpec((per, 1, GROUP), lambda i: (i, 0, 0)),
                   pl.BlockSpec((1, GROUP, MOBA_TILE), lambda i: (i, 0, 0))],
        out_shape=[jax.ShapeDtypeStruct((t, GROUP), BF16),
                   jax.ShapeDtypeStruct((ntile * per, 1, GROUP), F32),
                   jax.ShapeDtypeStruct((ntile, GROUP, MOBA_TILE), BF16)],
        compiler_params=_cparams(("parallel",)),
        name="moba_prep",
    )(proj, proj, gk)


def _moba_kernel(q_ref, k_ref, vt_ref, km_ref, g_ref, o_ref, bias_ref, acc_ref, qaug_ref, s0_ref, s1_ref,
                 p0_ref, p1_ref, *, nb):
    ti = pl.program_id(2)
    blk, tile = MOBA_BLOCK, MOBA_TILE
    qn = _rms_rows(q_ref[...], g_ref[...])

    gate = lax.dot_general(km_ref[0], qn, (((1,), (1,)), ((), ())),
                           precision=lax.Precision.HIGHEST, preferred_element_type=F32)
    row = lax.broadcasted_iota(jnp.int32, gate.shape, 0)
    own = 2 * ti + (lax.broadcasted_iota(jnp.int32, gate.shape, 1) >= blk).astype(jnp.int32)
    rowf = row.astype(F32)
    gate = jnp.where(row < own, gate, -jnp.inf)
    bias = jnp.where(row == own, 0.0, NEG)
    for _ in range(MOBA_TOPK):
        m = jnp.max(gate, axis=0, keepdims=True)
        idx = jnp.min(jnp.where(gate == m, rowf, float(nb)), axis=0, keepdims=True)
        hit = jnp.logical_and(rowf == idx, m > -jnp.inf)
        bias = jnp.where(hit, 0.0, bias)
        gate = jnp.where(hit, -jnp.inf, gate)
    bias_ref[...] = bias

    qaug_ref[0:HEAD_DIM, :] = (qn * (HEAD_DIM ** -0.5 * LOG2E)).T.astype(BF16)
    qaug_ref[HEAD_DIM + BIAS_ROWS:, :] = jnp.zeros((HEAD_DIM - BIAS_ROWS, tile), BF16)
    er = lax.broadcasted_iota(jnp.int32, (tile, HEAD_DIM), 0)
    ec = lax.broadcasted_iota(jnp.int32, (tile, HEAD_DIM), 1)
    onehot = jnp.where(ec == er // blk, 1.0, 0.0).astype(BF16)
    brow = lax.broadcasted_iota(jnp.int32, (BIAS_ROWS, tile), 0)
    ones_rows = jnp.ones((BIAS_ROWS, tile), BF16)

    def scores(t, valid):
        b0 = jnp.where(valid, bias_ref[pl.ds(2 * t, 1), :], NEG)
        b1 = jnp.where(valid, bias_ref[pl.ds(2 * t + 1, 1), :], NEG)
        qaug_ref[HEAD_DIM:HEAD_DIM + BIAS_ROWS, :] = jnp.where(
            brow == 0, b0, jnp.where(brow == 1, b1, 0.0)).astype(BF16)
        kt = k_ref[pl.ds(pl.multiple_of(t * tile, tile), tile), :]
        return jnp.dot(jnp.concatenate([kt, onehot], axis=1), qaug_ref[...],
                       preferred_element_type=F32)

    s_refs, p_refs = (s0_ref, s1_ref), (p0_ref, p1_ref)
    kr = lax.broadcasted_iota(jnp.int32, (tile, tile), 0)
    qc = lax.broadcasted_iota(jnp.int32, (tile, tile), 1)
    s_refs[0][...] = jnp.where(kr <= qc, scores(ti, True), NEG)
    p_refs[1][...] = jnp.zeros((tile, tile), BF16)
    acc_ref[...] = jnp.zeros_like(acc_ref)

    def trip(n, par, carry):
        m, alpha_prev = carry
        tc = jnp.where(n == 1, ti, jnp.clip(n - 2, 0, ti))
        pv = jnp.dot(jnp.concatenate([vt_ref[tc], ones_rows], axis=0), p_refs[1 - par][...],
                     preferred_element_type=F32)
        s_refs[1 - par][...] = scores(jnp.minimum(n, ti), n < ti)
        s = s_refs[par][...]
        m_new = jnp.maximum(m, jnp.max(s, axis=0, keepdims=True))
        alpha = jnp.exp2(m - m_new)
        p_refs[par][...] = jnp.exp2(s - m_new).astype(BF16)
        acc_ref[...] = acc_ref[...] * alpha_prev + pv
        return m_new, alpha

    def body(j, carry):
        return trip(2 * j + 1, 1, trip(2 * j, 0, carry))

    init = (jnp.full((1, tile), NEG, F32), jnp.ones((1, tile), F32))
    lax.fori_loop(0, (ti + 3) // 2, body, init)
    o_ref[...] = (acc_ref[0:HEAD_DIM, :] / acc_ref[HEAD_DIM:HEAD_DIM + 1, :]).T.astype(o_ref.dtype)


def _moba(proj, kn, km, vt, gq, batch):
    t = proj.shape[0]
    s = t // batch
    nb = s // MOBA_BLOCK
    nt = s // MOBA_TILE
    km = km.reshape(batch, nb, GROUP)
    return pl.pallas_call(
        functools.partial(_moba_kernel, nb=nb),
        grid=(batch, N_HEADS, nt),
        in_specs=[pl.BlockSpec((MOBA_TILE, HEAD_DIM), lambda b, h, i: (b * nt + i, CB_MQ + h)),
                  pl.BlockSpec((s, HEAD_DIM), lambda b, h, i: (b, h)),
                  pl.BlockSpec((nt, HEAD_DIM, MOBA_TILE), lambda b, h, i: (b, h, 0)),
                  pl.BlockSpec((1, nb, HEAD_DIM), lambda b, h, i: (b, 0, h)),
                  pl.BlockSpec((1, HEAD_DIM), lambda b, h, i: (0, 0))],
        out_specs=pl.BlockSpec((MOBA_TILE, HEAD_DIM), lambda b, h, i: (b * nt + i, h)),
        out_shape=jax.ShapeDtypeStruct((t, GROUP), BF16),
        scratch_shapes=[pltpu.VMEM((nb, MOBA_TILE), F32),
                        pltpu.VMEM((HEAD_DIM + BIAS_ROWS, MOBA_TILE), F32),
                        pltpu.VMEM((2 * HEAD_DIM, MOBA_TILE), BF16),
                        pltpu.VMEM((MOBA_TILE, MOBA_TILE), F32), pltpu.VMEM((MOBA_TILE, MOBA_TILE), F32),
                        pltpu.VMEM((MOBA_TILE, MOBA_TILE), BF16), pltpu.VMEM((MOBA_TILE, MOBA_TILE), BF16)],
        compiler_params=_cparams(("parallel", "parallel", "arbitrary")),
        name="moba",
    )(proj, kn, vt, km, gq)


GDN_ROWS = 256
GDN_HPS = 4


def _conv_silu(x_ref, halo_ref, w_ref, first):
    r = x_ref.shape[0]
    halo = halo_ref[...] * jnp.where(first, 0.0, 1.0)
    xb = jnp.concatenate([halo, x_ref[...]], axis=0)
    w = w_ref[...]
    out = None
    for tap in range(GDN_CONV):
        sh = GDN_CONV - 1 - tap
        xs = xb if sh == 0 else pltpu.roll(xb, sh, axis=0)
        term = xs[8:8 + r] * w[tap:tap + 1, :]
        out = term if out is None else out + term
    return _silu(out)


def _gdn_heads(qs, ks, vs, gs, betas, states):
    r = qs[0].shape[0]
    c = GDN_CHUNK
    pair = 2 * c
    npair = r // pair
    nh = len(qs)
    row = lax.broadcasted_iota(jnp.int32, (pair, pair), 0)
    col = lax.broadcasted_iota(jnp.int32, (pair, pair), 1)
    same = (row // c) == (col // c)
    tril = jnp.logical_and(same, row >= col)
    strict = jnp.logical_and(same, row > col)
    eye = (row == col).astype(F32)
    rin = row % c
    units = [(h, pi) for pi in range(npair) for h in range(nh)]

    def rows(x, u):
        return x[u[0]][u[1] * pair:(u[1] + 1) * pair]

    gcum = {u: rows(gs, u) for u in units}
    sh = 1
    while sh < c:
        gcum = {u: gcum[u] + jnp.where(rin >= sh, pltpu.roll(gcum[u], sh, axis=0), 0.0) for u in units}
        sh *= 2
    decay = {u: jnp.exp(jnp.where(tril, gcum[u] - gcum[u].T, -jnp.inf)) for u in units}
    eg = {u: jnp.exp(gcum[u]) for u in units}
    g_end = {u: (gcum[u][c - 1:c, :], gcum[u][pair - 1:pair, :]) for u in units}
    kb = {u: rows(ks, u) * rows(betas, u) for u in units}
    vb = {u: rows(vs, u) * rows(betas, u) for u in units}
    lmat = {u: jnp.where(strict, _bdot_nt(kb[u], rows(ks, u)) * decay[u], 0.0) for u in units}
    qk = {u: _bdot_nt(rows(qs, u), rows(ks, u)) * decay[u] for u in units}
    tinv = {u: eye - lmat[u] for u in units}
    lpow = lmat
    span = 1
    while 2 * span < c:
        lpow = {u: _bdot(lpow[u], lpow[u]) for u in units}
        tinv = {u: tinv[u] + _bdot(tinv[u], lpow[u]) for u in units}
        span *= 2
    uw = {u: _bdot(tinv[u], jnp.concatenate([vb[u], kb[u] * eg[u]], axis=1)) for u in units}
    qd = {u: rows(qs, u) * eg[u] for u in units}
    kdt = {u: (rows(ks, u) * jnp.exp(jnp.where(row < c, g_end[u][0], g_end[u][1]) - gcum[u])).T for u in units}

    states = list(states)
    vns = {u: [] for u in units}
    o_st = {u: [] for u in units}
    for pi in range(npair):
        for ci in range(2):
            cs = slice(ci * c, (ci + 1) * c)
            for h in range(nh):
                u = (h, pi)
                ws = _bdot(jnp.concatenate([uw[u][cs, HEAD_DIM:], qd[u][cs]], axis=0), states[h])
                vn = uw[u][cs, :HEAD_DIM] - ws[:c]
                o_st[u].append(ws[c:])
                vns[u].append(vn)
                zero = jnp.zeros_like(vn)
                vn_pad = jnp.concatenate([vn, zero] if ci == 0 else [zero, vn], axis=0)
                states[h] = states[h] * jnp.exp(g_end[u][ci]) + _bdot(kdt[u], vn_pad)
    outs = []
    for h in range(nh):
        parts = [jnp.concatenate(o_st[(h, pi)], axis=0) + _bdot(qk[(h, pi)], jnp.concatenate(vns[(h, pi)], axis=0))
                 for pi in range(npair)]
        outs.append(jnp.concatenate(parts, axis=0))
    return outs, states


def _gdn_kernel(q_ref, k_ref, v_ref, qh_ref, kh_ref, vh_ref, wq_ref, wk_ref, wv_ref,
                gab_ref, z_ref, hp_ref, gn_ref, o_ref, s_ref):
    hg = pl.program_id(1)
    first = pl.program_id(2) == 0

    @pl.when(first)
    def _():
        s_ref[...] = jnp.zeros_like(s_ref)

    q2 = _conv_silu(q_ref, qh_ref, wq_ref, first)
    k2 = _conv_silu(k_ref, kh_ref, wk_ref, first)
    v2 = _conv_silu(v_ref, vh_ref, wv_ref, first)
    gab = gab_ref[...]
    lane = lax.broadcasted_iota(jnp.int32, gab.shape, 1)

    qs, ks, vs, gs, betas = [], [], [], [], []
    for hh in range(GDN_HPS):
        h = hg * GDN_HPS + hh
        sl = slice(hh * HEAD_DIM, (hh + 1) * HEAD_DIM)
        q, k = q2[:, sl], k2[:, sl]
        qs.append(q * lax.rsqrt(jnp.sum(q * q, axis=-1, keepdims=True) + EPS) * (HEAD_DIM ** -0.5))
        ks.append(k * lax.rsqrt(jnp.sum(k * k, axis=-1, keepdims=True) + EPS))
        vs.append(v2[:, sl])
        ga = jnp.sum(jnp.where(lane == h, gab, 0.0), axis=1, keepdims=True)
        gb = jnp.sum(jnp.where(lane == N_HEADS + h, gab, 0.0), axis=1, keepdims=True)
        a_log = hp_ref[hh, 0:1, :]
        dt_bias = hp_ref[hh, 1:2, :]
        xg = ga + dt_bias
        softplus = jnp.maximum(xg, 0.0) + jnp.log1p(jnp.exp(-jnp.abs(xg)))
        gs.append(-jnp.exp(a_log) * softplus)
        betas.append(_sigmoid(gb + jnp.zeros_like(xg)))

    outs, states = _gdn_heads(qs, ks, vs, gs, betas, [s_ref[hh] for hh in range(GDN_HPS)])
    for hh in range(GDN_HPS):
        sl = slice(hh * HEAD_DIM, (hh + 1) * HEAD_DIM)
        s_ref[hh] = states[hh]
        on = _rms_rows(outs[hh], gn_ref[...])
        o_ref[:, sl] = (on * _silu(z_ref[:, sl])).astype(o_ref.dtype)


def _gdn(proj, conv_w, hp, gn, batch):
    t = proj.shape[0]
    s = t // batch
    r = GDN_ROWS
    steps = s // r
    hb = r // 8
    wide = GDN_HPS * HEAD_DIM
    cpb = GDN_HPS

    def main(cb):
        return pl.BlockSpec((r, wide), lambda b, h, i: (b * steps + i, cb // cpb + h))

    def halo(cb):
        return pl.BlockSpec((8, wide), lambda b, h, i: (jnp.maximum((b * steps + i) * hb - 1, 0), cb // cpb + h))

    def wspec(off):
        return pl.BlockSpec((GDN_CONV, wide), lambda b, h, i: (0, off // cpb + h))

    return pl.pallas_call(
        _gdn_kernel,
        grid=(batch, N_HEADS // GDN_HPS, steps),
        in_specs=[main(CB_GQ), main(CB_GK), main(CB_GV), halo(CB_GQ), halo(CB_GK), halo(CB_GV),
                  wspec(0), wspec(N_HEADS), wspec(2 * N_HEADS),
                  pl.BlockSpec((r, LANES), lambda b, h, i: (b * steps + i, CB_GAB)),
                  main(CB_GZ),
                  pl.BlockSpec((GDN_HPS, 2, LANES), lambda b, h, i: (h, 0, 0)),
                  pl.BlockSpec((1, HEAD_DIM), lambda b, h, i: (0, 0))],
        out_specs=pl.BlockSpec((r, wide), lambda b, h, i: (b * steps + i, h)),
        out_shape=jax.ShapeDtypeStruct((t, GROUP), BF16),
        scratch_shapes=[pltpu.VMEM((GDN_HPS, HEAD_DIM, HEAD_DIM), F32)],
        compiler_params=_cparams(("parallel", "parallel", "arbitrary")),
        name="gdn",
    )(proj, proj, proj, proj, proj, proj, conv_w, conv_w, conv_w, proj, proj, hp, gn)


SC_ROWS = 512


def _sconv_kernel(b_ref, c_ref, x_ref, ch_ref, xh_ref, w_ref, o_ref, *, steps):
    first = pl.program_id(0) % steps == 0
    r = b_ref.shape[0]
    y = c_ref[...] * x_ref[...]
    yh = ch_ref[...] * xh_ref[...] * jnp.where(first, 0.0, 1.0)
    yb = jnp.concatenate([yh, y], axis=0)
    w = w_ref[...]
    out = None
    for tap in range(SC_CONV):
        sh = SC_CONV - 1 - tap
        ys = yb if sh == 0 else pltpu.roll(yb, sh, axis=0)
        term = ys[8:8 + r] * w[tap:tap + 1, :]
        out = term if out is None else out + term
    o_ref[...] = (b_ref[...] * out).astype(o_ref.dtype)


def _sconv(proj, w, batch):
    t = proj.shape[0]
    r = min(SC_ROWS, t // batch)
    steps = (t // batch) // r
    hb = r // 8

    def main(cb):
        return pl.BlockSpec((r, GROUP), lambda i: (i, cb // 8))

    def halo(cb):
        return pl.BlockSpec((8, GROUP), lambda i: (jnp.maximum(i * hb - 1, 0), cb // 8))

    return pl.pallas_call(
        functools.partial(_sconv_kernel, steps=steps),
        grid=(t // r,),
        in_specs=[main(CB_SCB), main(CB_SCC), main(CB_SCX), halo(CB_SCC), halo(CB_SCX),
                  pl.BlockSpec((SC_CONV, GROUP), lambda i: (0, 0))],
        out_specs=pl.BlockSpec((r, GROUP), lambda i: (i, 0)),
        out_shape=jax.ShapeDtypeStruct((t, GROUP), BF16),
        compiler_params=_cparams(("parallel",)),
        name="sconv",
    )(proj, proj, proj, proj, proj, w)


SWA_ROWS = 512


def _half_rms(x, g2):
    lane = lax.broadcasted_iota(jnp.int32, x.shape, 1)
    lo = lane < SWA_D
    x2 = x * x
    ms_lo = jnp.sum(jnp.where(lo, x2, 0.0), axis=-1, keepdims=True) * (1.0 / SWA_D)
    ms_hi = jnp.sum(jnp.where(lo, 0.0, x2), axis=-1, keepdims=True) * (1.0 / SWA_D)
    rs = jnp.where(lo, lax.rsqrt(ms_lo + EPS), lax.rsqrt(ms_hi + EPS))
    return x * rs * g2


def _swa_kernel(q_ref, k_ref, v_ref, kh_ref, vh_ref, gq_ref, gk_ref, sink_ref, o_ref, *, steps):
    first = pl.program_id(0) % steps == 0
    r = q_ref.shape[0]
    w = SWA_W
    nsub = r // w
    pairs = SWA_Q_HEADS // 2
    lane = lax.broadcasted_iota(jnp.int32, (r + w, LANES), 1)

    kn = _half_rms(jnp.concatenate([kh_ref[...], k_ref[...]], axis=0), gk_ref[...])
    kroll = pltpu.roll(kn, SWA_D, axis=1)
    kdup = (jnp.where(lane < SWA_D, kn, kroll).astype(BF16),
            jnp.where(lane < SWA_D, kroll, kn).astype(BF16))
    vt = jnp.concatenate([vh_ref[...], v_ref[...]], axis=0).T.astype(BF16)

    kr = lax.broadcasted_iota(jnp.int32, (2 * w, 2 * w), 0)
    qc = lax.broadcasted_iota(jnp.int32, (2 * w, 2 * w), 1) % w
    band = jnp.logical_and(kr > qc, kr <= qc + w)
    band0 = jnp.logical_and(band, kr >= jnp.where(first, w, 0))
    qlane = lax.broadcasted_iota(jnp.int32, (w, LANES), 1)

    heads = range(pairs)
    kvh = [c // (pairs // SWA_KV_HEADS) for c in heads]
    for sub in range(nsub):
        mask = band0 if sub == 0 else band
        ks = slice(sub * w, sub * w + 2 * w)
        rows = slice(sub * w, (sub + 1) * w)
        qts = []
        for c in heads:
            qn = _half_rms(q_ref[rows, c * LANES:(c + 1) * LANES], gq_ref[...]) * (SWA_D ** -0.5 * LOG2E)
            qa = jnp.where(qlane < SWA_D, qn, 0.0)
            qb = jnp.where(qlane < SWA_D, 0.0, qn)
            qts.append(jnp.concatenate([qa.T, qb.T], axis=1).astype(BF16))
        ss = [jnp.dot(kdup[kvh[c]][ks], qts[c], preferred_element_type=F32) for c in heads]
        pns = []
        for c in heads:
            s = jnp.where(mask, ss[c], NEG)
            sink = sink_ref[c:c + 1, :] * LOG2E
            m = jnp.maximum(jnp.max(s, axis=0, keepdims=True), sink)
            p = jnp.exp2(s - m)
            l = jnp.sum(p, axis=0, keepdims=True) + jnp.exp2(sink - m)
            pns.append((p * (1.0 / l)).astype(BF16))
        ots = [jnp.dot(vt[kvh[c] * SWA_D:(kvh[c] + 1) * SWA_D, ks], pns[c], preferred_element_type=F32)
               for c in heads]
        for c in heads:
            o = jnp.concatenate([ots[c][:, :w], ots[c][:, w:]], axis=0).T
            o_ref[rows, c * LANES:(c + 1) * LANES] = o.astype(o_ref.dtype)


def _swa(proj, gq2, gk2, sinkrow, batch):
    t = proj.shape[0]
    r = min(SWA_ROWS, t // batch)
    steps = (t // batch) // r
    hb = r // SWA_W

    def halo(cb):
        return pl.BlockSpec((SWA_W, LANES), lambda i: (jnp.maximum(i * hb - 1, 0), cb))

    return pl.pallas_call(
        functools.partial(_swa_kernel, steps=steps),
        grid=(t // r,),
        in_specs=[pl.BlockSpec((r, GROUP), lambda i: (i, CB_SQ // 8)),
                  pl.BlockSpec((r, LANES), lambda i: (i, CB_SK)),
                  pl.BlockSpec((r, LANES), lambda i: (i, CB_SV)),
                  halo(CB_SK), halo(CB_SV),
                  pl.BlockSpec((1, LANES), lambda i: (0, 0)),
                  pl.BlockSpec((1, LANES), lambda i: (0, 0)),
                  pl.BlockSpec((SWA_Q_HEADS // 2, 2 * SWA_W), lambda i: (0, 0))],
        out_specs=pl.BlockSpec((r, GROUP), lambda i: (i, 0)),
        out_shape=jax.ShapeDtypeStruct((t, GROUP), BF16),
        compiler_params=_cparams(("parallel",)),
        name="swa",
    )(proj, proj, proj, proj, proj, gq2, gk2, sinkrow)


def _w_in_tail(w):
    a1 = A_COLS + 2 * N_HEADS
    pad = jnp.zeros(w.shape[:2] + (NP_COLS - w.shape[2],), BF16)
    return jnp.concatenate([w[:, :, a1:], w[:, :, A_COLS:a1], pad], axis=2)


def _cast_extend_kernel(x_ref, o_ref, *, axis, valid):
    idx = pl.program_id(axis) * x_ref.shape[axis - 1] + lax.broadcasted_iota(jnp.int32, x_ref.shape, axis - 1)
    o_ref[...] = jnp.where(idx < valid, x_ref[...], 0.0).astype(o_ref.dtype)


def _bf16_zero_extend(w, axis, size, tr=1024, tc=1024):
    nl, r, c = w.shape
    out_shape = (nl, size, c) if axis == 1 else (nl, r, size)
    spec = pl.BlockSpec((None, tr, tc), lambda l, i, j: (l, i, j))
    return pl.pallas_call(
        functools.partial(_cast_extend_kernel, axis=axis, valid=w.shape[axis]),
        grid=(nl, out_shape[1] // tr, out_shape[2] // tc),
        in_specs=[spec],
        out_specs=spec,
        out_shape=jax.ShapeDtypeStruct(out_shape, BF16),
        compiler_params=_cparams(("parallel", "parallel", "parallel")),
        name="cast_extend",
    )(w)


def _layer(x, batch, layer, norm_mix, w_in, w_in_tail, moba_q_norm, moba_k_norm, gdn_conv, gdn_a_log, gdn_dt_bias,
           gdn_out_norm, sc_conv, swa_q_norm, swa_k_norm, swa_sinks, w_out, norm_ffn, w_gate, w_up, w_down):
    row = lambda a: a.reshape(1, -1).astype(F32)

    proj = _in_proj(x, row(norm_mix), w_in, w_in_tail, layer, tm=1024, tn=512)

    kn, km, vt = _moba_prep(proj, row(moba_k_norm))
    o_a = _moba(proj, kn, km, vt, row(moba_q_norm), batch)

    hp = jnp.broadcast_to(jnp.stack([gdn_a_log, gdn_dt_bias], axis=1)[:, :, None], (N_HEADS, 2, LANES)).astype(F32)
    o_b = _gdn(proj, gdn_conv.astype(F32), hp, row(gdn_out_norm), batch)

    o_c = _sconv(proj, sc_conv.astype(F32), batch)

    sinkrow = jnp.repeat(swa_sinks.astype(F32), SWA_W).reshape(SWA_Q_HEADS // 2, 2 * SWA_W)
    o_d = _swa(proj, row(jnp.tile(swa_q_norm, 2)), row(jnp.tile(swa_k_norm, 2)), sinkrow, batch)

    x = _out_proj((o_a, o_b, o_c, o_d), w_out, x, layer, tm=1024, tn=512)

    act = _ffn_up(x, row(norm_ffn), w_gate, w_up, layer, tm=1024, tn=512)
    return _mm_res(act, w_down, x, layer, tm=2048, tn=1024, tk=1024)


def kernel(x, norm_mix, w_in, moba_q_norm, moba_k_norm, gdn_conv, gdn_a_log, gdn_dt_bias, gdn_out_norm, sc_conv, swa_q_norm, swa_k_norm, swa_sinks, w_out, norm_ffn, w_gate, w_up, w_down):
    batch, seq, d = x.shape
    w_in_c = w_in.astype(BF16)
    w_in_b = w_in_c[:, :, :A_COLS]
    w_in_t = _w_in_tail(w_in_c)
    w_out_b = w_out.astype(BF16)
    w_gate_b = _bf16_zero_extend(w_gate, 2, D_FF_PAD)
    w_up_b = _bf16_zero_extend(w_up, 2, D_FF_PAD)
    w_down_b = _bf16_zero_extend(w_down, 1, D_FF_PAD)
    h = x.reshape(batch * seq, d)
    for l in range(norm_mix.shape[0]):
        h = _layer(h, batch, l, norm_mix[l], w_in_b, w_in_t, moba_q_norm[l], moba_k_norm[l], gdn_conv[l], gdn_a_log[l],
                   gdn_dt_bias[l], gdn_out_norm[l], sc_conv[l], swa_q_norm[l], swa_k_norm[l], swa_sinks[l],
                   w_out_b, norm_ffn[l], w_gate_b, w_up_b, w_down_b)
    return h.reshape(batch, seq, d)
```

```python
import functools

import jax
import jax.numpy as jnp
from jax import lax
from jax.experimental import pallas as pl
from jax.experimental.pallas import tpu as pltpu

F32 = jnp.float32
BF16 = jnp.bfloat16

EPS = 1e-6
LANES = 128
GROUP = 1024
HEAD_DIM = 128
N_HEADS = GROUP // HEAD_DIM
MOBA_BLOCK = 256
MOBA_TILE = 2 * MOBA_BLOCK
MOBA_TOPK = 3
GDN_CONV = 4
GDN_CHUNK = 64
SC_CONV = 3
SWA_D = 64
SWA_Q_HEADS = GROUP // SWA_D
SWA_KV_HEADS = 2
SWA_W = 128
NEG = -1e30
LOG2E = 1.4426950408889634
BIAS_ROWS = 16

CB_MQ, CB_MK, CB_MV = 0, 8, 16
CB_GQ, CB_GK, CB_GV, CB_GZ = 24, 32, 40, 48
CB_SCB, CB_SCC, CB_SCX = 56, 64, 72
CB_SQ, CB_SK, CB_SV, CB_GAB = 80, 88, 89, 90
NP_COLS = 92 * LANES
A_COLS = 6 * GROUP
D_FF_PAD = 11264

VMEM_LIMIT = 56 * 1024 * 1024


def _cparams(sem, vmem=VMEM_LIMIT):
    return pltpu.CompilerParams(dimension_semantics=sem, vmem_limit_bytes=vmem)


def _bdot(a, b):
    return jnp.dot(a.astype(BF16), b.astype(BF16), preferred_element_type=F32)


def _bdot_nt(a, b):
    return lax.dot_general(a.astype(BF16), b.astype(BF16), (((1,), (1,)), ((), ())),
                           preferred_element_type=F32)


def _sigmoid(x):
    return 1.0 / (1.0 + jnp.exp(-x))


def _silu(x):
    return x * _sigmoid(x)


def _rms_rows(x, g):
    ms = jnp.mean(x * x, axis=-1, keepdims=True)
    return x * lax.rsqrt(ms + EPS) * g


NORM_CHUNK = 256


def _row_scale_to(xb_ref, rs_ref):
    chunk = min(NORM_CHUNK, xb_ref.shape[0])

    def body(c, carry):
        rows = pl.ds(pl.multiple_of(c * chunk, chunk), chunk)
        x = xb_ref[rows, :].astype(F32)
        ms = jnp.mean(x * x, axis=-1, keepdims=True)
        rs_ref[rows, :] = jnp.broadcast_to(lax.rsqrt(ms + EPS), (chunk, LANES))
        return carry

    lax.fori_loop(0, xb_ref.shape[0] // chunk, body, 0)


def _scaled_dot(xb_ref, w_ref, rs_ref):
    acc = jnp.dot(xb_ref[...], w_ref[...], preferred_element_type=F32)
    rs = rs_ref[...]
    return jnp.concatenate([acc[:, c * LANES:(c + 1) * LANES] * rs for c in range(acc.shape[1] // LANES)], axis=1)


def _in_proj_kernel(xb_ref, wa_ref, wb_ref, o_ref, rs_ref, *, na):
    j = pl.program_id(1)

    @pl.when(j == 0)
    def _():
        _row_scale_to(xb_ref, rs_ref)

    @pl.when(j < na)
    def _():
        o_ref[...] = _scaled_dot(xb_ref, wa_ref, rs_ref).astype(o_ref.dtype)

    @pl.when(j >= na)
    def _():
        o_ref[...] = _scaled_dot(xb_ref, wb_ref, rs_ref).astype(o_ref.dtype)


def _in_proj(xb, wa, wb, layer, tm, tn):
    t, d = xb.shape
    na = A_COLS // tn
    nbt = wb.shape[2] // tn
    tm = min(tm, t)
    return pl.pallas_call(
        functools.partial(_in_proj_kernel, na=na),
        grid=(t // tm, na + nbt),
        in_specs=[pl.BlockSpec((tm, d), lambda i, j: (i, 0)),
                  pl.BlockSpec((None, d, tn), lambda i, j: (layer, 0, jnp.minimum(j, na - 1))),
                  pl.BlockSpec((None, d, tn), lambda i, j: (layer, 0, jnp.maximum(j - na, 0)))],
        out_specs=pl.BlockSpec((tm, tn), lambda i, j: (i, j)),
        out_shape=jax.ShapeDtypeStruct((t, (na + nbt) * tn), F32),
        scratch_shapes=[pltpu.VMEM((tm, LANES), F32)],
        compiler_params=_cparams(("parallel", "arbitrary")),
        name="in_proj",
    )(xb, wa, wb)


def _ffn_up_kernel(xb_ref, wg_ref, wu_ref, o_ref, rs_ref):
    @pl.when(pl.program_id(1) == 0)
    def _():
        _row_scale_to(xb_ref, rs_ref)

    a = _scaled_dot(xb_ref, wg_ref, rs_ref)
    b = _scaled_dot(xb_ref, wu_ref, rs_ref)
    o_ref[...] = (_silu(a) * b).astype(o_ref.dtype)


def _ffn_up(xb, wg, wu, layer, tm, tn):
    t, d = xb.shape
    n = wg.shape[2]
    tm = min(tm, t)
    w_spec = pl.BlockSpec((None, d, tn), lambda i, j: (layer, 0, j))
    return pl.pallas_call(
        _ffn_up_kernel,
        grid=(t // tm, n // tn),
        in_specs=[pl.BlockSpec((tm, d), lambda i, j: (i, 0)), w_spec, w_spec],
        out_specs=pl.BlockSpec((tm, tn), lambda i, j: (i, j)),
        out_shape=jax.ShapeDtypeStruct((t, n), BF16),
        scratch_shapes=[pltpu.VMEM((tm, LANES), F32)],
        compiler_params=_cparams(("parallel", "arbitrary")),
        name="ffn_up",
    )(xb, wg, wu)


def _mm_res_kernel(a_ref, w_ref, r_ref, o_ref, ob_ref):
    k = pl.program_id(2)

    @pl.when(k == 0)
    def _():
        o_ref[...] = r_ref[...] + jnp.dot(a_ref[...], w_ref[...], preferred_element_type=F32)

    @pl.when(k > 0)
    def _():
        o_ref[...] += jnp.dot(a_ref[...], w_ref[...], preferred_element_type=F32)

    @pl.when(k == pl.num_programs(2) - 1)
    def _():
        ob_ref[...] = o_ref[...].astype(BF16)


def _mm_res(a, w, r, layer, tm, tn, tk):
    t, kd = a.shape
    n = w.shape[2]
    tm = min(tm, t)
    o_spec = pl.BlockSpec((tm, tn), lambda i, j, k: (i, j))
    return pl.pallas_call(
        _mm_res_kernel,
        grid=(t // tm, n // tn, kd // tk),
        in_specs=[pl.BlockSpec((tm, tk), lambda i, j, k: (i, k)),
                  pl.BlockSpec((None, tk, tn), lambda i, j, k: (layer, k, j)),
                  o_spec],
        out_specs=[o_spec, o_spec],
        out_shape=[jax.ShapeDtypeStruct((t, n), F32), jax.ShapeDtypeStruct((t, n), BF16)],
        compiler_params=_cparams(("parallel", "parallel", "arbitrary")),
        name="mm_res",
    )(a, w, r)


def _out_proj_kernel(a0_ref, a1_ref, a2_ref, a3_ref, w_ref, r_ref, o_ref, ob_ref):
    acc = r_ref[...]
    for g, a_ref in enumerate((a0_ref, a1_ref, a2_ref, a3_ref)):
        acc = acc + jnp.dot(a_ref[...], w_ref[g * GROUP:(g + 1) * GROUP, :], preferred_element_type=F32)
    o_ref[...] = acc
    ob_ref[...] = acc.astype(BF16)


def _out_proj(mix, w, r, layer, tm, tn):
    t = r.shape[0]
    n = w.shape[2]
    tm = min(tm, t)
    a_spec = pl.BlockSpec((tm, GROUP), lambda i, j: (i, 0))
    o_spec = pl.BlockSpec((tm, tn), lambda i, j: (i, j))
    return pl.pallas_call(
        _out_proj_kernel,
        grid=(t // tm, n // tn),
        in_specs=[a_spec, a_spec, a_spec, a_spec,
                  pl.BlockSpec((None, 4 * GROUP, tn), lambda i, j: (layer, 0, j)),
                  o_spec],
        out_specs=[o_spec, o_spec],
        out_shape=[jax.ShapeDtypeStruct((t, n), F32), jax.ShapeDtypeStruct((t, n), BF16)],
        compiler_params=_cparams(("parallel", "arbitrary")),
        name="out_proj",
    )(*mix, w, r)


def _moba_prep_kernel(k_ref, v_ref, g_ref, kn_ref, km_ref, vt_ref):
    g = g_ref[...]
    for h in range(N_HEADS):
        sl = slice(h * HEAD_DIM, (h + 1) * HEAD_DIM)
        kn = _rms_rows(k_ref[:, sl], g)
        kn_ref[:, sl] = kn.astype(BF16)
        for half in range(MOBA_TILE // MOBA_BLOCK):
            km_ref[half, :, sl] = jnp.mean(kn[half * MOBA_BLOCK:(half + 1) * MOBA_BLOCK], axis=0, keepdims=True)
        vt_ref[0, sl, :] = v_ref[:, sl].T.astype(BF16)


def _moba_prep(proj, gk):
    t = proj.shape[0]
    ntile = t // MOBA_TILE
    per = MOBA_TILE // MOBA_BLOCK
    return pl.pallas_call(
        _moba_prep_kernel,
        grid=(ntile,),
        in_specs=[pl.BlockSpec((MOBA_TILE, GROUP), lambda i: (i, CB_MK // 8)),
                  pl.BlockSpec((MOBA_TILE, GROUP), lambda i: (i, CB_MV // 8)),
                  pl.BlockSpec((1, HEAD_DIM), lambda i: (0, 0))],
        out_specs=[pl.BlockSpec((MOBA_TILE, GROUP), lambda i: (i, 0)),
                   pl.BlockSpec((per, 1, GROUP), lambda i: (i, 0, 0)),
                   pl.BlockSpec((1, GROUP, MOBA_TILE), lambda i: (i, 0, 0))],
        out_shape=[jax.ShapeDtypeStruct((t, GROUP), BF16),
                   jax.ShapeDtypeStruct((ntile * per, 1, GROUP), F32),
                   jax.ShapeDtypeStruct((ntile, GROUP, MOBA_TILE), BF16)],
        compiler_params=_cparams(("parallel",)),
        name="moba_prep",
    )(proj, proj, gk)


def _moba_kernel(q_ref, k_ref, vt_ref, km_ref, g_ref, o_ref, bias_ref, acc_ref, qaug_ref, s0_ref, s1_ref,
                 p0_ref, p1_ref, *, nb):
    ti = pl.program_id(2)
    blk, tile = MOBA_BLOCK, MOBA_TILE
    qn = _rms_rows(q_ref[...], g_ref[...])

    gate = lax.dot_general(km_ref[0], qn, (((1,), (1,)), ((), ())),
                           precision=lax.Precision.HIGHEST, preferred_element_type=F32)
    row = lax.broadcasted_iota(jnp.int32, gate.shape, 0)
    own = 2 * ti + (lax.broadcasted_iota(jnp.int32, gate.shape, 1) >= blk).astype(jnp.int32)
    rowf = row.astype(F32)
    gate = jnp.where(row < own, gate, -jnp.inf)
    bias = jnp.where(row == own, 0.0, NEG)
    for _ in range(MOBA_TOPK):
        m = jnp.max(gate, axis=0, keepdims=True)
        idx = jnp.min(jnp.where(gate == m, rowf, float(nb)), axis=0, keepdims=True)
        hit = jnp.logical_and(rowf == idx, m > -jnp.inf)
        bias = jnp.where(hit, 0.0, bias)
        gate = jnp.where(hit, -jnp.inf, gate)
    bias_ref[...] = bias

    qaug_ref[0:HEAD_DIM, :] = (qn * (HEAD_DIM ** -0.5 * LOG2E)).T.astype(BF16)
    qaug_ref[HEAD_DIM + BIAS_ROWS:, :] = jnp.zeros((HEAD_DIM - BIAS_ROWS, tile), BF16)
    er = lax.broadcasted_iota(jnp.int32, (tile, HEAD_DIM), 0)
    ec = lax.broadcasted_iota(jnp.int32, (tile, HEAD_DIM), 1)
    onehot = jnp.where(ec == er // blk, 1.0, 0.0).astype(BF16)
    brow = lax.broadcasted_iota(jnp.int32, (BIAS_ROWS, tile), 0)
    ones_rows = jnp.ones((BIAS_ROWS, tile), BF16)

    def scores(t, valid):
        b0 = jnp.where(valid, bias_ref[pl.ds(2 * t, 1), :], NEG)
        b1 = jnp.where(valid, bias_ref[pl.ds(2 * t + 1, 1), :], NEG)
        qaug_ref[HEAD_DIM:HEAD_DIM + BIAS_ROWS, :] = jnp.where(
            brow == 0, b0, jnp.where(brow == 1, b1, 0.0)).astype(BF16)
        kt = k_ref[pl.ds(pl.multiple_of(t * tile, tile), tile), :]
        return jnp.dot(jnp.concatenate([kt, onehot], axis=1), qaug_ref[...],
                       preferred_element_type=F32)

    s_refs, p_refs = (s0_ref, s1_ref), (p0_ref, p1_ref)
    kr = lax.broadcasted_iota(jnp.int32, (tile, tile), 0)
    qc = lax.broadcasted_iota(jnp.int32, (tile, tile), 1)
    s_refs[0][...] = jnp.where(kr <= qc, scores(ti, True), NEG)
    p_refs[1][...] = jnp.zeros((tile, tile), BF16)
    acc_ref[...] = jnp.zeros_like(acc_ref)

    def trip(n, par, carry):
        m, alpha_prev = carry
        tc = jnp.where(n == 1, ti, jnp.clip(n - 2, 0, ti))
        pv = jnp.dot(jnp.concatenate([vt_ref[tc], ones_rows], axis=0), p_refs[1 - par][...],
                     preferred_element_type=F32)
        s_refs[1 - par][...] = scores(jnp.minimum(n, ti), n < ti)
        s = s_refs[par][...]
        m_new = jnp.maximum(m, jnp.max(s, axis=0, keepdims=True))
        alpha = jnp.exp2(m - m_new)
        p_refs[par][...] = jnp.exp2(s - m_new).astype(BF16)
        acc_ref[...] = acc_ref[...] * alpha_prev + pv
        return m_new, alpha

    def body(j, carry):
        return trip(2 * j + 1, 1, trip(2 * j, 0, carry))

    init = (jnp.full((1, tile), NEG, F32), jnp.ones((1, tile), F32))
    lax.fori_loop(0, (ti + 3) // 2, body, init)
    o_ref[...] = (acc_ref[0:HEAD_DIM, :] / acc_ref[HEAD_DIM:HEAD_DIM + 1, :]).T.astype(o_ref.dtype)


def _moba(proj, kn, km, vt, gq, batch):
    t = proj.shape[0]
    s = t // batch
    nb = s // MOBA_BLOCK
    nt = s // MOBA_TILE
    km = km.reshape(batch, nb, GROUP)
    return pl.pallas_call(
        functools.partial(_moba_kernel, nb=nb),
        grid=(batch, N_HEADS, nt),
        in_specs=[pl.BlockSpec((MOBA_TILE, HEAD_DIM), lambda b, h, i: (b * nt + i, CB_MQ + h)),
                  pl.BlockSpec((s, HEAD_DIM), lambda b, h, i: (b, h)),
                  pl.BlockSpec((nt, HEAD_DIM, MOBA_TILE), lambda b, h, i: (b, h, 0)),
                  pl.BlockSpec((1, nb, HEAD_DIM), lambda b, h, i: (b, 0, h)),
                  pl.BlockSpec((1, HEAD_DIM), lambda b, h, i: (0, 0))],
        out_specs=pl.BlockSpec((MOBA_TILE, HEAD_DIM), lambda b, h, i: (b * nt + i, h)),
        out_shape=jax.ShapeDtypeStruct((t, GROUP), BF16),
        scratch_shapes=[pltpu.VMEM((nb, MOBA_TILE), F32),
                        pltpu.VMEM((HEAD_DIM + BIAS_ROWS, MOBA_TILE), F32),
                        pltpu.VMEM((2 * HEAD_DIM, MOBA_TILE), BF16),
                        pltpu.VMEM((MOBA_TILE, MOBA_TILE), F32), pltpu.VMEM((MOBA_TILE, MOBA_TILE), F32),
                        pltpu.VMEM((MOBA_TILE, MOBA_TILE), BF16), pltpu.VMEM((MOBA_TILE, MOBA_TILE), BF16)],
        compiler_params=_cparams(("parallel", "parallel", "arbitrary")),
        name="moba",
    )(proj, kn, vt, km, gq)


GDN_ROWS = 256
GDN_HPS = 8


def _conv_silu(x_ref, halo_ref, w_ref, first):
    r = x_ref.shape[0]
    halo = halo_ref[...] * jnp.where(first, 0.0, 1.0)
    xb = jnp.concatenate([halo, x_ref[...]], axis=0)
    w = w_ref[...]
    out = None
    for tap in range(GDN_CONV):
        sh = GDN_CONV - 1 - tap
        xs = xb if sh == 0 else pltpu.roll(xb, sh, axis=0)
        term = xs[8:8 + r] * w[tap:tap + 1, :]
        out = term if out is None else out + term
    return _silu(out)


def _gdn_heads(qs, ks, vs, gs, betas, states):
    r = qs[0].shape[0]
    c = GDN_CHUNK
    pair = 2 * c
    npair = r // pair
    nh = len(qs)
    row = lax.broadcasted_iota(jnp.int32, (pair, pair), 0)
    col = lax.broadcasted_iota(jnp.int32, (pair, pair), 1)
    same = (row // c) == (col // c)
    tril = jnp.logical_and(same, row >= col)
    strict = jnp.logical_and(same, row > col)
    eye = (row == col).astype(F32)
    rin = row % c
    units = [(h, pi) for pi in range(npair) for h in range(nh)]

    def rows(x, u):
        return x[u[0]][u[1] * pair:(u[1] + 1) * pair]

    gcum = {u: rows(gs, u) for u in units}
    decay = {u: jnp.exp(jnp.where(tril, gcum[u] - gcum[u].T, -jnp.inf)) for u in units}
    eg = {u: jnp.exp(gcum[u]) for u in units}
    g_end = {u: (gcum[u][c - 1:c, :], gcum[u][pair - 1:pair, :]) for u in units}
    kb = {u: rows(ks, u) * rows(betas, u) for u in units}
    vb = {u: rows(vs, u) * rows(betas, u) for u in units}
    lmat = {u: jnp.where(strict, _bdot_nt(kb[u], rows(ks, u)) * decay[u], 0.0) for u in units}
    qk = {u: _bdot_nt(rows(qs, u), rows(ks, u)) * decay[u] for u in units}
    tinv = {u: eye - lmat[u] for u in units}
    lpow = lmat
    span = 1
    while 2 * span < c:
        lpow = {u: _bdot(lpow[u], lpow[u]) for u in units}
        tinv = {u: tinv[u] + _bdot(tinv[u], lpow[u]) for u in units}
        span *= 2
    uw = {u: _bdot(tinv[u], jnp.concatenate([vb[u], kb[u] * eg[u]], axis=1)) for u in units}
    qd = {u: rows(qs, u) * eg[u] for u in units}
    kdt = {u: (rows(ks, u) * jnp.exp(jnp.where(row < c, g_end[u][0], g_end[u][1]) - gcum[u])).T for u in units}

    states = list(states)
    vns = {u: [] for u in units}
    o_st = {u: [] for u in units}
    for pi in range(npair):
        for ci in range(2):
            cs = slice(ci * c, (ci + 1) * c)
            for h in range(nh):
                u = (h, pi)
                ws = _bdot(jnp.concatenate([uw[u][cs, HEAD_DIM:], qd[u][cs]], axis=0), states[h])
                vn = uw[u][cs, :HEAD_DIM] - ws[:c]
                o_st[u].append(ws[c:])
                vns[u].append(vn)
                zero = jnp.zeros_like(vn)
                vn_pad = jnp.concatenate([vn, zero] if ci == 0 else [zero, vn], axis=0)
                states[h] = states[h] * jnp.exp(g_end[u][ci]) + _bdot(kdt[u], vn_pad)
    outs = []
    for h in range(nh):
        parts = [jnp.concatenate(o_st[(h, pi)], axis=0) + _bdot(qk[(h, pi)], jnp.concatenate(vns[(h, pi)], axis=0))
                 for pi in range(npair)]
        outs.append(jnp.concatenate(parts, axis=0))
    return outs, states


def _gdn_kernel(q_ref, k_ref, v_ref, qh_ref, kh_ref, vh_ref, wq_ref, wk_ref, wv_ref,
                gab_ref, z_ref, hp_ref, gn_ref, o_ref, s_ref):
    hg = pl.program_id(1)
    first = pl.program_id(2) == 0

    @pl.when(first)
    def _():
        s_ref[...] = jnp.zeros_like(s_ref)

    q2 = _conv_silu(q_ref, qh_ref, wq_ref, first)
    k2 = _conv_silu(k_ref, kh_ref, wk_ref, first)
    v2 = _conv_silu(v_ref, vh_ref, wv_ref, first)
    gab = gab_ref[...]
    lane = lax.broadcasted_iota(jnp.int32, gab.shape, 1)
    xg = gab + hp_ref[1:2, :]
    softplus = jnp.maximum(xg, 0.0) + jnp.log1p(jnp.exp(-jnp.abs(xg)))
    gcum_all = -jnp.exp(hp_ref[0:1, :]) * softplus
    beta_all = _sigmoid(gab)
    rin = lax.broadcasted_iota(jnp.int32, gab.shape, 0) % GDN_CHUNK
    sh = 1
    while sh < GDN_CHUNK:
        gcum_all = gcum_all + jnp.where(rin >= sh, pltpu.roll(gcum_all, sh, axis=0), 0.0)
        sh *= 2
    zeros = jnp.zeros_like(gab)

    qs, ks, vs, gs, betas = [], [], [], [], []
    for hh in range(GDN_HPS):
        h = hg * GDN_HPS + hh
        sl = slice(hh * HEAD_DIM, (hh + 1) * HEAD_DIM)
        q, k = q2[:, sl], k2[:, sl]
        qs.append(q * lax.rsqrt(jnp.sum(q * q, axis=-1, keepdims=True) + EPS) * (HEAD_DIM ** -0.5))
        ks.append(k * lax.rsqrt(jnp.sum(k * k, axis=-1, keepdims=True) + EPS))
        vs.append(v2[:, sl])
        gs.append(jnp.sum(jnp.where(lane == h, gcum_all, 0.0), axis=1, keepdims=True) + zeros)
        betas.append(jnp.sum(jnp.where(lane == N_HEADS + h, beta_all, 0.0), axis=1, keepdims=True) + zeros)

    outs, states = _gdn_heads(qs, ks, vs, gs, betas, [s_ref[hh] for hh in range(GDN_HPS)])
    for hh in range(GDN_HPS):
        sl = slice(hh * HEAD_DIM, (hh + 1) * HEAD_DIM)
        s_ref[hh] = states[hh]
        on = _rms_rows(outs[hh], gn_ref[...])
        o_ref[:, sl] = (on * _silu(z_ref[:, sl])).astype(o_ref.dtype)


def _gdn(proj, conv_w, hp, gn, batch):
    t = proj.shape[0]
    s = t // batch
    r = GDN_ROWS
    steps = s // r
    hb = r // 8
    wide = GDN_HPS * HEAD_DIM
    cpb = GDN_HPS

    def main(cb):
        return pl.BlockSpec((r, wide), lambda b, h, i: (b * steps + i, cb // cpb + h))

    def halo(cb):
        return pl.BlockSpec((8, wide), lambda b, h, i: (jnp.maximum((b * steps + i) * hb - 1, 0), cb // cpb + h))

    def wspec(off):
        return pl.BlockSpec((GDN_CONV, wide), lambda b, h, i: (0, off // cpb + h))

    return pl.pallas_call(
        _gdn_kernel,
        grid=(batch, N_HEADS // GDN_HPS, steps),
        in_specs=[main(CB_GQ), main(CB_GK), main(CB_GV), halo(CB_GQ), halo(CB_GK), halo(CB_GV),
                  wspec(0), wspec(N_HEADS), wspec(2 * N_HEADS),
                  pl.BlockSpec((r, LANES), lambda b, h, i: (b * steps + i, CB_GAB)),
                  main(CB_GZ),
                  pl.BlockSpec((2, LANES), lambda b, h, i: (0, 0)),
                  pl.BlockSpec((1, HEAD_DIM), lambda b, h, i: (0, 0))],
        out_specs=pl.BlockSpec((r, wide), lambda b, h, i: (b * steps + i, h)),
        out_shape=jax.ShapeDtypeStruct((t, GROUP), BF16),
        scratch_shapes=[pltpu.VMEM((GDN_HPS, HEAD_DIM, HEAD_DIM), F32)],
        compiler_params=_cparams(("parallel", "parallel", "arbitrary")),
        name="gdn",
    )(proj, proj, proj, proj, proj, proj, conv_w, conv_w, conv_w, proj, proj, hp, gn)


SC_ROWS = 512


def _sconv_kernel(b_ref, c_ref, x_ref, ch_ref, xh_ref, w_ref, o_ref, *, steps):
    first = pl.program_id(0) % steps == 0
    r = b_ref.shape[0]
    y = c_ref[...] * x_ref[...]
    yh = ch_ref[...] * xh_ref[...] * jnp.where(first, 0.0, 1.0)
    yb = jnp.concatenate([yh, y], axis=0)
    w = w_ref[...]
    out = None
    for tap in range(SC_CONV):
        sh = SC_CONV - 1 - tap
        ys = yb if sh == 0 else pltpu.roll(yb, sh, axis=0)
        term = ys[8:8 + r] * w[tap:tap + 1, :]
        out = term if out is None else out + term
    o_ref[...] = (b_ref[...] * out).astype(o_ref.dtype)


def _sconv(proj, w, batch):
    t = proj.shape[0]
    r = min(SC_ROWS, t // batch)
    steps = (t // batch) // r
    hb = r // 8

    def main(cb):
        return pl.BlockSpec((r, GROUP), lambda i: (i, cb // 8))

    def halo(cb):
        return pl.BlockSpec((8, GROUP), lambda i: (jnp.maximum(i * hb - 1, 0), cb // 8))

    return pl.pallas_call(
        functools.partial(_sconv_kernel, steps=steps),
        grid=(t // r,),
        in_specs=[main(CB_SCB), main(CB_SCC), main(CB_SCX), halo(CB_SCC), halo(CB_SCX),
                  pl.BlockSpec((SC_CONV, GROUP), lambda i: (0, 0))],
        out_specs=pl.BlockSpec((r, GROUP), lambda i: (i, 0)),
        out_shape=jax.ShapeDtypeStruct((t, GROUP), BF16),
        compiler_params=_cparams(("parallel",)),
        name="sconv",
    )(proj, proj, proj, proj, proj, w)


SWA_ROWS = 512


def _half_rms(x, g2):
    lane = lax.broadcasted_iota(jnp.int32, x.shape, 1)
    lo = lane < SWA_D
    x2 = x * x
    ms_lo = jnp.sum(jnp.where(lo, x2, 0.0), axis=-1, keepdims=True) * (1.0 / SWA_D)
    ms_hi = jnp.sum(jnp.where(lo, 0.0, x2), axis=-1, keepdims=True) * (1.0 / SWA_D)
    rs = jnp.where(lo, lax.rsqrt(ms_lo + EPS), lax.rsqrt(ms_hi + EPS))
    return x * rs * g2


def _swa_kernel(q_ref, k_ref, v_ref, kh_ref, vh_ref, gq_ref, gk_ref, sink_ref, o_ref, *, steps):
    first = pl.program_id(0) % steps == 0
    r = q_ref.shape[0]
    w = SWA_W
    nsub = r // w
    pairs = SWA_Q_HEADS // 2
    lane = lax.broadcasted_iota(jnp.int32, (r + w, LANES), 1)

    kn = _half_rms(jnp.concatenate([kh_ref[...], k_ref[...]], axis=0), gk_ref[...])
    kroll = pltpu.roll(kn, SWA_D, axis=1)
    kdup = (jnp.where(lane < SWA_D, kn, kroll).astype(BF16),
            jnp.where(lane < SWA_D, kroll, kn).astype(BF16))
    vt = jnp.concatenate([vh_ref[...], v_ref[...]], axis=0).T.astype(BF16)

    kr = lax.broadcasted_iota(jnp.int32, (2 * w, 2 * w), 0)
    qc = lax.broadcasted_iota(jnp.int32, (2 * w, 2 * w), 1) % w
    band = jnp.logical_and(kr > qc, kr <= qc + w)
    band0 = jnp.logical_and(band, kr >= jnp.where(first, w, 0))
    qlane = lax.broadcasted_iota(jnp.int32, (w, LANES), 1)

    heads = range(pairs)
    kvh = [c // (pairs // SWA_KV_HEADS) for c in heads]
    for sub in range(nsub):
        mask = band0 if sub == 0 else band
        ks = slice(sub * w, sub * w + 2 * w)
        rows = slice(sub * w, (sub + 1) * w)
        qts = []
        for c in heads:
            qn = _half_rms(q_ref[rows, c * LANES:(c + 1) * LANES], gq_ref[...]) * (SWA_D ** -0.5 * LOG2E)
            qa = jnp.where(qlane < SWA_D, qn, 0.0)
            qb = jnp.where(qlane < SWA_D, 0.0, qn)
            qts.append(jnp.concatenate([qa.T, qb.T], axis=1).astype(BF16))
        ss = [jnp.dot(kdup[kvh[c]][ks], qts[c], preferred_element_type=F32) for c in heads]
        pns = []
        for c in heads:
            s = jnp.where(mask, ss[c], NEG)
            sink = sink_ref[c:c + 1, :] * LOG2E
            m = jnp.maximum(jnp.max(s, axis=0, keepdims=True), sink)
            p = jnp.exp2(s - m)
            l = jnp.sum(p, axis=0, keepdims=True) + jnp.exp2(sink - m)
            pns.append((p * (1.0 / l)).astype(BF16))
        ots = [jnp.dot(vt[kvh[c] * SWA_D:(kvh[c] + 1) * SWA_D, ks], pns[c], preferred_element_type=F32)
               for c in heads]
        for c in heads:
            o = jnp.concatenate([ots[c][:, :w], ots[c][:, w:]], axis=0).T
            o_ref[rows, c * LANES:(c + 1) * LANES] = o.astype(o_ref.dtype)


def _swa(proj, gq2, gk2, sinkrow, batch):
    t = proj.shape[0]
    r = min(SWA_ROWS, t // batch)
    steps = (t // batch) // r
    hb = r // SWA_W

    def halo(cb):
        return pl.BlockSpec((SWA_W, LANES), lambda i: (jnp.maximum(i * hb - 1, 0), cb))

    return pl.pallas_call(
        functools.partial(_swa_kernel, steps=steps),
        grid=(t // r,),
        in_specs=[pl.BlockSpec((r, GROUP), lambda i: (i, CB_SQ // 8)),
                  pl.BlockSpec((r, LANES), lambda i: (i, CB_SK)),
                  pl.BlockSpec((r, LANES), lambda i: (i, CB_SV)),
                  halo(CB_SK), halo(CB_SV),
                  pl.BlockSpec((1, LANES), lambda i: (0, 0)),
                  pl.BlockSpec((1, LANES), lambda i: (0, 0)),
                  pl.BlockSpec((SWA_Q_HEADS // 2, 2 * SWA_W), lambda i: (0, 0))],
        out_specs=pl.BlockSpec((r, GROUP), lambda i: (i, 0)),
        out_shape=jax.ShapeDtypeStruct((t, GROUP), BF16),
        compiler_params=_cparams(("parallel",)),
        name="swa",
    )(proj, proj, proj, proj, proj, gq2, gk2, sinkrow)


def _w_in_tail(w):
    a1 = A_COLS + 2 * N_HEADS
    pad = jnp.zeros(w.shape[:2] + (NP_COLS - w.shape[2],), BF16)
    return jnp.concatenate([w[:, :, a1:], w[:, :, A_COLS:a1], pad], axis=2)


def _cast_extend_kernel(x_ref, g_ref, o_ref, *, axis, valid):
    idx = pl.program_id(axis) * x_ref.shape[axis - 1] + lax.broadcasted_iota(jnp.int32, x_ref.shape, axis - 1)
    o_ref[...] = jnp.where(idx < valid, x_ref[...] * g_ref[...], 0.0).astype(o_ref.dtype)


def _bf16_zero_extend(w, axis, size, row_gain=None, tr=1024, tc=1024):
    nl, r, c = w.shape
    out_shape = (nl, size, c) if axis == 1 else (nl, r, size)
    gain = jnp.ones((nl, out_shape[1], 1), F32) if row_gain is None else row_gain.astype(F32)[:, :, None]
    spec = pl.BlockSpec((None, tr, tc), lambda l, i, j: (l, i, j))
    return pl.pallas_call(
        functools.partial(_cast_extend_kernel, axis=axis, valid=w.shape[axis]),
        grid=(nl, out_shape[1] // tr, out_shape[2] // tc),
        in_specs=[spec, pl.BlockSpec((None, tr, 1), lambda l, i, j: (l, i, 0))],
        out_specs=spec,
        out_shape=jax.ShapeDtypeStruct(out_shape, BF16),
        compiler_params=_cparams(("parallel", "parallel", "parallel")),
        name="cast_extend",
    )(w, gain)


def _layer(x, xb, batch, layer, w_in, w_in_tail, moba_q_norm, moba_k_norm, gdn_conv, gdn_a_log, gdn_dt_bias,
           gdn_out_norm, sc_conv, swa_q_norm, swa_k_norm, swa_sinks, w_out, w_gate, w_up, w_down):
    row = lambda a: a.reshape(1, -1).astype(F32)

    proj = _in_proj(xb, w_in, w_in_tail, layer, tm=1024, tn=512)

    kn, km, vt = _moba_prep(proj, row(moba_k_norm))
    o_a = _moba(proj, kn, km, vt, row(moba_q_norm), batch)

    hp = jnp.pad(jnp.stack([gdn_a_log, gdn_dt_bias]).astype(F32), ((0, 0), (0, LANES - N_HEADS)))
    o_b = _gdn(proj, gdn_conv.astype(F32), hp, row(gdn_out_norm), batch)

    o_c = _sconv(proj, sc_conv.astype(F32), batch)

    sinkrow = jnp.repeat(swa_sinks.astype(F32), SWA_W).reshape(SWA_Q_HEADS // 2, 2 * SWA_W)
    o_d = _swa(proj, row(jnp.tile(swa_q_norm, 2)), row(jnp.tile(swa_k_norm, 2)), sinkrow, batch)

    x, xb = _out_proj((o_a, o_b, o_c, o_d), w_out, x, layer, tm=1024, tn=512)

    act = _ffn_up(xb, w_gate, w_up, layer, tm=1024, tn=512)
    return _mm_res(act, w_down, x, layer, tm=1024, tn=1024, tk=D_FF_PAD // 4)


def kernel(x, norm_mix, w_in, moba_q_norm, moba_k_norm, gdn_conv, gdn_a_log, gdn_dt_bias, gdn_out_norm, sc_conv, swa_q_norm, swa_k_norm, swa_sinks, w_out, norm_ffn, w_gate, w_up, w_down):
    batch, seq, d = x.shape
    w_in_c = (w_in * norm_mix[:, :, None]).astype(BF16)
    w_in_b = w_in_c[:, :, :A_COLS]
    w_in_t = _w_in_tail(w_in_c)
    w_out_b = w_out.astype(BF16)
    w_gate_b = _bf16_zero_extend(w_gate, 2, D_FF_PAD, row_gain=norm_ffn)
    w_up_b = _bf16_zero_extend(w_up, 2, D_FF_PAD, row_gain=norm_ffn)
    w_down_b = _bf16_zero_extend(w_down, 1, D_FF_PAD)
    h = x.reshape(batch * seq, d)
    hb = h.astype(BF16)
    for l in range(norm_mix.shape[0]):
        h, hb = _layer(h, hb, batch, l, w_in_b, w_in_t, moba_q_norm[l], moba_k_norm[l], gdn_conv[l], gdn_a_log[l],
                       gdn_dt_bias[l], gdn_out_norm[l], sc_conv[l], swa_q_norm[l], swa_k_norm[l], swa_sinks[l],
                       w_out_b, w_gate_b, w_up_b, w_down_b)
    return h.reshape(batch, seq, d)
```

```python
import functools

import jax
import jax.numpy as jnp
from jax import lax
from jax.experimental import pallas as pl
from jax.experimental.pallas import tpu as pltpu

F32 = jnp.float32
BF16 = jnp.bfloat16

EPS = 1e-6
LANES = 128
GROUP = 1024
HEAD_DIM = 128
N_HEADS = GROUP // HEAD_DIM
MOBA_BLOCK = 256
MOBA_TILE = 2 * MOBA_BLOCK
MOBA_TOPK = 3
GDN_CONV = 4
GDN_CHUNK = 64
SC_CONV = 3
SWA_D = 64
SWA_Q_HEADS = GROUP // SWA_D
SWA_KV_HEADS = 2
SWA_W = 128
NEG = -1e30
LOG2E = 1.4426950408889634
BIAS_ROWS = 16

CB_MQ, CB_MK, CB_MV = 0, 8, 16
CB_GQ, CB_GK, CB_GV, CB_GZ = 24, 32, 40, 48
CB_SCB, CB_SCC, CB_SCX = 56, 64, 72
CB_SQ, CB_SK, CB_SV, CB_GAB = 80, 88, 89, 90
NP_COLS = 92 * LANES
A_COLS = 6 * GROUP
D_FF_PAD = 11264

VMEM_LIMIT = 56 * 1024 * 1024


def _cparams(sem, vmem=VMEM_LIMIT):
    return pltpu.CompilerParams(dimension_semantics=sem, vmem_limit_bytes=vmem)


def _bdot(a, b):
    return jnp.dot(a.astype(BF16), b.astype(BF16), preferred_element_type=F32)


def _bdot_nt(a, b):
    return lax.dot_general(a.astype(BF16), b.astype(BF16), (((1,), (1,)), ((), ())),
                           preferred_element_type=F32)


def _sigmoid(x):
    return 1.0 / (1.0 + jnp.exp(-x))


def _silu(x):
    return x * _sigmoid(x)


def _rms_rows(x, g):
    ms = jnp.mean(x * x, axis=-1, keepdims=True)
    return x * lax.rsqrt(ms + EPS) * g


NORM_CHUNK = 256


def _row_scale_to(xb_ref, rs_ref):
    chunk = min(NORM_CHUNK, xb_ref.shape[0])

    def body(c, carry):
        rows = pl.ds(pl.multiple_of(c * chunk, chunk), chunk)
        x = xb_ref[rows, :].astype(F32)
        ms = jnp.mean(x * x, axis=-1, keepdims=True)
        rs_ref[rows, :] = jnp.broadcast_to(lax.rsqrt(ms + EPS), (chunk, LANES))
        return carry

    lax.fori_loop(0, xb_ref.shape[0] // chunk, body, 0)


def _scaled_dot(xb_ref, w_ref, rs_ref, w_transposed=False):
    dims = (((1,), (1,)), ((), ())) if w_transposed else (((1,), (0,)), ((), ()))
    acc = lax.dot_general(xb_ref[...], w_ref[...], dims, preferred_element_type=F32)
    rs = rs_ref[...]
    return jnp.concatenate([acc[:, c * LANES:(c + 1) * LANES] * rs for c in range(acc.shape[1] // LANES)], axis=1)


def _in_proj_kernel(xb_ref, wa_ref, wb_ref, o_ref, rs_ref, *, na):
    j = pl.program_id(1)

    @pl.when(j == 0)
    def _():
        _row_scale_to(xb_ref, rs_ref)

    @pl.when(j < na)
    def _():
        o_ref[...] = _scaled_dot(xb_ref, wa_ref, rs_ref, w_transposed=True).astype(o_ref.dtype)

    @pl.when(j >= na)
    def _():
        o_ref[...] = _scaled_dot(xb_ref, wb_ref, rs_ref, w_transposed=True).astype(o_ref.dtype)


def _in_proj(xb, wa, wb, layer, tm, tn):
    t, d = xb.shape
    na = A_COLS // tn
    nbt = wb.shape[1] // tn
    tm = min(tm, t)
    return pl.pallas_call(
        functools.partial(_in_proj_kernel, na=na),
        grid=(t // tm, na + nbt),
        in_specs=[pl.BlockSpec((tm, d), lambda i, j: (i, 0)),
                  pl.BlockSpec((None, tn, d), lambda i, j: (layer, jnp.minimum(j, na - 1), 0)),
                  pl.BlockSpec((None, tn, d), lambda i, j: (layer, jnp.maximum(j - na, 0), 0))],
        out_specs=pl.BlockSpec((tm, tn), lambda i, j: (i, j)),
        out_shape=jax.ShapeDtypeStruct((t, (na + nbt) * tn), F32),
        scratch_shapes=[pltpu.VMEM((tm, LANES), F32)],
        compiler_params=_cparams(("parallel", "arbitrary")),
        name="in_proj",
    )(xb, wa, wb)


def _ffn_up_kernel(xb_ref, wg_ref, wu_ref, o_ref, rs_ref):
    @pl.when(pl.program_id(1) == 0)
    def _():
        _row_scale_to(xb_ref, rs_ref)

    a = _scaled_dot(xb_ref, wg_ref, rs_ref)
    b = _scaled_dot(xb_ref, wu_ref, rs_ref)
    o_ref[...] = (_silu(a) * b).astype(o_ref.dtype)


def _ffn_up(xb, wg, wu, layer, tm, tn):
    t, d = xb.shape
    n = wg.shape[2]
    tm = min(tm, t)
    w_spec = pl.BlockSpec((None, d, tn), lambda i, j: (layer, 0, j))
    return pl.pallas_call(
        _ffn_up_kernel,
        grid=(t // tm, n // tn),
        in_specs=[pl.BlockSpec((tm, d), lambda i, j: (i, 0)), w_spec, w_spec],
        out_specs=pl.BlockSpec((tm, tn), lambda i, j: (i, j)),
        out_shape=jax.ShapeDtypeStruct((t, n), BF16),
        scratch_shapes=[pltpu.VMEM((tm, LANES), F32)],
        compiler_params=_cparams(("parallel", "arbitrary")),
        name="ffn_up",
    )(xb, wg, wu)


def _mm_res_kernel(a_ref, w_ref, r_ref, o_ref, ob_ref):
    k = pl.program_id(2)

    @pl.when(k == 0)
    def _():
        o_ref[...] = r_ref[...] + jnp.dot(a_ref[...], w_ref[...], preferred_element_type=F32)

    @pl.when(k > 0)
    def _():
        o_ref[...] += jnp.dot(a_ref[...], w_ref[...], preferred_element_type=F32)

    @pl.when(k == pl.num_programs(2) - 1)
    def _():
        ob_ref[...] = o_ref[...].astype(BF16)


def _mm_res(a, w, r, layer, tm, tn, tk):
    t, kd = a.shape
    n = w.shape[2]
    tm = min(tm, t)
    o_spec = pl.BlockSpec((tm, tn), lambda i, j, k: (i, j))
    return pl.pallas_call(
        _mm_res_kernel,
        grid=(t // tm, n // tn, kd // tk),
        in_specs=[pl.BlockSpec((tm, tk), lambda i, j, k: (i, k)),
                  pl.BlockSpec((None, tk, tn), lambda i, j, k: (layer, k, j)),
                  o_spec],
        out_specs=[o_spec, o_spec],
        out_shape=[jax.ShapeDtypeStruct((t, n), F32), jax.ShapeDtypeStruct((t, n), BF16)],
        compiler_params=_cparams(("parallel", "parallel", "arbitrary")),
        name="mm_res",
    )(a, w, r)


def _out_proj_kernel(a0_ref, a1_ref, a2_ref, a3_ref, w_ref, r_ref, o_ref, ob_ref):
    acc = r_ref[...]
    for g, a_ref in enumerate((a0_ref, a1_ref, a2_ref, a3_ref)):
        acc = acc + jnp.dot(a_ref[...], w_ref[g * GROUP:(g + 1) * GROUP, :], preferred_element_type=F32)
    o_ref[...] = acc
    ob_ref[...] = acc.astype(BF16)


def _out_proj(mix, w, r, layer, tm, tn):
    t = r.shape[0]
    n = w.shape[2]
    tm = min(tm, t)
    a_spec = pl.BlockSpec((tm, GROUP), lambda i, j: (i, 0))
    o_spec = pl.BlockSpec((tm, tn), lambda i, j: (i, j))
    return pl.pallas_call(
        _out_proj_kernel,
        grid=(t // tm, n // tn),
        in_specs=[a_spec, a_spec, a_spec, a_spec,
                  pl.BlockSpec((None, 4 * GROUP, tn), lambda i, j: (layer, 0, j)),
                  o_spec],
        out_specs=[o_spec, o_spec],
        out_shape=[jax.ShapeDtypeStruct((t, n), F32), jax.ShapeDtypeStruct((t, n), BF16)],
        compiler_params=_cparams(("parallel", "arbitrary")),
        name="out_proj",
    )(*mix, w, r)


def _moba_prep_kernel(k_ref, v_ref, g_ref, kn_ref, km_ref, vt_ref):
    g = g_ref[...]
    for h in range(N_HEADS):
        sl = slice(h * HEAD_DIM, (h + 1) * HEAD_DIM)
        kn = _rms_rows(k_ref[:, sl], g)
        kn_ref[:, sl] = kn.astype(BF16)
        for half in range(MOBA_TILE // MOBA_BLOCK):
            km_ref[half, :, sl] = jnp.mean(kn[half * MOBA_BLOCK:(half + 1) * MOBA_BLOCK], axis=0, keepdims=True)
        vt_ref[0, sl, :] = v_ref[:, sl].T.astype(BF16)


def _moba_prep(proj, gk):
    t = proj.shape[0]
    ntile = t // MOBA_TILE
    per = MOBA_TILE // MOBA_BLOCK
    return pl.pallas_call(
        _moba_prep_kernel,
        grid=(ntile,),
        in_specs=[pl.BlockSpec((MOBA_TILE, GROUP), lambda i: (i, CB_MK // 8)),
                  pl.BlockSpec((MOBA_TILE, GROUP), lambda i: (i, CB_MV // 8)),
                  pl.BlockSpec((1, HEAD_DIM), lambda i: (0, 0))],
        out_specs=[pl.BlockSpec((MOBA_TILE, GROUP), lambda i: (i, 0)),
                   pl.BlockSpec((per, 1, GROUP), lambda i: (i, 0, 0)),
                   pl.BlockSpec((1, GROUP, MOBA_TILE), lambda i: (i, 0, 0))],
        out_shape=[jax.ShapeDtypeStruct((t, GROUP), BF16),
                   jax.ShapeDtypeStruct((ntile * per, 1, GROUP), F32),
                   jax.ShapeDtypeStruct((ntile, GROUP, MOBA_TILE), BF16)],
        compiler_params=_cparams(("parallel",)),
        name="moba_prep",
    )(proj, proj, gk)


def _moba_kernel(q_ref, k_ref, vt_ref, km_ref, g_ref, o_ref, bias_ref, acc_ref, qaug_ref, s0_ref, s1_ref,
                 p0_ref, p1_ref, *, nb):
    ti = pl.program_id(2)
    blk, tile = MOBA_BLOCK, MOBA_TILE
    qn = _rms_rows(q_ref[...], g_ref[...])

    gate = lax.dot_general(km_ref[0], qn, (((1,), (1,)), ((), ())),
                           precision=lax.Precision.HIGHEST, preferred_element_type=F32)
    row = lax.broadcasted_iota(jnp.int32, gate.shape, 0)
    own = 2 * ti + (lax.broadcasted_iota(jnp.int32, gate.shape, 1) >= blk).astype(jnp.int32)
    rowf = row.astype(F32)
    gate = jnp.where(row < own, gate, -jnp.inf)
    bias = jnp.where(row == own, 0.0, NEG)
    for _ in range(MOBA_TOPK):
        m = jnp.max(gate, axis=0, keepdims=True)
        idx = jnp.min(jnp.where(gate == m, rowf, float(nb)), axis=0, keepdims=True)
        hit = jnp.logical_and(rowf == idx, m > -jnp.inf)
        bias = jnp.where(hit, 0.0, bias)
        gate = jnp.where(hit, -jnp.inf, gate)
    bias_ref[...] = bias

    qaug_ref[0:HEAD_DIM, :] = (qn * (HEAD_DIM ** -0.5 * LOG2E)).T.astype(BF16)
    qaug_ref[HEAD_DIM + BIAS_ROWS:, :] = jnp.zeros((HEAD_DIM - BIAS_ROWS, tile), BF16)
    er = lax.broadcasted_iota(jnp.int32, (tile, HEAD_DIM), 0)
    ec = lax.broadcasted_iota(jnp.int32, (tile, HEAD_DIM), 1)
    onehot = jnp.where(ec == er // blk, 1.0, 0.0).astype(BF16)
    brow = lax.broadcasted_iota(jnp.int32, (BIAS_ROWS, tile), 0)
    ones_rows = jnp.ones((BIAS_ROWS, tile), BF16)

    def scores(t, valid):
        b0 = jnp.where(valid, bias_ref[pl.ds(2 * t, 1), :], NEG)
        b1 = jnp.where(valid, bias_ref[pl.ds(2 * t + 1, 1), :], NEG)
        qaug_ref[HEAD_DIM:HEAD_DIM + BIAS_ROWS, :] = jnp.where(
            brow == 0, b0, jnp.where(brow == 1, b1, 0.0)).astype(BF16)
        kt = k_ref[pl.ds(pl.multiple_of(t * tile, tile), tile), :]
        return jnp.dot(jnp.concatenate([kt, onehot], axis=1), qaug_ref[...],
                       preferred_element_type=F32)

    s_refs, p_refs = (s0_ref, s1_ref), (p0_ref, p1_ref)
    kr = lax.broadcasted_iota(jnp.int32, (tile, tile), 0)
    qc = lax.broadcasted_iota(jnp.int32, (tile, tile), 1)
    s_refs[0][...] = jnp.where(kr <= qc, scores(ti, True), NEG)
    p_refs[1][...] = jnp.zeros((tile, tile), BF16)
    acc_ref[...] = jnp.zeros_like(acc_ref)

    def trip(n, par, carry):
        m, alpha_prev = carry
        tc = jnp.where(n == 1, ti, jnp.clip(n - 2, 0, ti))
        pv = jnp.dot(jnp.concatenate([vt_ref[tc], ones_rows], axis=0), p_refs[1 - par][...],
                     preferred_element_type=F32)
        s_refs[1 - par][...] = scores(jnp.minimum(n, ti), n < ti)
        s = s_refs[par][...]
        m_new = jnp.maximum(m, jnp.max(s, axis=0, keepdims=True))
        alpha = jnp.exp2(m - m_new)
        p_refs[par][...] = jnp.exp2(s - m_new).astype(BF16)
        acc_ref[...] = acc_ref[...] * alpha_prev + pv
        return m_new, alpha

    def body(j, carry):
        return trip(2 * j + 1, 1, trip(2 * j, 0, carry))

    init = (jnp.full((1, tile), NEG, F32), jnp.ones((1, tile), F32))
    lax.fori_loop(0, (ti + 3) // 2, body, init)
    o_ref[...] = (acc_ref[0:HEAD_DIM, :] / acc_ref[HEAD_DIM:HEAD_DIM + 1, :]).T.astype(o_ref.dtype)


def _moba(proj, kn, km, vt, gq, batch):
    t = proj.shape[0]
    s = t // batch
    nb = s // MOBA_BLOCK
    nt = s // MOBA_TILE
    km = km.reshape(batch, nb, GROUP)
    return pl.pallas_call(
        functools.partial(_moba_kernel, nb=nb),
        grid=(batch, N_HEADS, nt),
        in_specs=[pl.BlockSpec((MOBA_TILE, HEAD_DIM), lambda b, h, i: (b * nt + i, CB_MQ + h)),
                  pl.BlockSpec((s, HEAD_DIM), lambda b, h, i: (b, h)),
                  pl.BlockSpec((nt, HEAD_DIM, MOBA_TILE), lambda b, h, i: (b, h, 0)),
                  pl.BlockSpec((1, nb, HEAD_DIM), lambda b, h, i: (b, 0, h)),
                  pl.BlockSpec((1, HEAD_DIM), lambda b, h, i: (0, 0))],
        out_specs=pl.BlockSpec((MOBA_TILE, HEAD_DIM), lambda b, h, i: (b * nt + i, h)),
        out_shape=jax.ShapeDtypeStruct((t, GROUP), BF16),
        scratch_shapes=[pltpu.VMEM((nb, MOBA_TILE), F32),
                        pltpu.VMEM((HEAD_DIM + BIAS_ROWS, MOBA_TILE), F32),
                        pltpu.VMEM((2 * HEAD_DIM, MOBA_TILE), BF16),
                        pltpu.VMEM((MOBA_TILE, MOBA_TILE), F32), pltpu.VMEM((MOBA_TILE, MOBA_TILE), F32),
                        pltpu.VMEM((MOBA_TILE, MOBA_TILE), BF16), pltpu.VMEM((MOBA_TILE, MOBA_TILE), BF16)],
        compiler_params=_cparams(("parallel", "parallel", "arbitrary")),
        name="moba",
    )(proj, kn, vt, km, gq)


GDN_ROWS = 256
GDN_HPS = 8


def _conv_silu(x_ref, halo_ref, w_ref, first):
    r = x_ref.shape[0]
    halo = halo_ref[...] * jnp.where(first, 0.0, 1.0)
    xb = jnp.concatenate([halo, x_ref[...]], axis=0)
    w = w_ref[...]
    out = None
    for tap in range(GDN_CONV):
        sh = GDN_CONV - 1 - tap
        xs = xb if sh == 0 else pltpu.roll(xb, sh, axis=0)
        term = xs[8:8 + r] * w[tap:tap + 1, :]
        out = term if out is None else out + term
    return _silu(out)


def _gdn_heads(qs, ks, vs, gs, betas, states):
    r = qs[0].shape[0]
    c = GDN_CHUNK
    pair = 2 * c
    npair = r // pair
    nh = len(qs)
    row = lax.broadcasted_iota(jnp.int32, (pair, pair), 0)
    col = lax.broadcasted_iota(jnp.int32, (pair, pair), 1)
    same = (row // c) == (col // c)
    tril = jnp.logical_and(same, row >= col)
    strict = jnp.logical_and(same, row > col)
    eye = (row == col).astype(F32)
    rin = row % c
    units = [(h, pi) for pi in range(npair) for h in range(nh)]

    def rows(x, u):
        return x[u[0]][u[1] * pair:(u[1] + 1) * pair]

    gcum = {u: rows(gs, u) for u in units}
    decay = {u: jnp.exp(jnp.where(tril, gcum[u] - gcum[u].T, -jnp.inf)) for u in units}
    eg = {u: jnp.exp(gcum[u]) for u in units}
    g_end = {u: (gcum[u][c - 1:c, :], gcum[u][pair - 1:pair, :]) for u in units}
    kb = {u: rows(ks, u) * rows(betas, u) for u in units}
    vb = {u: rows(vs, u) * rows(betas, u) for u in units}
    lmat = {u: jnp.where(strict, _bdot_nt(kb[u], rows(ks, u)) * decay[u], 0.0) for u in units}
    qk = {u: _bdot_nt(rows(qs, u), rows(ks, u)) * decay[u] for u in units}
    tinv = {u: eye - lmat[u] for u in units}
    lpow = lmat
    span = 1
    while 2 * span < c:
        lpow = {u: _bdot(lpow[u], lpow[u]) for u in units}
        tinv = {u: tinv[u] + _bdot(tinv[u], lpow[u]) for u in units}
        span *= 2
    uw = {u: _bdot(tinv[u], jnp.concatenate([vb[u], kb[u] * eg[u]], axis=1)) for u in units}
    qd = {u: rows(qs, u) * eg[u] for u in units}
    kdt = {u: (rows(ks, u) * jnp.exp(jnp.where(row < c, g_end[u][0], g_end[u][1]) - gcum[u])).T for u in units}

    states = list(states)
    vns = {u: [] for u in units}
    o_st = {u: [] for u in units}
    for pi in range(npair):
        for ci in range(2):
            cs = slice(ci * c, (ci + 1) * c)
            for h in range(nh):
                u = (h, pi)
                ws = _bdot(jnp.concatenate([uw[u][cs, HEAD_DIM:], qd[u][cs]], axis=0), states[h])
                vn = uw[u][cs, :HEAD_DIM] - ws[:c]
                o_st[u].append(ws[c:])
                vns[u].append(vn)
                zero = jnp.zeros_like(vn)
                vn_pad = jnp.concatenate([vn, zero] if ci == 0 else [zero, vn], axis=0)
                states[h] = states[h] * jnp.exp(g_end[u][ci]) + _bdot(kdt[u], vn_pad)
    outs = []
    for h in range(nh):
        parts = [jnp.concatenate(o_st[(h, pi)], axis=0) + _bdot(qk[(h, pi)], jnp.concatenate(vns[(h, pi)], axis=0))
                 for pi in range(npair)]
        outs.append(jnp.concatenate(parts, axis=0))
    return outs, states


def _gdn_kernel(q_ref, k_ref, v_ref, qh_ref, kh_ref, vh_ref, wq_ref, wk_ref, wv_ref,
                gab_ref, z_ref, hp_ref, gn_ref, o_ref, s_ref):
    hg = pl.program_id(1)
    first = pl.program_id(2) == 0

    @pl.when(first)
    def _():
        s_ref[...] = jnp.zeros_like(s_ref)

    q2 = _conv_silu(q_ref, qh_ref, wq_ref, first)
    k2 = _conv_silu(k_ref, kh_ref, wk_ref, first)
    v2 = _conv_silu(v_ref, vh_ref, wv_ref, first)
    gab = gab_ref[...]
    lane = lax.broadcasted_iota(jnp.int32, gab.shape, 1)
    xg = gab + hp_ref[1:2, :]
    softplus = jnp.maximum(xg, 0.0) + jnp.log1p(jnp.exp(-jnp.abs(xg)))
    gcum_all = -jnp.exp(hp_ref[0:1, :]) * softplus
    beta_all = _sigmoid(gab)
    rin = lax.broadcasted_iota(jnp.int32, gab.shape, 0) % GDN_CHUNK
    sh = 1
    while sh < GDN_CHUNK:
        gcum_all = gcum_all + jnp.where(rin >= sh, pltpu.roll(gcum_all, sh, axis=0), 0.0)
        sh *= 2
    zeros = jnp.zeros_like(gab)

    qs, ks, vs, gs, betas = [], [], [], [], []
    for hh in range(GDN_HPS):
        h = hg * GDN_HPS + hh
        sl = slice(hh * HEAD_DIM, (hh + 1) * HEAD_DIM)
        q, k = q2[:, sl], k2[:, sl]
        qs.append(q * lax.rsqrt(jnp.sum(q * q, axis=-1, keepdims=True) + EPS) * (HEAD_DIM ** -0.5))
        ks.append(k * lax.rsqrt(jnp.sum(k * k, axis=-1, keepdims=True) + EPS))
        vs.append(v2[:, sl])
        gs.append(jnp.sum(jnp.where(lane == h, gcum_all, 0.0), axis=1, keepdims=True) + zeros)
        betas.append(jnp.sum(jnp.where(lane == N_HEADS + h, beta_all, 0.0), axis=1, keepdims=True) + zeros)

    outs, states = _gdn_heads(qs, ks, vs, gs, betas, [s_ref[hh] for hh in range(GDN_HPS)])
    for hh in range(GDN_HPS):
        sl = slice(hh * HEAD_DIM, (hh + 1) * HEAD_DIM)
        s_ref[hh] = states[hh]
        on = _rms_rows(outs[hh], gn_ref[...])
        o_ref[:, sl] = (on * _silu(z_ref[:, sl])).astype(o_ref.dtype)


def _gdn(proj, conv_w, hp, gn, batch):
    t = proj.shape[0]
    s = t // batch
    r = GDN_ROWS
    steps = s // r
    hb = r // 8
    wide = GDN_HPS * HEAD_DIM
    cpb = GDN_HPS

    def main(cb):
        return pl.BlockSpec((r, wide), lambda b, h, i: (b * steps + i, cb // cpb + h))

    def halo(cb):
        return pl.BlockSpec((8, wide), lambda b, h, i: (jnp.maximum((b * steps + i) * hb - 1, 0), cb // cpb + h))

    def wspec(off):
        return pl.BlockSpec((GDN_CONV, wide), lambda b, h, i: (0, off // cpb + h))

    return pl.pallas_call(
        _gdn_kernel,
        grid=(batch, N_HEADS // GDN_HPS, steps),
        in_specs=[main(CB_GQ), main(CB_GK), main(CB_GV), halo(CB_GQ), halo(CB_GK), halo(CB_GV),
                  wspec(0), wspec(N_HEADS), wspec(2 * N_HEADS),
                  pl.BlockSpec((r, LANES), lambda b, h, i: (b * steps + i, CB_GAB)),
                  main(CB_GZ),
                  pl.BlockSpec((2, LANES), lambda b, h, i: (0, 0)),
                  pl.BlockSpec((1, HEAD_DIM), lambda b, h, i: (0, 0))],
        out_specs=pl.BlockSpec((r, wide), lambda b, h, i: (b * steps + i, h)),
        out_shape=jax.ShapeDtypeStruct((t, GROUP), BF16),
        scratch_shapes=[pltpu.VMEM((GDN_HPS, HEAD_DIM, HEAD_DIM), F32)],
        compiler_params=_cparams(("parallel", "parallel", "arbitrary")),
        name="gdn",
    )(proj, proj, proj, proj, proj, proj, conv_w, conv_w, conv_w, proj, proj, hp, gn)


SC_ROWS = 512


def _sconv_kernel(b_ref, c_ref, x_ref, ch_ref, xh_ref, w_ref, o_ref, *, steps):
    first = pl.program_id(0) % steps == 0
    r = b_ref.shape[0]
    y = c_ref[...] * x_ref[...]
    yh = ch_ref[...] * xh_ref[...] * jnp.where(first, 0.0, 1.0)
    yb = jnp.concatenate([yh, y], axis=0)
    w = w_ref[...]
    out = None
    for tap in range(SC_CONV):
        sh = SC_CONV - 1 - tap
        ys = yb if sh == 0 else pltpu.roll(yb, sh, axis=0)
        term = ys[8:8 + r] * w[tap:tap + 1, :]
        out = term if out is None else out + term
    o_ref[...] = (b_ref[...] * out).astype(o_ref.dtype)


def _sconv(proj, w, batch):
    t = proj.shape[0]
    r = min(SC_ROWS, t // batch)
    steps = (t // batch) // r
    hb = r // 8

    def main(cb):
        return pl.BlockSpec((r, GROUP), lambda i: (i, cb // 8))

    def halo(cb):
        return pl.BlockSpec((8, GROUP), lambda i: (jnp.maximum(i * hb - 1, 0), cb // 8))

    return pl.pallas_call(
        functools.partial(_sconv_kernel, steps=steps),
        grid=(t // r,),
        in_specs=[main(CB_SCB), main(CB_SCC), main(CB_SCX), halo(CB_SCC), halo(CB_SCX),
                  pl.BlockSpec((SC_CONV, GROUP), lambda i: (0, 0))],
        out_specs=pl.BlockSpec((r, GROUP), lambda i: (i, 0)),
        out_shape=jax.ShapeDtypeStruct((t, GROUP), BF16),
        compiler_params=_cparams(("parallel",)),
        name="sconv",
    )(proj, proj, proj, proj, proj, w)


SWA_ROWS = 512


def _half_rms(x, g2):
    lane = lax.broadcasted_iota(jnp.int32, x.shape, 1)
    lo = lane < SWA_D
    x2 = x * x
    ms_lo = jnp.sum(jnp.where(lo, x2, 0.0), axis=-1, keepdims=True) * (1.0 / SWA_D)
    ms_hi = jnp.sum(jnp.where(lo, 0.0, x2), axis=-1, keepdims=True) * (1.0 / SWA_D)
    rs = jnp.where(lo, lax.rsqrt(ms_lo + EPS), lax.rsqrt(ms_hi + EPS))
    return x * rs * g2


def _swa_kernel(q_ref, k_ref, v_ref, kh_ref, vh_ref, gq_ref, gk_ref, sink_ref, o_ref, *, steps):
    first = pl.program_id(0) % steps == 0
    r = q_ref.shape[0]
    w = SWA_W
    nsub = r // w
    pairs = SWA_Q_HEADS // 2
    lane = lax.broadcasted_iota(jnp.int32, (r + w, LANES), 1)

    kn = _half_rms(jnp.concatenate([kh_ref[...], k_ref[...]], axis=0), gk_ref[...])
    kroll = pltpu.roll(kn, SWA_D, axis=1)
    kdup = (jnp.where(lane < SWA_D, kn, kroll).astype(BF16),
            jnp.where(lane < SWA_D, kroll, kn).astype(BF16))
    vt = jnp.concatenate([vh_ref[...], v_ref[...]], axis=0).T.astype(BF16)

    kr = lax.broadcasted_iota(jnp.int32, (2 * w, 2 * w), 0)
    qc = lax.broadcasted_iota(jnp.int32, (2 * w, 2 * w), 1) % w
    band = jnp.logical_and(kr > qc, kr <= qc + w)
    band0 = jnp.logical_and(band, kr >= jnp.where(first, w, 0))
    qlane = lax.broadcasted_iota(jnp.int32, (w, LANES), 1)

    heads = range(pairs)
    kvh = [c // (pairs // SWA_KV_HEADS) for c in heads]
    for sub in range(nsub):
        mask = band0 if sub == 0 else band
        ks = slice(sub * w, sub * w + 2 * w)
        rows = slice(sub * w, (sub + 1) * w)
        qts = []
        for c in heads:
            qn = _half_rms(q_ref[rows, c * LANES:(c + 1) * LANES], gq_ref[...]) * (SWA_D ** -0.5 * LOG2E)
            qa = jnp.where(qlane < SWA_D, qn, 0.0)
            qb = jnp.where(qlane < SWA_D, 0.0, qn)
            qts.append(jnp.concatenate([qa.T, qb.T], axis=1).astype(BF16))
        ss = [jnp.dot(kdup[kvh[c]][ks], qts[c], preferred_element_type=F32) for c in heads]
        pns = []
        for c in heads:
            s = jnp.where(mask, ss[c], NEG)
            sink = sink_ref[c:c + 1, :] * LOG2E
            m = jnp.maximum(jnp.max(s, axis=0, keepdims=True), sink)
            p = jnp.exp2(s - m)
            l = jnp.sum(p, axis=0, keepdims=True) + jnp.exp2(sink - m)
            pns.append((p * (1.0 / l)).astype(BF16))
        ots = [jnp.dot(vt[kvh[c] * SWA_D:(kvh[c] + 1) * SWA_D, ks], pns[c], preferred_element_type=F32)
               for c in heads]
        for c in heads:
            o = jnp.concatenate([ots[c][:, :w], ots[c][:, w:]], axis=0).T
            o_ref[rows, c * LANES:(c + 1) * LANES] = o.astype(o_ref.dtype)


def _swa(proj, gq2, gk2, sinkrow, batch):
    t = proj.shape[0]
    r = min(SWA_ROWS, t // batch)
    steps = (t // batch) // r
    hb = r // SWA_W

    def halo(cb):
        return pl.BlockSpec((SWA_W, LANES), lambda i: (jnp.maximum(i * hb - 1, 0), cb))

    return pl.pallas_call(
        functools.partial(_swa_kernel, steps=steps),
        grid=(t // r,),
        in_specs=[pl.BlockSpec((r, GROUP), lambda i: (i, CB_SQ // 8)),
                  pl.BlockSpec((r, LANES), lambda i: (i, CB_SK)),
                  pl.BlockSpec((r, LANES), lambda i: (i, CB_SV)),
                  halo(CB_SK), halo(CB_SV),
                  pl.BlockSpec((1, LANES), lambda i: (0, 0)),
                  pl.BlockSpec((1, LANES), lambda i: (0, 0)),
                  pl.BlockSpec((SWA_Q_HEADS // 2, 2 * SWA_W), lambda i: (0, 0))],
        out_specs=pl.BlockSpec((r, GROUP), lambda i: (i, 0)),
        out_shape=jax.ShapeDtypeStruct((t, GROUP), BF16),
        compiler_params=_cparams(("parallel",)),
        name="swa",
    )(proj, proj, proj, proj, proj, gq2, gk2, sinkrow)


def _w_in_tail(wt):
    a1 = A_COLS + 2 * N_HEADS
    pad = jnp.zeros((wt.shape[0], NP_COLS - wt.shape[1], wt.shape[2]), BF16)
    return jnp.concatenate([wt[:, a1:, :], wt[:, A_COLS:a1, :], pad], axis=1)


def _cast_extend_kernel(x_ref, g_ref, o_ref, *, axis, valid):
    idx = pl.program_id(axis) * x_ref.shape[axis - 1] + lax.broadcasted_iota(jnp.int32, x_ref.shape, axis - 1)
    o_ref[...] = jnp.where(idx < valid, x_ref[...] * g_ref[...], 0.0).astype(o_ref.dtype)


def _bf16_zero_extend(w, axis, size, row_gain=None, tr=1024, tc=1024):
    nl, r, c = w.shape
    out_shape = (nl, size, c) if axis == 1 else (nl, r, size)
    gain = jnp.ones((nl, out_shape[1], 1), F32) if row_gain is None else row_gain.astype(F32)[:, :, None]
    spec = pl.BlockSpec((None, tr, tc), lambda l, i, j: (l, i, j))
    return pl.pallas_call(
        functools.partial(_cast_extend_kernel, axis=axis, valid=w.shape[axis]),
        grid=(nl, out_shape[1] // tr, out_shape[2] // tc),
        in_specs=[spec, pl.BlockSpec((None, tr, 1), lambda l, i, j: (l, i, 0))],
        out_specs=spec,
        out_shape=jax.ShapeDtypeStruct(out_shape, BF16),
        compiler_params=_cparams(("parallel", "parallel", "parallel")),
        name="cast_extend",
    )(w, gain)


def _layer(x, xb, batch, layer, w_in, w_in_tail, moba_q_norm, moba_k_norm, gdn_conv, gdn_a_log, gdn_dt_bias,
           gdn_out_norm, sc_conv, swa_q_norm, swa_k_norm, swa_sinks, w_out, w_gate, w_up, w_down):
    row = lambda a: a.reshape(1, -1).astype(F32)

    proj = _in_proj(xb, w_in, w_in_tail, layer, tm=1024, tn=512)

    kn, km, vt = _moba_prep(proj, row(moba_k_norm))
    o_a = _moba(proj, kn, km, vt, row(moba_q_norm), batch)

    hp = jnp.pad(jnp.stack([gdn_a_log, gdn_dt_bias]).astype(F32), ((0, 0), (0, LANES - N_HEADS)))
    o_b = _gdn(proj, gdn_conv.astype(F32), hp, row(gdn_out_norm), batch)

    o_c = _sconv(proj, sc_conv.astype(F32), batch)

    sinkrow = jnp.repeat(swa_sinks.astype(F32), SWA_W).reshape(SWA_Q_HEADS // 2, 2 * SWA_W)
    o_d = _swa(proj, row(jnp.tile(swa_q_norm, 2)), row(jnp.tile(swa_k_norm, 2)), sinkrow, batch)

    x, xb = _out_proj((o_a, o_b, o_c, o_d), w_out, x, layer, tm=1024, tn=512)

    act = _ffn_up(xb, w_gate, w_up, layer, tm=1024, tn=512)
    return _mm_res(act, w_down, x, layer, tm=1024, tn=1024, tk=D_FF_PAD // 4)


def kernel(x, norm_mix, w_in, moba_q_norm, moba_k_norm, gdn_conv, gdn_a_log, gdn_dt_bias, gdn_out_norm, sc_conv, swa_q_norm, swa_k_norm, swa_sinks, w_out, norm_ffn, w_gate, w_up, w_down):
    batch, seq, d = x.shape
    w_in_b = (jnp.swapaxes(w_in, 1, 2) * norm_mix[:, None, :]).astype(BF16)
    w_in_t = _w_in_tail(w_in_b)
    w_out_b = w_out.astype(BF16)
    w_gate_b = _bf16_zero_extend(w_gate, 2, D_FF_PAD, row_gain=norm_ffn)
    w_up_b = _bf16_zero_extend(w_up, 2, D_FF_PAD, row_gain=norm_ffn)
    w_down_b = _bf16_zero_extend(w_down, 1, D_FF_PAD)
    h = x.reshape(batch * seq, d)
    hb = h.astype(BF16)
    for l in range(norm_mix.shape[0]):
        h, hb = _layer(h, hb, batch, l, w_in_b, w_in_t, moba_q_norm[l], moba_k_norm[l], gdn_conv[l], gdn_a_log[l],
                       gdn_dt_bias[l], gdn_out_norm[l], sc_conv[l], swa_q_norm[l], swa_k_norm[l], swa_sinks[l],
                       w_out_b, w_gate_b, w_up_b, w_down_b)
    return h.reshape(batch, seq, d)
```

```python
import functools

import jax
import jax.numpy as jnp
from jax import lax
from jax.experimental import pallas as pl
from jax.experimental.pallas import tpu as pltpu

F32 = jnp.float32
BF16 = jnp.bfloat16

EPS = 1e-6
LANES = 128
GROUP = 1024
HEAD_DIM = 128
N_HEADS = GROUP // HEAD_DIM
MOBA_BLOCK = 256
MOBA_TILE = 2 * MOBA_BLOCK
MOBA_TOPK = 3
GDN_CONV = 4
GDN_CHUNK = 64
SC_CONV = 3
SWA_D = 64
SWA_Q_HEADS = GROUP // SWA_D
SWA_KV_HEADS = 2
SWA_W = 128
NEG = -1e30
LOG2E = 1.4426950408889634
BIAS_ROWS = 16

CB_MQ, CB_MK, CB_MV = 0, 8, 16
CB_GQ, CB_GK, CB_GV, CB_GZ = 24, 32, 40, 48
CB_SCB, CB_SCC, CB_SCX = 56, 64, 72
CB_SQ, CB_SK, CB_SV, CB_GAB = 80, 88, 89, 90
NP_COLS = 92 * LANES
A_COLS = 6 * GROUP
D_FF_PAD = 11264

VMEM_LIMIT = 56 * 1024 * 1024


def _cparams(sem, vmem=VMEM_LIMIT):
    return pltpu.CompilerParams(dimension_semantics=sem, vmem_limit_bytes=vmem)


def _bdot(a, b):
    return jnp.dot(a.astype(BF16), b.astype(BF16), preferred_element_type=F32)


def _bdot_nt(a, b):
    return lax.dot_general(a.astype(BF16), b.astype(BF16), (((1,), (1,)), ((), ())),
                           preferred_element_type=F32)


def _sigmoid(x):
    return 1.0 / (1.0 + jnp.exp(-x))


def _silu(x):
    return x * _sigmoid(x)


def _rms_rows(x, g):
    ms = jnp.mean(x * x, axis=-1, keepdims=True)
    return x * lax.rsqrt(ms + EPS) * g


NORM_CHUNK = 256


def _row_scale_to(xb_ref, rs_ref):
    chunk = min(NORM_CHUNK, xb_ref.shape[0])

    def body(c, carry):
        rows = pl.ds(pl.multiple_of(c * chunk, chunk), chunk)
        x = xb_ref[rows, :].astype(F32)
        ms = jnp.mean(x * x, axis=-1, keepdims=True)
        rs_ref[rows, :] = jnp.broadcast_to(lax.rsqrt(ms + EPS), (chunk, LANES))
        return carry

    lax.fori_loop(0, xb_ref.shape[0] // chunk, body, 0)


def _scaled_dot(xb_ref, w_ref, rs_ref, w_transposed=False):
    dims = (((1,), (1,)), ((), ())) if w_transposed else (((1,), (0,)), ((), ()))
    acc = lax.dot_general(xb_ref[...], w_ref[...], dims, preferred_element_type=F32)
    rs = rs_ref[...]
    return jnp.concatenate([acc[:, c * LANES:(c + 1) * LANES] * rs for c in range(acc.shape[1] // LANES)], axis=1)


def _in_proj_kernel(xb_ref, wa_ref, wb_ref, o_ref, rs_ref, *, na):
    j = pl.program_id(1)

    @pl.when(j == 0)
    def _():
        _row_scale_to(xb_ref, rs_ref)

    @pl.when(j < na)
    def _():
        o_ref[...] = _scaled_dot(xb_ref, wa_ref, rs_ref, w_transposed=True).astype(o_ref.dtype)

    @pl.when(j >= na)
    def _():
        o_ref[...] = _scaled_dot(xb_ref, wb_ref, rs_ref, w_transposed=True).astype(o_ref.dtype)


def _in_proj(xb, wa, wb, layer, tm, tn):
    t, d = xb.shape
    na = A_COLS // tn
    nbt = wb.shape[1] // tn
    tm = min(tm, t)
    return pl.pallas_call(
        functools.partial(_in_proj_kernel, na=na),
        grid=(t // tm, na + nbt),
        in_specs=[pl.BlockSpec((tm, d), lambda i, j: (i, 0)),
                  pl.BlockSpec((None, tn, d), lambda i, j: (layer, jnp.minimum(j, na - 1), 0)),
                  pl.BlockSpec((None, tn, d), lambda i, j: (layer, jnp.maximum(j - na, 0), 0))],
        out_specs=pl.BlockSpec((tm, tn), lambda i, j: (i, j)),
        out_shape=jax.ShapeDtypeStruct((t, (na + nbt) * tn), F32),
        scratch_shapes=[pltpu.VMEM((tm, LANES), F32)],
        compiler_params=_cparams(("parallel", "arbitrary")),
        name="in_proj",
    )(xb, wa, wb)


FIRST_ROWS = 1024


def _ffn_up_kernel(xb_ref, wg_ref, wu_ref, o_ref, rs_ref):
    @pl.when(pl.program_id(1) == 0)
    def _():
        _row_scale_to(xb_ref, rs_ref)

    a = _scaled_dot(xb_ref, wg_ref, rs_ref)
    b = _scaled_dot(xb_ref, wu_ref, rs_ref)
    o_ref[...] = (_silu(a) * b).astype(o_ref.dtype)


def _ffn_up_first_kernel(xb_ref, g_ref, wg32_ref, wu32_ref, o_ref, wgb_ref, wub_ref, rs_ref, *, valid):
    j = pl.program_id(0)

    @pl.when(j == 0)
    def _():
        _row_scale_to(xb_ref, rs_ref)

    col = j * wg32_ref.shape[1] + lax.broadcasted_iota(jnp.int32, wg32_ref.shape, 1)
    wgb_ref[...] = jnp.where(col < valid, wg32_ref[...] * g_ref[...], 0.0).astype(BF16)
    wub_ref[...] = jnp.where(col < valid, wu32_ref[...] * g_ref[...], 0.0).astype(BF16)
    a = _scaled_dot(xb_ref, wgb_ref, rs_ref)
    b = _scaled_dot(xb_ref, wub_ref, rs_ref)
    o_ref[...] = (_silu(a) * b).astype(o_ref.dtype)


def _ffn_up_rest_kernel(xb_ref, wg_ref, wu_ref, dst_ref, o_ref, rs_ref):
    del dst_ref
    _ffn_up_kernel(xb_ref, wg_ref, wu_ref, o_ref, rs_ref)


def _ffn_up(xb, gain, w_gate, w_up, layer, n_pad, tm, tn, tn_first):
    t, d = xb.shape
    n_real = w_gate.shape[2]
    tf = min(FIRST_ROWS, t)
    last_blk = (n_real - 1) // tn_first
    w32_spec = pl.BlockSpec((None, d, tn_first), lambda j: (layer, 0, jnp.minimum(j, last_blk)))
    wb_spec = pl.BlockSpec((d, tn_first), lambda j: (0, j))
    act, wgb, wub = pl.pallas_call(
        functools.partial(_ffn_up_first_kernel, valid=n_real),
        grid=(n_pad // tn_first,),
        in_specs=[pl.BlockSpec((tf, d), lambda j: (0, 0)),
                  pl.BlockSpec((None, d, 1), lambda j: (layer, 0, 0)),
                  w32_spec, w32_spec],
        out_specs=[pl.BlockSpec((tf, tn_first), lambda j: (0, j)), wb_spec, wb_spec],
        out_shape=[jax.ShapeDtypeStruct((t, n_pad), BF16),
                   jax.ShapeDtypeStruct((d, n_pad), BF16), jax.ShapeDtypeStruct((d, n_pad), BF16)],
        scratch_shapes=[pltpu.VMEM((tf, LANES), F32)],
        compiler_params=_cparams(("arbitrary",)),
        name="ffn_up_first",
    )(xb, gain, w_gate, w_up)
    if t == tf:
        return act
    off = tf // tm
    w_spec = pl.BlockSpec((d, tn), lambda i, j: (0, j))
    return pl.pallas_call(
        _ffn_up_rest_kernel,
        grid=(t // tm - off, n_pad // tn),
        in_specs=[pl.BlockSpec((tm, d), lambda i, j: (i + off, 0)), w_spec, w_spec,
                  pl.BlockSpec(memory_space=pl.ANY)],
        out_specs=pl.BlockSpec((tm, tn), lambda i, j: (i + off, j)),
        out_shape=jax.ShapeDtypeStruct((t, n_pad), BF16),
        scratch_shapes=[pltpu.VMEM((tm, LANES), F32)],
        input_output_aliases={3: 0},
        compiler_params=_cparams(("parallel", "arbitrary")),
        name="ffn_up",
    )(xb, wgb, wub, act)


def _mm_res_kernel(a_ref, w_ref, r_ref, o_ref, ob_ref):
    k = pl.program_id(2)

    @pl.when(k == 0)
    def _():
        o_ref[...] = r_ref[...] + jnp.dot(a_ref[...], w_ref[...], preferred_element_type=F32)

    @pl.when(k > 0)
    def _():
        o_ref[...] += jnp.dot(a_ref[...], w_ref[...], preferred_element_type=F32)

    @pl.when(k == pl.num_programs(2) - 1)
    def _():
        ob_ref[...] = o_ref[...].astype(BF16)


def _mm_res_first_kernel(a_ref, w32_ref, r_ref, o_ref, ob_ref, wb_ref, *, valid):
    k = pl.program_id(1)
    row = k * w32_ref.shape[0] + lax.broadcasted_iota(jnp.int32, w32_ref.shape, 0)
    wb_ref[...] = jnp.where(row < valid, w32_ref[...], 0.0).astype(BF16)

    @pl.when(k == 0)
    def _():
        o_ref[...] = r_ref[...] + jnp.dot(a_ref[...], wb_ref[...], preferred_element_type=F32)

    @pl.when(k > 0)
    def _():
        o_ref[...] += jnp.dot(a_ref[...], wb_ref[...], preferred_element_type=F32)

    @pl.when(k == pl.num_programs(1) - 1)
    def _():
        ob_ref[...] = o_ref[...].astype(BF16)


def _mm_res_rest_kernel(a_ref, w_ref, r_ref, dst_ref, dstb_ref, o_ref, ob_ref):
    del dst_ref, dstb_ref
    _mm_res_kernel(a_ref, w_ref, r_ref, o_ref, ob_ref)


def _mm_res(a, w, r, layer, tm, tn, tk, tn_first):
    t, kd = a.shape
    k_real, n = w.shape[1], w.shape[2]
    tf = min(FIRST_ROWS, t)
    last_blk = (k_real - 1) // tk
    of_spec = pl.BlockSpec((tf, tn_first), lambda j, k: (0, j))
    x, xb, wb = pl.pallas_call(
        functools.partial(_mm_res_first_kernel, valid=k_real),
        grid=(n // tn_first, kd // tk),
        in_specs=[pl.BlockSpec((tf, tk), lambda j, k: (0, k)),
                  pl.BlockSpec((None, tk, tn_first), lambda j, k: (layer, jnp.minimum(k, last_blk), j)),
                  of_spec],
        out_specs=[of_spec, of_spec, pl.BlockSpec((tk, tn_first), lambda j, k: (k, j))],
        out_shape=[jax.ShapeDtypeStruct((t, n), F32), jax.ShapeDtypeStruct((t, n), BF16),
                   jax.ShapeDtypeStruct((kd, n), BF16)],
        compiler_params=_cparams(("parallel", "arbitrary")),
        name="mm_res_first",
    )(a, w, r)
    if t == tf:
        return x, xb
    off = tf // tm
    o_spec = pl.BlockSpec((tm, tn), lambda i, j, k: (i + off, j))
    any_spec = pl.BlockSpec(memory_space=pl.ANY)
    return pl.pallas_call(
        _mm_res_rest_kernel,
        grid=(t // tm - off, n // tn, kd // tk),
        in_specs=[pl.BlockSpec((tm, tk), lambda i, j, k: (i + off, k)),
                  pl.BlockSpec((tk, tn), lambda i, j, k: (k, j)),
                  o_spec, any_spec, any_spec],
        out_specs=[o_spec, o_spec],
        out_shape=[jax.ShapeDtypeStruct((t, n), F32), jax.ShapeDtypeStruct((t, n), BF16)],
        input_output_aliases={3: 0, 4: 1},
        compiler_params=_cparams(("parallel", "parallel", "arbitrary")),
        name="mm_res",
    )(a, wb, r, x, xb)


def _out_proj_kernel(a0_ref, a1_ref, a2_ref, a3_ref, w_ref, r_ref, o_ref, ob_ref):
    acc = r_ref[...]
    for g, a_ref in enumerate((a0_ref, a1_ref, a2_ref, a3_ref)):
        acc = acc + jnp.dot(a_ref[...], w_ref[g * GROUP:(g + 1) * GROUP, :], preferred_element_type=F32)
    o_ref[...] = acc
    ob_ref[...] = acc.astype(BF16)


def _out_proj(mix, w, r, layer, tm, tn):
    t = r.shape[0]
    n = w.shape[2]
    tm = min(tm, t)
    a_spec = pl.BlockSpec((tm, GROUP), lambda i, j: (i, 0))
    o_spec = pl.BlockSpec((tm, tn), lambda i, j: (i, j))
    return pl.pallas_call(
        _out_proj_kernel,
        grid=(t // tm, n // tn),
        in_specs=[a_spec, a_spec, a_spec, a_spec,
                  pl.BlockSpec((None, 4 * GROUP, tn), lambda i, j: (layer, 0, j)),
                  o_spec],
        out_specs=[o_spec, o_spec],
        out_shape=[jax.ShapeDtypeStruct((t, n), F32), jax.ShapeDtypeStruct((t, n), BF16)],
        compiler_params=_cparams(("parallel", "arbitrary")),
        name="out_proj",
    )(*mix, w, r)


def _moba_prep_kernel(k_ref, v_ref, g_ref, kn_ref, km_ref, vt_ref):
    g = g_ref[...]
    for h in range(N_HEADS):
        sl = slice(h * HEAD_DIM, (h + 1) * HEAD_DIM)
        kn = _rms_rows(k_ref[:, sl], g)
        kn_ref[:, sl] = kn.astype(BF16)
        for half in range(MOBA_TILE // MOBA_BLOCK):
            km_ref[half, :, sl] = jnp.mean(kn[half * MOBA_BLOCK:(half + 1) * MOBA_BLOCK], axis=0, keepdims=True)
        vt_ref[0, sl, :] = v_ref[:, sl].T.astype(BF16)


def _moba_prep(proj, gk):
    t = proj.shape[0]
    ntile = t // MOBA_TILE
    per = MOBA_TILE // MOBA_BLOCK
    return pl.pallas_call(
        _moba_prep_kernel,
        grid=(ntile,),
        in_specs=[pl.BlockSpec((MOBA_TILE, GROUP), lambda i: (i, CB_MK // 8)),
                  pl.BlockSpec((MOBA_TILE, GROUP), lambda i: (i, CB_MV // 8)),
                  pl.BlockSpec((1, HEAD_DIM), lambda i: (0, 0))],
        out_specs=[pl.BlockSpec((MOBA_TILE, GROUP), lambda i: (i, 0)),
                   pl.BlockSpec((per, 1, GROUP), lambda i: (i, 0, 0)),
                   pl.BlockSpec((1, GROUP, MOBA_TILE), lambda i: (i, 0, 0))],
        out_shape=[jax.ShapeDtypeStruct((t, GROUP), BF16),
                   jax.ShapeDtypeStruct((ntile * per, 1, GROUP), F32),
                   jax.ShapeDtypeStruct((ntile, GROUP, MOBA_TILE), BF16)],
        compiler_params=_cparams(("parallel",)),
        name="moba_prep",
    )(proj, proj, gk)


def _moba_kernel(q_ref, k_ref, vt_ref, km_ref, g_ref, o_ref, bias_ref, acc_ref, qaug_ref, s0_ref, s1_ref,
                 p0_ref, p1_ref, *, nb):
    ti = pl.program_id(2)
    blk, tile = MOBA_BLOCK, MOBA_TILE
    qn = _rms_rows(q_ref[...], g_ref[...])

    gate = lax.dot_general(km_ref[0], qn, (((1,), (1,)), ((), ())),
                           precision=lax.Precision.HIGHEST, preferred_element_type=F32)
    row = lax.broadcasted_iota(jnp.int32, gate.shape, 0)
    own = 2 * ti + (lax.broadcasted_iota(jnp.int32, gate.shape, 1) >= blk).astype(jnp.int32)
    rowf = row.astype(F32)
    gate = jnp.where(row < own, gate, -jnp.inf)
    bias = jnp.where(row == own, 0.0, NEG)
    for _ in range(MOBA_TOPK):
        m = jnp.max(gate, axis=0, keepdims=True)
        idx = jnp.min(jnp.where(gate == m, rowf, float(nb)), axis=0, keepdims=True)
        hit = jnp.logical_and(rowf == idx, m > -jnp.inf)
        bias = jnp.where(hit, 0.0, bias)
        gate = jnp.where(hit, -jnp.inf, gate)
    bias_ref[...] = bias

    qaug_ref[0:HEAD_DIM, :] = (qn * (HEAD_DIM ** -0.5 * LOG2E)).T.astype(BF16)
    qaug_ref[HEAD_DIM + BIAS_ROWS:, :] = jnp.zeros((HEAD_DIM - BIAS_ROWS, tile), BF16)
    er = lax.broadcasted_iota(jnp.int32, (tile, HEAD_DIM), 0)
    ec = lax.broadcasted_iota(jnp.int32, (tile, HEAD_DIM), 1)
    onehot = jnp.where(ec == er // blk, 1.0, 0.0).astype(BF16)
    brow = lax.broadcasted_iota(jnp.int32, (BIAS_ROWS, tile), 0)
    ones_rows = jnp.ones((BIAS_ROWS, tile), BF16)

    def scores(t, valid):
        b0 = jnp.where(valid, bias_ref[pl.ds(2 * t, 1), :], NEG)
        b1 = jnp.where(valid, bias_ref[pl.ds(2 * t + 1, 1), :], NEG)
        qaug_ref[HEAD_DIM:HEAD_DIM + BIAS_ROWS, :] = jnp.where(
            brow == 0, b0, jnp.where(brow == 1, b1, 0.0)).astype(BF16)
        kt = k_ref[pl.ds(pl.multiple_of(t * tile, tile), tile), :]
        return jnp.dot(jnp.concatenate([kt, onehot], axis=1), qaug_ref[...],
                       preferred_element_type=F32)

    s_refs, p_refs = (s0_ref, s1_ref), (p0_ref, p1_ref)
    kr = lax.broadcasted_iota(jnp.int32, (tile, tile), 0)
    qc = lax.broadcasted_iota(jnp.int32, (tile, tile), 1)
    s_refs[0][...] = jnp.where(kr <= qc, scores(ti, True), NEG)
    p_refs[1][...] = jnp.zeros((tile, tile), BF16)
    acc_ref[...] = jnp.zeros_like(acc_ref)

    def trip(n, par, carry):
        m, alpha_prev = carry
        tc = jnp.where(n == 1, ti, jnp.clip(n - 2, 0, ti))
        pv = jnp.dot(jnp.concatenate([vt_ref[tc], ones_rows], axis=0), p_refs[1 - par][...],
                     preferred_element_type=F32)
        s_refs[1 - par][...] = scores(jnp.minimum(n, ti), n < ti)
        s = s_refs[par][...]
        m_new = jnp.maximum(m, jnp.max(s, axis=0, keepdims=True))
        alpha = jnp.exp2(m - m_new)
        p_refs[par][...] = jnp.exp2(s - m_new).astype(BF16)
        acc_ref[...] = acc_ref[...] * alpha_prev + pv
        return m_new, alpha

    def body(j, carry):
        return trip(2 * j + 1, 1, trip(2 * j, 0, carry))

    init = (jnp.full((1, tile), NEG, F32), jnp.ones((1, tile), F32))
    lax.fori_loop(0, (ti + 3) // 2, body, init)
    o_ref[...] = (acc_ref[0:HEAD_DIM, :] / acc_ref[HEAD_DIM:HEAD_DIM + 1, :]).T.astype(o_ref.dtype)


def _moba(proj, kn, km, vt, gq, batch):
    t = proj.shape[0]
    s = t // batch
    nb = s // MOBA_BLOCK
    nt = s // MOBA_TILE
    km = km.reshape(batch, nb, GROUP)
    return pl.pallas_call(
        functools.partial(_moba_kernel, nb=nb),
        grid=(batch, N_HEADS, nt),
        in_specs=[pl.BlockSpec((MOBA_TILE, HEAD_DIM), lambda b, h, i: (b * nt + i, CB_MQ + h)),
                  pl.BlockSpec((s, HEAD_DIM), lambda b, h, i: (b, h)),
                  pl.BlockSpec((nt, HEAD_DIM, MOBA_TILE), lambda b, h, i: (b, h, 0)),
                  pl.BlockSpec((1, nb, HEAD_DIM), lambda b, h, i: (b, 0, h)),
                  pl.BlockSpec((1, HEAD_DIM), lambda b, h, i: (0, 0))],
        out_specs=pl.BlockSpec((MOBA_TILE, HEAD_DIM), lambda b, h, i: (b * nt + i, h)),
        out_shape=jax.ShapeDtypeStruct((t, GROUP), BF16),
        scratch_shapes=[pltpu.VMEM((nb, MOBA_TILE), F32),
                        pltpu.VMEM((HEAD_DIM + BIAS_ROWS, MOBA_TILE), F32),
                        pltpu.VMEM((2 * HEAD_DIM, MOBA_TILE), BF16),
                        pltpu.VMEM((MOBA_TILE, MOBA_TILE), F32), pltpu.VMEM((MOBA_TILE, MOBA_TILE), F32),
                        pltpu.VMEM((MOBA_TILE, MOBA_TILE), BF16), pltpu.VMEM((MOBA_TILE, MOBA_TILE), BF16)],
        compiler_params=_cparams(("parallel", "parallel", "arbitrary")),
        name="moba",
    )(proj, kn, vt, km, gq)


GDN_ROWS = 256
GDN_HPS = 8


def _conv_silu(x_ref, halo_ref, w_ref, first):
    r = x_ref.shape[0]
    halo = halo_ref[...] * jnp.where(first, 0.0, 1.0)
    xb = jnp.concatenate([halo, x_ref[...]], axis=0)
    w = w_ref[...]
    out = None
    for tap in range(GDN_CONV):
        sh = GDN_CONV - 1 - tap
        xs = xb if sh == 0 else pltpu.roll(xb, sh, axis=0)
        term = xs[8:8 + r] * w[tap:tap + 1, :]
        out = term if out is None else out + term
    return _silu(out)


def _gdn_heads(qs, ks, vs, gs, betas, states):
    r = qs[0].shape[0]
    c = GDN_CHUNK
    pair = 2 * c
    npair = r // pair
    nh = len(qs)
    row = lax.broadcasted_iota(jnp.int32, (pair, pair), 0)
    col = lax.broadcasted_iota(jnp.int32, (pair, pair), 1)
    same = (row // c) == (col // c)
    tril = jnp.logical_and(same, row >= col)
    strict = jnp.logical_and(same, row > col)
    eye = (row == col).astype(F32)
    rin = row % c
    units = [(h, pi) for pi in range(npair) for h in range(nh)]

    def rows(x, u):
        return x[u[0]][u[1] * pair:(u[1] + 1) * pair]

    gcum = {u: rows(gs, u) for u in units}
    decay = {u: jnp.exp(jnp.where(tril, gcum[u] - gcum[u].T, -jnp.inf)) for u in units}
    eg = {u: jnp.exp(gcum[u]) for u in units}
    g_end = {u: (gcum[u][c - 1:c, :], gcum[u][pair - 1:pair, :]) for u in units}
    kb = {u: rows(ks, u) * rows(betas, u) for u in units}
    vb = {u: rows(vs, u) * rows(betas, u) for u in units}
    lmat = {u: jnp.where(strict, _bdot_nt(kb[u], rows(ks, u)) * decay[u], 0.0) for u in units}
    qk = {u: _bdot_nt(rows(qs, u), rows(ks, u)) * decay[u] for u in units}
    tinv = {u: eye - lmat[u] for u in units}
    lpow = lmat
    span = 1
    while 2 * span < c:
        lpow = {u: _bdot(lpow[u], lpow[u]) for u in units}
        tinv = {u: tinv[u] + _bdot(tinv[u], lpow[u]) for u in units}
        span *= 2
    uw = {u: _bdot(tinv[u], jnp.concatenate([vb[u], kb[u] * eg[u]], axis=1)) for u in units}
    qd = {u: rows(qs, u) * eg[u] for u in units}
    kdt = {u: (rows(ks, u) * jnp.exp(jnp.where(row < c, g_end[u][0], g_end[u][1]) - gcum[u])).T for u in units}

    states = list(states)
    vns = {u: [] for u in units}
    o_st = {u: [] for u in units}
    for pi in range(npair):
        for ci in range(2):
            cs = slice(ci * c, (ci + 1) * c)
            for h in range(nh):
                u = (h, pi)
                ws = _bdot(jnp.concatenate([uw[u][cs, HEAD_DIM:], qd[u][cs]], axis=0), states[h])
                vn = uw[u][cs, :HEAD_DIM] - ws[:c]
                o_st[u].append(ws[c:])
                vns[u].append(vn)
                zero = jnp.zeros_like(vn)
                vn_pad = jnp.concatenate([vn, zero] if ci == 0 else [zero, vn], axis=0)
                states[h] = states[h] * jnp.exp(g_end[u][ci]) + _bdot(kdt[u], vn_pad)
    outs = []
    for h in range(nh):
        parts = [jnp.concatenate(o_st[(h, pi)], axis=0) + _bdot(qk[(h, pi)], jnp.concatenate(vns[(h, pi)], axis=0))
                 for pi in range(npair)]
        outs.append(jnp.concatenate(parts, axis=0))
    return outs, states


def _gdn_kernel(q_ref, k_ref, v_ref, qh_ref, kh_ref, vh_ref, wq_ref, wk_ref, wv_ref,
                gab_ref, z_ref, hp_ref, gn_ref, o_ref, s_ref):
    hg = pl.program_id(1)
    first = pl.program_id(2) == 0

    @pl.when(first)
    def _():
        s_ref[...] = jnp.zeros_like(s_ref)

    q2 = _conv_silu(q_ref, qh_ref, wq_ref, first)
    k2 = _conv_silu(k_ref, kh_ref, wk_ref, first)
    v2 = _conv_silu(v_ref, vh_ref, wv_ref, first)
    gab = gab_ref[...]
    lane = lax.broadcasted_iota(jnp.int32, gab.shape, 1)
    xg = gab + hp_ref[1:2, :]
    softplus = jnp.maximum(xg, 0.0) + jnp.log1p(jnp.exp(-jnp.abs(xg)))
    gcum_all = -jnp.exp(hp_ref[0:1, :]) * softplus
    beta_all = _sigmoid(gab)
    rin = lax.broadcasted_iota(jnp.int32, gab.shape, 0) % GDN_CHUNK
    sh = 1
    while sh < GDN_CHUNK:
        gcum_all = gcum_all + jnp.where(rin >= sh, pltpu.roll(gcum_all, sh, axis=0), 0.0)
        sh *= 2
    zeros = jnp.zeros_like(gab)

    qs, ks, vs, gs, betas = [], [], [], [], []
    for hh in range(GDN_HPS):
        h = hg * GDN_HPS + hh
        sl = slice(hh * HEAD_DIM, (hh + 1) * HEAD_DIM)
        q, k = q2[:, sl], k2[:, sl]
        qs.append(q * lax.rsqrt(jnp.sum(q * q, axis=-1, keepdims=True) + EPS) * (HEAD_DIM ** -0.5))
        ks.append(k * lax.rsqrt(jnp.sum(k * k, axis=-1, keepdims=True) + EPS))
        vs.append(v2[:, sl])
        gs.append(jnp.sum(jnp.where(lane == h, gcum_all, 0.0), axis=1, keepdims=True) + zeros)
        betas.append(jnp.sum(jnp.where(lane == N_HEADS + h, beta_all, 0.0), axis=1, keepdims=True) + zeros)

    outs, states = _gdn_heads(qs, ks, vs, gs, betas, [s_ref[hh] for hh in range(GDN_HPS)])
    for hh in range(GDN_HPS):
        sl = slice(hh * HEAD_DIM, (hh + 1) * HEAD_DIM)
        s_ref[hh] = states[hh]
        on = _rms_rows(outs[hh], gn_ref[...])
        o_ref[:, sl] = (on * _silu(z_ref[:, sl])).astype(o_ref.dtype)


def _gdn(proj, conv_w, hp, gn, batch):
    t = proj.shape[0]
    s = t // batch
    r = GDN_ROWS
    steps = s // r
    hb = r // 8
    wide = GDN_HPS * HEAD_DIM
    cpb = GDN_HPS

    def main(cb):
        return pl.BlockSpec((r, wide), lambda b, h, i: (b * steps + i, cb // cpb + h))

    def halo(cb):
        return pl.BlockSpec((8, wide), lambda b, h, i: (jnp.maximum((b * steps + i) * hb - 1, 0), cb // cpb + h))

    def wspec(off):
        return pl.BlockSpec((GDN_CONV, wide), lambda b, h, i: (0, off // cpb + h))

    return pl.pallas_call(
        _gdn_kernel,
        grid=(batch, N_HEADS // GDN_HPS, steps),
        in_specs=[main(CB_GQ), main(CB_GK), main(CB_GV), halo(CB_GQ), halo(CB_GK), halo(CB_GV),
                  wspec(0), wspec(N_HEADS), wspec(2 * N_HEADS),
                  pl.BlockSpec((r, LANES), lambda b, h, i: (b * steps + i, CB_GAB)),
                  main(CB_GZ),
                  pl.BlockSpec((2, LANES), lambda b, h, i: (0, 0)),
                  pl.BlockSpec((1, HEAD_DIM), lambda b, h, i: (0, 0))],
        out_specs=pl.BlockSpec((r, wide), lambda b, h, i: (b * steps + i, h)),
        out_shape=jax.ShapeDtypeStruct((t, GROUP), BF16),
        scratch_shapes=[pltpu.VMEM((GDN_HPS, HEAD_DIM, HEAD_DIM), F32)],
        compiler_params=_cparams(("parallel", "parallel", "arbitrary")),
        name="gdn",
    )(proj, proj, proj, proj, proj, proj, conv_w, conv_w, conv_w, proj, proj, hp, gn)


SC_ROWS = 512


def _sconv_kernel(b_ref, c_ref, x_ref, ch_ref, xh_ref, w_ref, o_ref, *, steps):
    first = pl.program_id(0) % steps == 0
    r = b_ref.shape[0]
    y = c_ref[...] * x_ref[...]
    yh = ch_ref[...] * xh_ref[...] * jnp.where(first, 0.0, 1.0)
    yb = jnp.concatenate([yh, y], axis=0)
    w = w_ref[...]
    out = None
    for tap in range(SC_CONV):
        sh = SC_CONV - 1 - tap
        ys = yb if sh == 0 else pltpu.roll(yb, sh, axis=0)
        term = ys[8:8 + r] * w[tap:tap + 1, :]
        out = term if out is None else out + term
    o_ref[...] = (b_ref[...] * out).astype(o_ref.dtype)


def _sconv(proj, w, batch):
    t = proj.shape[0]
    r = min(SC_ROWS, t // batch)
    steps = (t // batch) // r
    hb = r // 8

    def main(cb):
        return pl.BlockSpec((r, GROUP), lambda i: (i, cb // 8))

    def halo(cb):
        return pl.BlockSpec((8, GROUP), lambda i: (jnp.maximum(i * hb - 1, 0), cb // 8))

    return pl.pallas_call(
        functools.partial(_sconv_kernel, steps=steps),
        grid=(t // r,),
        in_specs=[main(CB_SCB), main(CB_SCC), main(CB_SCX), halo(CB_SCC), halo(CB_SCX),
                  pl.BlockSpec((SC_CONV, GROUP), lambda i: (0, 0))],
        out_specs=pl.BlockSpec((r, GROUP), lambda i: (i, 0)),
        out_shape=jax.ShapeDtypeStruct((t, GROUP), BF16),
        compiler_params=_cparams(("parallel",)),
        name="sconv",
    )(proj, proj, proj, proj, proj, w)


SWA_ROWS = 512


def _half_rms(x, g2):
    lane = lax.broadcasted_iota(jnp.int32, x.shape, 1)
    lo = lane < SWA_D
    x2 = x * x
    ms_lo = jnp.sum(jnp.where(lo, x2, 0.0), axis=-1, keepdims=True) * (1.0 / SWA_D)
    ms_hi = jnp.sum(jnp.where(lo, 0.0, x2), axis=-1, keepdims=True) * (1.0 / SWA_D)
    rs = jnp.where(lo, lax.rsqrt(ms_lo + EPS), lax.rsqrt(ms_hi + EPS))
    return x * rs * g2


def _swa_kernel(q_ref, k_ref, v_ref, kh_ref, vh_ref, gq_ref, gk_ref, sink_ref, o_ref, *, steps):
    first = pl.program_id(0) % steps == 0
    r = q_ref.shape[0]
    w = SWA_W
    nsub = r // w
    pairs = SWA_Q_HEADS // 2
    lane = lax.broadcasted_iota(jnp.int32, (r + w, LANES), 1)

    kn = _half_rms(jnp.concatenate([kh_ref[...], k_ref[...]], axis=0), gk_ref[...])
    kroll = pltpu.roll(kn, SWA_D, axis=1)
    kdup = (jnp.where(lane < SWA_D, kn, kroll).astype(BF16),
            jnp.where(lane < SWA_D, kroll, kn).astype(BF16))
    vt = jnp.concatenate([vh_ref[...], v_ref[...]], axis=0).T.astype(BF16)

    kr = lax.broadcasted_iota(jnp.int32, (2 * w, 2 * w), 0)
    qc = lax.broadcasted_iota(jnp.int32, (2 * w, 2 * w), 1) % w
    band = jnp.logical_and(kr > qc, kr <= qc + w)
    band0 = jnp.logical_and(band, kr >= jnp.where(first, w, 0))
    qlane = lax.broadcasted_iota(jnp.int32, (w, LANES), 1)

    heads = range(pairs)
    kvh = [c // (pairs // SWA_KV_HEADS) for c in heads]
    for sub in range(nsub):
        mask = band0 if sub == 0 else band
        ks = slice(sub * w, sub * w + 2 * w)
        rows = slice(sub * w, (sub + 1) * w)
        qts = []
        for c in heads:
            qn = _half_rms(q_ref[rows, c * LANES:(c + 1) * LANES], gq_ref[...]) * (SWA_D ** -0.5 * LOG2E)
            qa = jnp.where(qlane < SWA_D, qn, 0.0)
            qb = jnp.where(qlane < SWA_D, 0.0, qn)
            qts.append(jnp.concatenate([qa.T, qb.T], axis=1).astype(BF16))
        ss = [jnp.dot(kdup[kvh[c]][ks], qts[c], preferred_element_type=F32) for c in heads]
        pns = []
        for c in heads:
            s = jnp.where(mask, ss[c], NEG)
            sink = sink_ref[c:c + 1, :] * LOG2E
            m = jnp.maximum(jnp.max(s, axis=0, keepdims=True), sink)
            p = jnp.exp2(s - m)
            l = jnp.sum(p, axis=0, keepdims=True) + jnp.exp2(sink - m)
            pns.append((p * (1.0 / l)).astype(BF16))
        ots = [jnp.dot(vt[kvh[c] * SWA_D:(kvh[c] + 1) * SWA_D, ks], pns[c], preferred_element_type=F32)
               for c in heads]
        for c in heads:
            o = jnp.concatenate([ots[c][:, :w], ots[c][:, w:]], axis=0).T
            o_ref[rows, c * LANES:(c + 1) * LANES] = o.astype(o_ref.dtype)


def _swa(proj, gq2, gk2, sinkrow, batch):
    t = proj.shape[0]
    r = min(SWA_ROWS, t // batch)
    steps = (t // batch) // r
    hb = r // SWA_W

    def halo(cb):
        return pl.BlockSpec((SWA_W, LANES), lambda i: (jnp.maximum(i * hb - 1, 0), cb))

    return pl.pallas_call(
        functools.partial(_swa_kernel, steps=steps),
        grid=(t // r,),
        in_specs=[pl.BlockSpec((r, GROUP), lambda i: (i, CB_SQ // 8)),
                  pl.BlockSpec((r, LANES), lambda i: (i, CB_SK)),
                  pl.BlockSpec((r, LANES), lambda i: (i, CB_SV)),
                  halo(CB_SK), halo(CB_SV),
                  pl.BlockSpec((1, LANES), lambda i: (0, 0)),
                  pl.BlockSpec((1, LANES), lambda i: (0, 0)),
                  pl.BlockSpec((SWA_Q_HEADS // 2, 2 * SWA_W), lambda i: (0, 0))],
        out_specs=pl.BlockSpec((r, GROUP), lambda i: (i, 0)),
        out_shape=jax.ShapeDtypeStruct((t, GROUP), BF16),
        compiler_params=_cparams(("parallel",)),
        name="swa",
    )(proj, proj, proj, proj, proj, gq2, gk2, sinkrow)


def _w_in_tail(wt):
    a1 = A_COLS + 2 * N_HEADS
    pad = jnp.zeros((wt.shape[0], NP_COLS - wt.shape[1], wt.shape[2]), BF16)
    return jnp.concatenate([wt[:, a1:, :], wt[:, A_COLS:a1, :], pad], axis=1)


def _layer(x, xb, batch, layer, w_in, w_in_tail, moba_q_norm, moba_k_norm, gdn_conv, gdn_a_log, gdn_dt_bias,
           gdn_out_norm, sc_conv, swa_q_norm, swa_k_norm, swa_sinks, w_out, ffn_gain, w_gate, w_up, w_down):
    row = lambda a: a.reshape(1, -1).astype(F32)

    proj = _in_proj(xb, w_in, w_in_tail, layer, tm=1024, tn=512)

    kn, km, vt = _moba_prep(proj, row(moba_k_norm))
    o_a = _moba(proj, kn, km, vt, row(moba_q_norm), batch)

    hp = jnp.pad(jnp.stack([gdn_a_log, gdn_dt_bias]).astype(F32), ((0, 0), (0, LANES - N_HEADS)))
    o_b = _gdn(proj, gdn_conv.astype(F32), hp, row(gdn_out_norm), batch)

    o_c = _sconv(proj, sc_conv.astype(F32), batch)

    sinkrow = jnp.repeat(swa_sinks.astype(F32), SWA_W).reshape(SWA_Q_HEADS // 2, 2 * SWA_W)
    o_d = _swa(proj, row(jnp.tile(swa_q_norm, 2)), row(jnp.tile(swa_k_norm, 2)), sinkrow, batch)

    x, xb = _out_proj((o_a, o_b, o_c, o_d), w_out, x, layer, tm=1024, tn=512)

    act = _ffn_up(xb, ffn_gain, w_gate, w_up, layer, D_FF_PAD, tm=1024, tn=512, tn_first=256)
    return _mm_res(act, w_down, x, layer, tm=1024, tn=1024, tk=D_FF_PAD // 4, tn_first=512)


def kernel(x, norm_mix, w_in, moba_q_norm, moba_k_norm, gdn_conv, gdn_a_log, gdn_dt_bias, gdn_out_norm, sc_conv, swa_q_norm, swa_k_norm, swa_sinks, w_out, norm_ffn, w_gate, w_up, w_down):
    batch, seq, d = x.shape
    w_in_b = (jnp.swapaxes(w_in, 1, 2) * norm_mix[:, None, :]).astype(BF16)
    w_in_t = _w_in_tail(w_in_b)
    w_out_b = w_out.astype(BF16)
    ffn_gain = norm_ffn.astype(F32)[:, :, None]
    h = x.reshape(batch * seq, d)
    hb = h.astype(BF16)
    for l in range(norm_mix.shape[0]):
        h, hb = _layer(h, hb, batch, l, w_in_b, w_in_t, moba_q_norm[l], moba_k_norm[l], gdn_conv[l], gdn_a_log[l],
                       gdn_dt_bias[l], gdn_out_norm[l], sc_conv[l], swa_q_norm[l], swa_k_norm[l], swa_sinks[l],
                       w_out_b, ffn_gain, w_gate, w_up, w_down)
    return h.reshape(batch, seq, d)
```

```python
import functools

import jax
import jax.numpy as jnp
from jax import lax
from jax.experimental import pallas as pl
from jax.experimental.pallas import tpu as pltpu

F32 = jnp.float32
BF16 = jnp.bfloat16

EPS = 1e-6
LANES = 128
GROUP = 1024
HEAD_DIM = 128
N_HEADS = GROUP // HEAD_DIM
MOBA_BLOCK = 256
MOBA_TILE = 2 * MOBA_BLOCK
MOBA_TOPK = 3
GDN_CONV = 4
GDN_CHUNK = 64
SC_CONV = 3
SWA_D = 64
SWA_Q_HEADS = GROUP // SWA_D
SWA_KV_HEADS = 2
SWA_W = 128
NEG = -1e30
LOG2E = 1.4426950408889634
BIAS_ROWS = 16

CB_MQ, CB_MK, CB_MV = 0, 8, 16
CB_GQ, CB_GK, CB_GV, CB_GZ = 24, 32, 40, 48
CB_SCB, CB_SCC, CB_SCX = 56, 64, 72
CB_SQ, CB_SK, CB_SV, CB_GAB = 80, 88, 89, 90
NP_COLS = 92 * LANES
A_COLS = 6 * GROUP
D_FF_PAD = 11264

VMEM_LIMIT = 56 * 1024 * 1024


def _cparams(sem, vmem=VMEM_LIMIT):
    return pltpu.CompilerParams(dimension_semantics=sem, vmem_limit_bytes=vmem)


def _bdot(a, b):
    return jnp.dot(a.astype(BF16), b.astype(BF16), preferred_element_type=F32)


def _bdot_nt(a, b):
    return lax.dot_general(a.astype(BF16), b.astype(BF16), (((1,), (1,)), ((), ())),
                           preferred_element_type=F32)


def _sigmoid(x):
    return 1.0 / (1.0 + jnp.exp(-x))


def _silu(x):
    return x * _sigmoid(x)


def _rms_rows(x, g):
    ms = jnp.mean(x * x, axis=-1, keepdims=True)
    return x * lax.rsqrt(ms + EPS) * g


NORM_CHUNK = 256


def _row_scale_to(xb_ref, rs_ref):
    chunk = min(NORM_CHUNK, xb_ref.shape[0])

    def body(c, carry):
        rows = pl.ds(pl.multiple_of(c * chunk, chunk), chunk)
        x = xb_ref[rows, :].astype(F32)
        ms = jnp.mean(x * x, axis=-1, keepdims=True)
        rs_ref[rows, :] = jnp.broadcast_to(lax.rsqrt(ms + EPS), (chunk, LANES))
        return carry

    lax.fori_loop(0, xb_ref.shape[0] // chunk, body, 0)


def _scaled_dot(xb_ref, w_ref, rs_ref, w_transposed=False):
    dims = (((1,), (1,)), ((), ())) if w_transposed else (((1,), (0,)), ((), ()))
    acc = lax.dot_general(xb_ref[...], w_ref[...], dims, preferred_element_type=F32)
    rs = rs_ref[...]
    return jnp.concatenate([acc[:, c * LANES:(c + 1) * LANES] * rs for c in range(acc.shape[1] // LANES)], axis=1)


def _in_proj_kernel(xb_ref, w_ref, o_ref, rs_ref):
    @pl.when(pl.program_id(1) == 0)
    def _():
        _row_scale_to(xb_ref, rs_ref)

    o_ref[...] = _scaled_dot(xb_ref, w_ref, rs_ref, w_transposed=True).astype(o_ref.dtype)


def _in_proj(xb, w, layer, tm, tn):
    t, d = xb.shape
    n = w.shape[1]
    tm = min(tm, t)
    return pl.pallas_call(
        _in_proj_kernel,
        grid=(t // tm, n // tn),
        in_specs=[pl.BlockSpec((tm, d), lambda i, j: (i, 0)),
                  pl.BlockSpec((None, tn, d), lambda i, j: (layer, j, 0))],
        out_specs=pl.BlockSpec((tm, tn), lambda i, j: (i, j)),
        out_shape=jax.ShapeDtypeStruct((t, n), F32),
        scratch_shapes=[pltpu.VMEM((tm, LANES), F32)],
        compiler_params=_cparams(("parallel", "arbitrary")),
        name="in_proj",
    )(xb, w)


FIRST_ROWS = 1024


def _ffn_up_kernel(xb_ref, wg_ref, wu_ref, o_ref, rs_ref):
    @pl.when(pl.program_id(1) == 0)
    def _():
        _row_scale_to(xb_ref, rs_ref)

    a = _scaled_dot(xb_ref, wg_ref, rs_ref)
    b = _scaled_dot(xb_ref, wu_ref, rs_ref)
    o_ref[...] = (_silu(a) * b).astype(o_ref.dtype)


def _ffn_up_first_kernel(xb_ref, g_ref, wg32_ref, wu32_ref, o_ref, wgb_ref, wub_ref, rs_ref, *, valid):
    j = pl.program_id(0)

    @pl.when(j == 0)
    def _():
        _row_scale_to(xb_ref, rs_ref)

    col = j * wg32_ref.shape[1] + lax.broadcasted_iota(jnp.int32, wg32_ref.shape, 1)
    wgb_ref[...] = jnp.where(col < valid, wg32_ref[...] * g_ref[...], 0.0).astype(BF16)
    wub_ref[...] = jnp.where(col < valid, wu32_ref[...] * g_ref[...], 0.0).astype(BF16)
    a = _scaled_dot(xb_ref, wgb_ref, rs_ref)
    b = _scaled_dot(xb_ref, wub_ref, rs_ref)
    o_ref[...] = (_silu(a) * b).astype(o_ref.dtype)


def _ffn_up_rest_kernel(xb_ref, wg_ref, wu_ref, dst_ref, o_ref, rs_ref):
    del dst_ref
    _ffn_up_kernel(xb_ref, wg_ref, wu_ref, o_ref, rs_ref)


def _ffn_up(xb, gain, w_gate, w_up, layer, n_pad, tm, tn, tn_first):
    t, d = xb.shape
    n_real = w_gate.shape[2]
    tf = min(FIRST_ROWS, t)
    fi = t // tf - 1
    last_blk = (n_real - 1) // tn_first
    w32_spec = pl.BlockSpec((None, d, tn_first), lambda j: (layer, 0, jnp.minimum(j, last_blk)))
    wb_spec = pl.BlockSpec((d, tn_first), lambda j: (0, j))
    act, wgb, wub = pl.pallas_call(
        functools.partial(_ffn_up_first_kernel, valid=n_real),
        grid=(n_pad // tn_first,),
        in_specs=[pl.BlockSpec((tf, d), lambda j: (fi, 0)),
                  pl.BlockSpec((None, d, 1), lambda j: (layer, 0, 0)),
                  w32_spec, w32_spec],
        out_specs=[pl.BlockSpec((tf, tn_first), lambda j: (fi, j)), wb_spec, wb_spec],
        out_shape=[jax.ShapeDtypeStruct((t, n_pad), BF16),
                   jax.ShapeDtypeStruct((d, n_pad), BF16), jax.ShapeDtypeStruct((d, n_pad), BF16)],
        scratch_shapes=[pltpu.VMEM((tf, LANES), F32)],
        compiler_params=_cparams(("arbitrary",)),
        name="ffn_up_first",
    )(xb, gain, w_gate, w_up)
    if t == tf:
        return act
    w_spec = pl.BlockSpec((d, tn), lambda i, j: (0, j))
    return pl.pallas_call(
        _ffn_up_rest_kernel,
        grid=((t - tf) // tm, n_pad // tn),
        in_specs=[pl.BlockSpec((tm, d), lambda i, j: (i, 0)), w_spec, w_spec,
                  pl.BlockSpec(memory_space=pl.ANY)],
        out_specs=pl.BlockSpec((tm, tn), lambda i, j: (i, j)),
        out_shape=jax.ShapeDtypeStruct((t, n_pad), BF16),
        scratch_shapes=[pltpu.VMEM((tm, LANES), F32)],
        input_output_aliases={3: 0},
        compiler_params=_cparams(("parallel", "arbitrary")),
        name="ffn_up",
    )(xb, wgb, wub, act)


def _mm_res_kernel(a_ref, w_ref, r_ref, o_ref, ob_ref):
    k = pl.program_id(2)

    @pl.when(k == 0)
    def _():
        o_ref[...] = r_ref[...] + jnp.dot(a_ref[...], w_ref[...], preferred_element_type=F32)

    @pl.when(k > 0)
    def _():
        o_ref[...] += jnp.dot(a_ref[...], w_ref[...], preferred_element_type=F32)

    @pl.when(k == pl.num_programs(2) - 1)
    def _():
        ob_ref[...] = o_ref[...].astype(BF16)


def _mm_res_first_kernel(a_ref, w32_ref, r_ref, o_ref, ob_ref, wb_ref, *, valid):
    k = pl.program_id(1)
    row = k * w32_ref.shape[0] + lax.broadcasted_iota(jnp.int32, w32_ref.shape, 0)
    wb_ref[...] = jnp.where(row < valid, w32_ref[...], 0.0).astype(BF16)

    @pl.when(k == 0)
    def _():
        o_ref[...] = r_ref[...] + jnp.dot(a_ref[...], wb_ref[...], preferred_element_type=F32)

    @pl.when(k > 0)
    def _():
        o_ref[...] += jnp.dot(a_ref[...], wb_ref[...], preferred_element_type=F32)

    @pl.when(k == pl.num_programs(1) - 1)
    def _():
        ob_ref[...] = o_ref[...].astype(BF16)


def _mm_res_rest_kernel(a_ref, w_ref, r_ref, dst_ref, dstb_ref, o_ref, ob_ref):
    del dst_ref, dstb_ref
    _mm_res_kernel(a_ref, w_ref, r_ref, o_ref, ob_ref)


def _mm_res(a, w, r, layer, tm, tn, tk, tn_first):
    t, kd = a.shape
    k_real, n = w.shape[1], w.shape[2]
    tf = min(FIRST_ROWS, t)
    fi = t // tf - 1
    last_blk = (k_real - 1) // tk
    of_spec = pl.BlockSpec((tf, tn_first), lambda j, k: (fi, j))
    x, xb, wb = pl.pallas_call(
        functools.partial(_mm_res_first_kernel, valid=k_real),
        grid=(n // tn_first, kd // tk),
        in_specs=[pl.BlockSpec((tf, tk), lambda j, k: (fi, k)),
                  pl.BlockSpec((None, tk, tn_first), lambda j, k: (layer, jnp.minimum(k, last_blk), j)),
                  of_spec],
        out_specs=[of_spec, of_spec, pl.BlockSpec((tk, tn_first), lambda j, k: (k, j))],
        out_shape=[jax.ShapeDtypeStruct((t, n), F32), jax.ShapeDtypeStruct((t, n), BF16),
                   jax.ShapeDtypeStruct((kd, n), BF16)],
        compiler_params=_cparams(("parallel", "arbitrary")),
        name="mm_res_first",
    )(a, w, r)
    if t == tf:
        return x, xb
    o_spec = pl.BlockSpec((tm, tn), lambda i, j, k: (i, j))
    any_spec = pl.BlockSpec(memory_space=pl.ANY)
    return pl.pallas_call(
        _mm_res_rest_kernel,
        grid=((t - tf) // tm, n // tn, kd // tk),
        in_specs=[pl.BlockSpec((tm, tk), lambda i, j, k: (i, k)),
                  pl.BlockSpec((tk, tn), lambda i, j, k: (k, j)),
                  o_spec, any_spec, any_spec],
        out_specs=[o_spec, o_spec],
        out_shape=[jax.ShapeDtypeStruct((t, n), F32), jax.ShapeDtypeStruct((t, n), BF16)],
        input_output_aliases={3: 0, 4: 1},
        compiler_params=_cparams(("parallel", "parallel", "arbitrary")),
        name="mm_res",
    )(a, wb, r, x, xb)


def _out_proj_kernel(a0_ref, a1_ref, a2_ref, a3_ref, w_ref, r_ref, o_ref, ob_ref):
    acc = r_ref[...]
    for g, a_ref in enumerate((a0_ref, a1_ref, a2_ref, a3_ref)):
        acc = acc + jnp.dot(a_ref[...], w_ref[g * GROUP:(g + 1) * GROUP, :], preferred_element_type=F32)
    o_ref[...] = acc
    ob_ref[...] = acc.astype(BF16)


def _out_proj(mix, w, r, layer, tm, tn):
    t = r.shape[0]
    n = w.shape[2]
    tm = min(tm, t)
    a_spec = pl.BlockSpec((tm, GROUP), lambda i, j: (i, 0))
    o_spec = pl.BlockSpec((tm, tn), lambda i, j: (i, j))
    return pl.pallas_call(
        _out_proj_kernel,
        grid=(t // tm, n // tn),
        in_specs=[a_spec, a_spec, a_spec, a_spec,
                  pl.BlockSpec((None, 4 * GROUP, tn), lambda i, j: (layer, 0, j)),
                  o_spec],
        out_specs=[o_spec, o_spec],
        out_shape=[jax.ShapeDtypeStruct((t, n), F32), jax.ShapeDtypeStruct((t, n), BF16)],
        compiler_params=_cparams(("parallel", "arbitrary")),
        name="out_proj",
    )(*mix, w, r)


def _moba_prep_kernel(k_ref, v_ref, g_ref, kn_ref, km_ref, vt_ref):
    g = g_ref[...]
    for h in range(N_HEADS):
        sl = slice(h * HEAD_DIM, (h + 1) * HEAD_DIM)
        kn = _rms_rows(k_ref[:, sl], g)
        kn_ref[:, sl] = kn.astype(BF16)
        for half in range(MOBA_TILE // MOBA_BLOCK):
            km_ref[half, :, sl] = jnp.mean(kn[half * MOBA_BLOCK:(half + 1) * MOBA_BLOCK], axis=0, keepdims=True)
        vt_ref[0, sl, :] = v_ref[:, sl].T.astype(BF16)


def _moba_prep(proj, gk):
    t = proj.shape[0]
    ntile = t // MOBA_TILE
    per = MOBA_TILE // MOBA_BLOCK
    return pl.pallas_call(
        _moba_prep_kernel,
        grid=(ntile,),
        in_specs=[pl.BlockSpec((MOBA_TILE, GROUP), lambda i: (i, CB_MK // 8)),
                  pl.BlockSpec((MOBA_TILE, GROUP), lambda i: (i, CB_MV // 8)),
                  pl.BlockSpec((1, HEAD_DIM), lambda i: (0, 0))],
        out_specs=[pl.BlockSpec((MOBA_TILE, GROUP), lambda i: (i, 0)),
                   pl.BlockSpec((per, 1, GROUP), lambda i: (i, 0, 0)),
                   pl.BlockSpec((1, GROUP, MOBA_TILE), lambda i: (i, 0, 0))],
        out_shape=[jax.ShapeDtypeStruct((t, GROUP), BF16),
                   jax.ShapeDtypeStruct((ntile * per, 1, GROUP), F32),
                   jax.ShapeDtypeStruct((ntile, GROUP, MOBA_TILE), BF16)],
        compiler_params=_cparams(("parallel",)),
        name="moba_prep",
    )(proj, proj, gk)


def _moba_kernel(q_ref, k_ref, vt_ref, km_ref, g_ref, o_ref, bias_ref, acc_ref, qaug_ref, s0_ref, s1_ref,
                 p0_ref, p1_ref, *, nb):
    ti = pl.program_id(2)
    blk, tile = MOBA_BLOCK, MOBA_TILE
    qn = _rms_rows(q_ref[...], g_ref[...])

    gate = lax.dot_general(km_ref[0], qn, (((1,), (1,)), ((), ())),
                           precision=lax.Precision.HIGHEST, preferred_element_type=F32)
    row = lax.broadcasted_iota(jnp.int32, gate.shape, 0)
    own = 2 * ti + (lax.broadcasted_iota(jnp.int32, gate.shape, 1) >= blk).astype(jnp.int32)
    rowf = row.astype(F32)
    gate = jnp.where(row < own, gate, -jnp.inf)
    bias = jnp.where(row == own, 0.0, NEG)
    for _ in range(MOBA_TOPK):
        m = jnp.max(gate, axis=0, keepdims=True)
        idx = jnp.min(jnp.where(gate == m, rowf, float(nb)), axis=0, keepdims=True)
        hit = jnp.logical_and(rowf == idx, m > -jnp.inf)
        bias = jnp.where(hit, 0.0, bias)
        gate = jnp.where(hit, -jnp.inf, gate)
    bias_ref[...] = bias

    qaug_ref[0:HEAD_DIM, :] = (qn * (HEAD_DIM ** -0.5 * LOG2E)).T.astype(BF16)
    qaug_ref[HEAD_DIM + BIAS_ROWS:, :] = jnp.zeros((HEAD_DIM - BIAS_ROWS, tile), BF16)
    er = lax.broadcasted_iota(jnp.int32, (tile, HEAD_DIM), 0)
    ec = lax.broadcasted_iota(jnp.int32, (tile, HEAD_DIM), 1)
    onehot = jnp.where(ec == er // blk, 1.0, 0.0).astype(BF16)
    brow = lax.broadcasted_iota(jnp.int32, (BIAS_ROWS, tile), 0)
    ones_rows = jnp.ones((BIAS_ROWS, tile), BF16)

    def scores(t, valid):
        b0 = jnp.where(valid, bias_ref[pl.ds(2 * t, 1), :], NEG)
        b1 = jnp.where(valid, bias_ref[pl.ds(2 * t + 1, 1), :], NEG)
        qaug_ref[HEAD_DIM:HEAD_DIM + BIAS_ROWS, :] = jnp.where(
            brow == 0, b0, jnp.where(brow == 1, b1, 0.0)).astype(BF16)
        kt = k_ref[pl.ds(pl.multiple_of(t * tile, tile), tile), :]
        return jnp.dot(jnp.concatenate([kt, onehot], axis=1), qaug_ref[...],
                       preferred_element_type=F32)

    s_refs, p_refs = (s0_ref, s1_ref), (p0_ref, p1_ref)
    kr = lax.broadcasted_iota(jnp.int32, (tile, tile), 0)
    qc = lax.broadcasted_iota(jnp.int32, (tile, tile), 1)
    s_refs[0][...] = jnp.where(kr <= qc, scores(ti, True), NEG)
    p_refs[1][...] = jnp.zeros((tile, tile), BF16)
    acc_ref[...] = jnp.zeros_like(acc_ref)

    def trip(n, par, carry):
        m, alpha_prev = carry
        tc = jnp.where(n == 1, ti, jnp.clip(n - 2, 0, ti))
        pv = jnp.dot(jnp.concatenate([vt_ref[tc], ones_rows], axis=0), p_refs[1 - par][...],
                     preferred_element_type=F32)
        s_refs[1 - par][...] = scores(jnp.minimum(n, ti), n < ti)
        s = s_refs[par][...]
        m_new = jnp.maximum(m, jnp.max(s, axis=0, keepdims=True))
        alpha = jnp.exp2(m - m_new)
        p_refs[par][...] = jnp.exp2(s - m_new).astype(BF16)
        acc_ref[...] = acc_ref[...] * alpha_prev + pv
        return m_new, alpha

    def body(j, carry):
        return trip(2 * j + 1, 1, trip(2 * j, 0, carry))

    init = (jnp.full((1, tile), NEG, F32), jnp.ones((1, tile), F32))
    lax.fori_loop(0, (ti + 3) // 2, body, init)
    o_ref[...] = (acc_ref[0:HEAD_DIM, :] / acc_ref[HEAD_DIM:HEAD_DIM + 1, :]).T.astype(o_ref.dtype)


def _moba(proj, kn, km, vt, gq, batch):
    t = proj.shape[0]
    s = t // batch
    nb = s // MOBA_BLOCK
    nt = s // MOBA_TILE
    km = km.reshape(batch, nb, GROUP)
    return pl.pallas_call(
        functools.partial(_moba_kernel, nb=nb),
        grid=(batch, N_HEADS, nt),
        in_specs=[pl.BlockSpec((MOBA_TILE, HEAD_DIM), lambda b, h, i: (b * nt + i, CB_MQ + h)),
                  pl.BlockSpec((s, HEAD_DIM), lambda b, h, i: (b, h)),
                  pl.BlockSpec((nt, HEAD_DIM, MOBA_TILE), lambda b, h, i: (b, h, 0)),
                  pl.BlockSpec((1, nb, HEAD_DIM), lambda b, h, i: (b, 0, h)),
                  pl.BlockSpec((1, HEAD_DIM), lambda b, h, i: (0, 0))],
        out_specs=pl.BlockSpec((MOBA_TILE, HEAD_DIM), lambda b, h, i: (b * nt + i, h)),
        out_shape=jax.ShapeDtypeStruct((t, GROUP), BF16),
        scratch_shapes=[pltpu.VMEM((nb, MOBA_TILE), F32),
                        pltpu.VMEM((HEAD_DIM + BIAS_ROWS, MOBA_TILE), F32),
                        pltpu.VMEM((2 * HEAD_DIM, MOBA_TILE), BF16),
                        pltpu.VMEM((MOBA_TILE, MOBA_TILE), F32), pltpu.VMEM((MOBA_TILE, MOBA_TILE), F32),
                        pltpu.VMEM((MOBA_TILE, MOBA_TILE), BF16), pltpu.VMEM((MOBA_TILE, MOBA_TILE), BF16)],
        compiler_params=_cparams(("parallel", "parallel", "arbitrary")),
        name="moba",
    )(proj, kn, vt, km, gq)


GDN_ROWS = 256
GDN_HPS = 8


def _conv_silu(x_ref, halo_ref, w_ref, first):
    r = x_ref.shape[0]
    halo = halo_ref[...] * jnp.where(first, 0.0, 1.0)
    xb = jnp.concatenate([halo, x_ref[...]], axis=0)
    w = w_ref[...]
    out = None
    for tap in range(GDN_CONV):
        sh = GDN_CONV - 1 - tap
        xs = xb if sh == 0 else pltpu.roll(xb, sh, axis=0)
        term = xs[8:8 + r] * w[tap:tap + 1, :]
        out = term if out is None else out + term
    return _silu(out)


def _gdn_heads(qs, ks, vs, gs, betas, states):
    r = qs[0].shape[0]
    c = GDN_CHUNK
    pair = 2 * c
    npair = r // pair
    nh = len(qs)
    row = lax.broadcasted_iota(jnp.int32, (pair, pair), 0)
    col = lax.broadcasted_iota(jnp.int32, (pair, pair), 1)
    same = (row // c) == (col // c)
    tril = jnp.logical_and(same, row >= col)
    strict = jnp.logical_and(same, row > col)
    eye = (row == col).astype(F32)
    rin = row % c
    units = [(h, pi) for pi in range(npair) for h in range(nh)]

    def rows(x, u):
        return x[u[0]][u[1] * pair:(u[1] + 1) * pair]

    gcum = {u: rows(gs, u) for u in units}
    decay = {u: jnp.exp(jnp.where(tril, gcum[u] - gcum[u].T, -jnp.inf)) for u in units}
    eg = {u: jnp.exp(gcum[u]) for u in units}
    g_end = {u: (gcum[u][c - 1:c, :], gcum[u][pair - 1:pair, :]) for u in units}
    kb = {u: rows(ks, u) * rows(betas, u) for u in units}
    vb = {u: rows(vs, u) * rows(betas, u) for u in units}
    lmat = {u: jnp.where(strict, _bdot_nt(kb[u], rows(ks, u)) * decay[u], 0.0) for u in units}
    qk = {u: _bdot_nt(rows(qs, u), rows(ks, u)) * decay[u] for u in units}
    tinv = {u: eye - lmat[u] for u in units}
    lpow = lmat
    span = 1
    while 2 * span < c:
        lpow = {u: _bdot(lpow[u], lpow[u]) for u in units}
        tinv = {u: tinv[u] + _bdot(tinv[u], lpow[u]) for u in units}
        span *= 2
    uw = {u: _bdot(tinv[u], jnp.concatenate([vb[u], kb[u] * eg[u]], axis=1)) for u in units}
    qd = {u: rows(qs, u) * eg[u] for u in units}
    kdt = {u: (rows(ks, u) * jnp.exp(jnp.where(row < c, g_end[u][0], g_end[u][1]) - gcum[u])).T for u in units}

    states = list(states)
    vns = {u: [] for u in units}
    o_st = {u: [] for u in units}
    for pi in range(npair):
        for ci in range(2):
            cs = slice(ci * c, (ci + 1) * c)
            for h in range(nh):
                u = (h, pi)
                ws = _bdot(jnp.concatenate([uw[u][cs, HEAD_DIM:], qd[u][cs]], axis=0), states[h])
                vn = uw[u][cs, :HEAD_DIM] - ws[:c]
                o_st[u].append(ws[c:])
                vns[u].append(vn)
                zero = jnp.zeros_like(vn)
                vn_pad = jnp.concatenate([vn, zero] if ci == 0 else [zero, vn], axis=0)
                states[h] = states[h] * jnp.exp(g_end[u][ci]) + _bdot(kdt[u], vn_pad)
    outs = []
    for h in range(nh):
        parts = [jnp.concatenate(o_st[(h, pi)], axis=0) + _bdot(qk[(h, pi)], jnp.concatenate(vns[(h, pi)], axis=0))
                 for pi in range(npair)]
        outs.append(jnp.concatenate(parts, axis=0))
    return outs, states


def _gdn_kernel(q_ref, k_ref, v_ref, qh_ref, kh_ref, vh_ref, wq_ref, wk_ref, wv_ref,
                gab_ref, z_ref, hp_ref, gn_ref, o_ref, s_ref):
    hg = pl.program_id(1)
    first = pl.program_id(2) == 0

    @pl.when(first)
    def _():
        s_ref[...] = jnp.zeros_like(s_ref)

    q2 = _conv_silu(q_ref, qh_ref, wq_ref, first)
    k2 = _conv_silu(k_ref, kh_ref, wk_ref, first)
    v2 = _conv_silu(v_ref, vh_ref, wv_ref, first)
    gab = gab_ref[...]
    lane = lax.broadcasted_iota(jnp.int32, gab.shape, 1)
    xg = gab + hp_ref[1:2, :]
    softplus = jnp.maximum(xg, 0.0) + jnp.log1p(jnp.exp(-jnp.abs(xg)))
    gcum_all = -jnp.exp(hp_ref[0:1, :]) * softplus
    beta_all = _sigmoid(gab)
    rin = lax.broadcasted_iota(jnp.int32, gab.shape, 0) % GDN_CHUNK
    sh = 1
    while sh < GDN_CHUNK:
        gcum_all = gcum_all + jnp.where(rin >= sh, pltpu.roll(gcum_all, sh, axis=0), 0.0)
        sh *= 2
    zeros = jnp.zeros_like(gab)

    qs, ks, vs, gs, betas = [], [], [], [], []
    for hh in range(GDN_HPS):
        h = hg * GDN_HPS + hh
        sl = slice(hh * HEAD_DIM, (hh + 1) * HEAD_DIM)
        q, k = q2[:, sl], k2[:, sl]
        qs.append(q * lax.rsqrt(jnp.sum(q * q, axis=-1, keepdims=True) + EPS) * (HEAD_DIM ** -0.5))
        ks.append(k * lax.rsqrt(jnp.sum(k * k, axis=-1, keepdims=True) + EPS))
        vs.append(v2[:, sl])
        gs.append(jnp.sum(jnp.where(lane == h, gcum_all, 0.0), axis=1, keepdims=True) + zeros)
        betas.append(jnp.sum(jnp.where(lane == N_HEADS + h, beta_all, 0.0), axis=1, keepdims=True) + zeros)

    outs, states = _gdn_heads(qs, ks, vs, gs, betas, [s_ref[hh] for hh in range(GDN_HPS)])
    for hh in range(GDN_HPS):
        sl = slice(hh * HEAD_DIM, (hh + 1) * HEAD_DIM)
        s_ref[hh] = states[hh]
        on = _rms_rows(outs[hh], gn_ref[...])
        o_ref[:, sl] = (on * _silu(z_ref[:, sl])).astype(o_ref.dtype)


def _gdn(proj, conv_w, hp, gn, batch):
    t = proj.shape[0]
    s = t // batch
    r = GDN_ROWS
    steps = s // r
    hb = r // 8
    wide = GDN_HPS * HEAD_DIM
    cpb = GDN_HPS

    def main(cb):
        return pl.BlockSpec((r, wide), lambda b, h, i: (b * steps + i, cb // cpb + h))

    def halo(cb):
        return pl.BlockSpec((8, wide), lambda b, h, i: (jnp.maximum((b * steps + i) * hb - 1, 0), cb // cpb + h))

    def wspec(off):
        return pl.BlockSpec((GDN_CONV, wide), lambda b, h, i: (0, off // cpb + h))

    return pl.pallas_call(
        _gdn_kernel,
        grid=(batch, N_HEADS // GDN_HPS, steps),
        in_specs=[main(CB_GQ), main(CB_GK), main(CB_GV), halo(CB_GQ), halo(CB_GK), halo(CB_GV),
                  wspec(0), wspec(N_HEADS), wspec(2 * N_HEADS),
                  pl.BlockSpec((r, LANES), lambda b, h, i: (b * steps + i, CB_GAB)),
                  main(CB_GZ),
                  pl.BlockSpec((2, LANES), lambda b, h, i: (0, 0)),
                  pl.BlockSpec((1, HEAD_DIM), lambda b, h, i: (0, 0))],
        out_specs=pl.BlockSpec((r, wide), lambda b, h, i: (b * steps + i, h)),
        out_shape=jax.ShapeDtypeStruct((t, GROUP), BF16),
        scratch_shapes=[pltpu.VMEM((GDN_HPS, HEAD_DIM, HEAD_DIM), F32)],
        compiler_params=_cparams(("parallel", "parallel", "arbitrary")),
        name="gdn",
    )(proj, proj, proj, proj, proj, proj, conv_w, conv_w, conv_w, proj, proj, hp, gn)


SC_ROWS = 512


def _sconv_kernel(b_ref, c_ref, x_ref, ch_ref, xh_ref, w_ref, o_ref, *, steps):
    first = pl.program_id(0) % steps == 0
    r = b_ref.shape[0]
    y = c_ref[...] * x_ref[...]
    yh = ch_ref[...] * xh_ref[...] * jnp.where(first, 0.0, 1.0)
    yb = jnp.concatenate([yh, y], axis=0)
    w = w_ref[...]
    out = None
    for tap in range(SC_CONV):
        sh = SC_CONV - 1 - tap
        ys = yb if sh == 0 else pltpu.roll(yb, sh, axis=0)
        term = ys[8:8 + r] * w[tap:tap + 1, :]
        out = term if out is None else out + term
    o_ref[...] = (b_ref[...] * out).astype(o_ref.dtype)


def _sconv(proj, w, batch):
    t = proj.shape[0]
    r = min(SC_ROWS, t // batch)
    steps = (t // batch) // r
    hb = r // 8

    def main(cb):
        return pl.BlockSpec((r, GROUP), lambda i: (i, cb // 8))

    def halo(cb):
        return pl.BlockSpec((8, GROUP), lambda i: (jnp.maximum(i * hb - 1, 0), cb // 8))

    return pl.pallas_call(
        functools.partial(_sconv_kernel, steps=steps),
        grid=(t // r,),
        in_specs=[main(CB_SCB), main(CB_SCC), main(CB_SCX), halo(CB_SCC), halo(CB_SCX),
                  pl.BlockSpec((SC_CONV, GROUP), lambda i: (0, 0))],
        out_specs=pl.BlockSpec((r, GROUP), lambda i: (i, 0)),
        out_shape=jax.ShapeDtypeStruct((t, GROUP), BF16),
        compiler_params=_cparams(("parallel",)),
        name="sconv",
    )(proj, proj, proj, proj, proj, w)


SWA_ROWS = 512


def _half_rms(x, g2):
    lane = lax.broadcasted_iota(jnp.int32, x.shape, 1)
    lo = lane < SWA_D
    x2 = x * x
    ms_lo = jnp.sum(jnp.where(lo, x2, 0.0), axis=-1, keepdims=True) * (1.0 / SWA_D)
    ms_hi = jnp.sum(jnp.where(lo, 0.0, x2), axis=-1, keepdims=True) * (1.0 / SWA_D)
    rs = jnp.where(lo, lax.rsqrt(ms_lo + EPS), lax.rsqrt(ms_hi + EPS))
    return x * rs * g2


def _swa_kernel(q_ref, k_ref, v_ref, kh_ref, vh_ref, gq_ref, gk_ref, sink_ref, o_ref, *, steps):
    first = pl.program_id(0) % steps == 0
    r = q_ref.shape[0]
    w = SWA_W
    nsub = r // w
    pairs = SWA_Q_HEADS // 2
    lane = lax.broadcasted_iota(jnp.int32, (r + w, LANES), 1)

    kn = _half_rms(jnp.concatenate([kh_ref[...], k_ref[...]], axis=0), gk_ref[...])
    kroll = pltpu.roll(kn, SWA_D, axis=1)
    kdup = (jnp.where(lane < SWA_D, kn, kroll).astype(BF16),
            jnp.where(lane < SWA_D, kroll, kn).astype(BF16))
    vt = jnp.concatenate([vh_ref[...], v_ref[...]], axis=0).T.astype(BF16)

    kr = lax.broadcasted_iota(jnp.int32, (2 * w, 2 * w), 0)
    qc = lax.broadcasted_iota(jnp.int32, (2 * w, 2 * w), 1) % w
    band = jnp.logical_and(kr > qc, kr <= qc + w)
    band0 = jnp.logical_and(band, kr >= jnp.where(first, w, 0))
    qlane = lax.broadcasted_iota(jnp.int32, (w, LANES), 1)

    heads = range(pairs)
    kvh = [c // (pairs // SWA_KV_HEADS) for c in heads]
    for sub in range(nsub):
        mask = band0 if sub == 0 else band
        ks = slice(sub * w, sub * w + 2 * w)
        rows = slice(sub * w, (sub + 1) * w)
        qts = []
        for c in heads:
            qn = _half_rms(q_ref[rows, c * LANES:(c + 1) * LANES], gq_ref[...]) * (SWA_D ** -0.5 * LOG2E)
            qa = jnp.where(qlane < SWA_D, qn, 0.0)
            qb = jnp.where(qlane < SWA_D, 0.0, qn)
            qts.append(jnp.concatenate([qa.T, qb.T], axis=1).astype(BF16))
        ss = [jnp.dot(kdup[kvh[c]][ks], qts[c], preferred_element_type=F32) for c in heads]
        pns = []
        for c in heads:
            s = jnp.where(mask, ss[c], NEG)
            sink = sink_ref[c:c + 1, :] * LOG2E
            m = jnp.maximum(jnp.max(s, axis=0, keepdims=True), sink)
            p = jnp.exp2(s - m)
            l = jnp.sum(p, axis=0, keepdims=True) + jnp.exp2(sink - m)
            pns.append((p * (1.0 / l)).astype(BF16))
        ots = [jnp.dot(vt[kvh[c] * SWA_D:(kvh[c] + 1) * SWA_D, ks], pns[c], preferred_element_type=F32)
               for c in heads]
        for c in heads:
            o = jnp.concatenate([ots[c][:, :w], ots[c][:, w:]], axis=0).T
            o_ref[rows, c * LANES:(c + 1) * LANES] = o.astype(o_ref.dtype)


def _swa(proj, gq2, gk2, sinkrow, batch):
    t = proj.shape[0]
    r = min(SWA_ROWS, t // batch)
    steps = (t // batch) // r
    hb = r // SWA_W

    def halo(cb):
        return pl.BlockSpec((SWA_W, LANES), lambda i: (jnp.maximum(i * hb - 1, 0), cb))

    return pl.pallas_call(
        functools.partial(_swa_kernel, steps=steps),
        grid=(t // r,),
        in_specs=[pl.BlockSpec((r, GROUP), lambda i: (i, CB_SQ // 8)),
                  pl.BlockSpec((r, LANES), lambda i: (i, CB_SK)),
                  pl.BlockSpec((r, LANES), lambda i: (i, CB_SV)),
                  halo(CB_SK), halo(CB_SV),
                  pl.BlockSpec((1, LANES), lambda i: (0, 0)),
                  pl.BlockSpec((1, LANES), lambda i: (0, 0)),
                  pl.BlockSpec((SWA_Q_HEADS // 2, 2 * SWA_W), lambda i: (0, 0))],
        out_specs=pl.BlockSpec((r, GROUP), lambda i: (i, 0)),
        out_shape=jax.ShapeDtypeStruct((t, GROUP), BF16),
        compiler_params=_cparams(("parallel",)),
        name="swa",
    )(proj, proj, proj, proj, proj, gq2, gk2, sinkrow)


def _regroup_w_in(w_in, gain):
    wt = jnp.swapaxes(w_in, 1, 2) * gain[:, None, :]
    a1 = A_COLS + 2 * N_HEADS
    pad = jnp.zeros((wt.shape[0], NP_COLS - wt.shape[1], wt.shape[2]), wt.dtype)
    return jnp.concatenate([wt[:, :A_COLS], wt[:, a1:], wt[:, A_COLS:a1], pad], axis=1).astype(BF16)


def _layer(x, xb, batch, layer, w_in, moba_q_norm, moba_k_norm, gdn_conv, gdn_a_log, gdn_dt_bias,
           gdn_out_norm, sc_conv, swa_q_norm, swa_k_norm, swa_sinks, w_out, ffn_gain, w_gate, w_up, w_down):
    row = lambda a: a.reshape(1, -1).astype(F32)

    proj = _in_proj(xb, w_in, layer, tm=2048, tn=512)

    kn, km, vt = _moba_prep(proj, row(moba_k_norm))
    o_a = _moba(proj, kn, km, vt, row(moba_q_norm), batch)

    hp = jnp.pad(jnp.stack([gdn_a_log, gdn_dt_bias]).astype(F32), ((0, 0), (0, LANES - N_HEADS)))
    o_b = _gdn(proj, gdn_conv.astype(F32), hp, row(gdn_out_norm), batch)

    o_c = _sconv(proj, sc_conv.astype(F32), batch)

    sinkrow = jnp.repeat(swa_sinks.astype(F32), SWA_W).reshape(SWA_Q_HEADS // 2, 2 * SWA_W)
    o_d = _swa(proj, row(jnp.tile(swa_q_norm, 2)), row(jnp.tile(swa_k_norm, 2)), sinkrow, batch)

    x, xb = _out_proj((o_a, o_b, o_c, o_d), w_out, x, layer, tm=1024, tn=512)

    act = _ffn_up(xb, ffn_gain, w_gate, w_up, layer, D_FF_PAD, tm=1024, tn=512, tn_first=256)
    return _mm_res(act, w_down, x, layer, tm=1024, tn=1024, tk=D_FF_PAD // 4, tn_first=512)


def kernel(x, norm_mix, w_in, moba_q_norm, moba_k_norm, gdn_conv, gdn_a_log, gdn_dt_bias, gdn_out_norm, sc_conv, swa_q_norm, swa_k_norm, swa_sinks, w_out, norm_ffn, w_gate, w_up, w_down):
    batch, seq, d = x.shape
    w_in_b = _regroup_w_in(w_in, norm_mix)
    w_out_b = w_out.astype(BF16)
    ffn_gain = norm_ffn.astype(F32)[:, :, None]
    h = x.reshape(batch * seq, d)
    hb = h.astype(BF16)
    for l in range(norm_mix.shape[0]):
        h, hb = _layer(h, hb, batch, l, w_in_b, moba_q_norm[l], moba_k_norm[l], gdn_conv[l], gdn_a_log[l],
                       gdn_dt_bias[l], gdn_out_norm[l], sc_conv[l], swa_q_norm[l], swa_k_norm[l], swa_sinks[l],
                       w_out_b, ffn_gain, w_gate, w_up, w_down)
    return h.reshape(batch, seq, d)
```

```python
import functools

import jax
import jax.numpy as jnp
from jax import lax
from jax.experimental import pallas as pl
from jax.experimental.pallas import tpu as pltpu

F32 = jnp.float32
BF16 = jnp.bfloat16

EPS = 1e-6
LANES = 128
GROUP = 1024
HEAD_DIM = 128
N_HEADS = GROUP // HEAD_DIM
MOBA_BLOCK = 256
MOBA_TILE = 2 * MOBA_BLOCK
MOBA_TOPK = 3
GDN_CONV = 4
GDN_CHUNK = 64
SC_CONV = 3
SWA_D = 64
SWA_Q_HEADS = GROUP // SWA_D
SWA_KV_HEADS = 2
SWA_W = 128
NEG = -1e30
LOG2E = 1.4426950408889634
BIAS_ROWS = 16

CB_MQ, CB_MK, CB_MV = 0, 8, 16
CB_GQ, CB_GK, CB_GV, CB_GZ = 24, 32, 40, 48
CB_SCB, CB_SCC, CB_SCX = 56, 64, 72
CB_SQ, CB_SK, CB_SV, CB_GAB = 80, 88, 89, 90
NP_COLS = 92 * LANES
A_COLS = 6 * GROUP
D_FF_PAD = 11264

VMEM_LIMIT = 56 * 1024 * 1024


def _cparams(sem, vmem=VMEM_LIMIT):
    return pltpu.CompilerParams(dimension_semantics=sem, vmem_limit_bytes=vmem)


def _bdot(a, b):
    return jnp.dot(a.astype(BF16), b.astype(BF16), preferred_element_type=F32)


def _bdot_nt(a, b):
    return lax.dot_general(a.astype(BF16), b.astype(BF16), (((1,), (1,)), ((), ())),
                           preferred_element_type=F32)


def _sigmoid(x):
    return 1.0 / (1.0 + jnp.exp(-x))


def _silu(x):
    return x * _sigmoid(x)


def _rms_rows(x, g):
    ms = jnp.mean(x * x, axis=-1, keepdims=True)
    return x * lax.rsqrt(ms + EPS) * g


NORM_CHUNK = 256


def _row_scale_to(xb_ref, rs_ref):
    chunk = min(NORM_CHUNK, xb_ref.shape[0])

    def body(c, carry):
        rows = pl.ds(pl.multiple_of(c * chunk, chunk), chunk)
        x = xb_ref[rows, :].astype(F32)
        ms = jnp.mean(x * x, axis=-1, keepdims=True)
        rs_ref[rows, :] = jnp.broadcast_to(lax.rsqrt(ms + EPS), (chunk, LANES))
        return carry

    lax.fori_loop(0, xb_ref.shape[0] // chunk, body, 0)


def _scaled_dot(xb_ref, w_ref, rs_ref, w_transposed=False):
    dims = (((1,), (1,)), ((), ())) if w_transposed else (((1,), (0,)), ((), ()))
    acc = lax.dot_general(xb_ref[...], w_ref[...], dims, preferred_element_type=F32)
    rs = rs_ref[...]
    return jnp.concatenate([acc[:, c * LANES:(c + 1) * LANES] * rs for c in range(acc.shape[1] // LANES)], axis=1)


def _in_proj_kernel(xb_ref, w_ref, o_ref, rs_ref):
    @pl.when(pl.program_id(1) == 0)
    def _():
        _row_scale_to(xb_ref, rs_ref)

    o_ref[...] = _scaled_dot(xb_ref, w_ref, rs_ref, w_transposed=True).astype(o_ref.dtype)


def _in_proj(xb, w, layer, tm, tn):
    t, d = xb.shape
    n = w.shape[1]
    tm = min(tm, t)
    return pl.pallas_call(
        _in_proj_kernel,
        grid=(t // tm, n // tn),
        in_specs=[pl.BlockSpec((tm, d), lambda i, j: (i, 0)),
                  pl.BlockSpec((None, tn, d), lambda i, j: (layer, j, 0))],
        out_specs=pl.BlockSpec((tm, tn), lambda i, j: (i, j)),
        out_shape=jax.ShapeDtypeStruct((t, n), F32),
        scratch_shapes=[pltpu.VMEM((tm, LANES), F32)],
        compiler_params=_cparams(("parallel", "arbitrary")),
        name="in_proj",
    )(xb, w)


FIRST_ROWS = 1024


def _ffn_up_kernel(xb_ref, wg_ref, wu_ref, o_ref, rs_ref):
    @pl.when(pl.program_id(1) == 0)
    def _():
        _row_scale_to(xb_ref, rs_ref)

    a = _scaled_dot(xb_ref, wg_ref, rs_ref)
    b = _scaled_dot(xb_ref, wu_ref, rs_ref)
    o_ref[...] = (_silu(a) * b).astype(o_ref.dtype)


def _ffn_up_first_kernel(xb_ref, g_ref, wg32_ref, wu32_ref, o_ref, wgb_ref, wub_ref, rs_ref, *, valid):
    j = pl.program_id(0)

    @pl.when(j == 0)
    def _():
        _row_scale_to(xb_ref, rs_ref)

    col = j * wg32_ref.shape[1] + lax.broadcasted_iota(jnp.int32, wg32_ref.shape, 1)
    wgb_ref[...] = jnp.where(col < valid, wg32_ref[...] * g_ref[...], 0.0).astype(BF16)
    wub_ref[...] = jnp.where(col < valid, wu32_ref[...] * g_ref[...], 0.0).astype(BF16)
    a = _scaled_dot(xb_ref, wgb_ref, rs_ref)
    b = _scaled_dot(xb_ref, wub_ref, rs_ref)
    o_ref[...] = (_silu(a) * b).astype(o_ref.dtype)


def _ffn_up_rest_kernel(xb_ref, wg_ref, wu_ref, dst_ref, o_ref, rs_ref):
    del dst_ref
    _ffn_up_kernel(xb_ref, wg_ref, wu_ref, o_ref, rs_ref)


def _ffn_up(xb, gain, w_gate, w_up, layer, n_pad, tm, tn, tn_first):
    t, d = xb.shape
    n_real = w_gate.shape[2]
    tf = min(FIRST_ROWS, t)
    fi = t // tf - 1
    last_blk = (n_real - 1) // tn_first
    w32_spec = pl.BlockSpec((None, d, tn_first), lambda j: (layer, 0, jnp.minimum(j, last_blk)))
    wb_spec = pl.BlockSpec((d, tn_first), lambda j: (0, j))
    act, wgb, wub = pl.pallas_call(
        functools.partial(_ffn_up_first_kernel, valid=n_real),
        grid=(n_pad // tn_first,),
        in_specs=[pl.BlockSpec((tf, d), lambda j: (fi, 0)),
                  pl.BlockSpec((None, d, 1), lambda j: (layer, 0, 0)),
                  w32_spec, w32_spec],
        out_specs=[pl.BlockSpec((tf, tn_first), lambda j: (fi, j)), wb_spec, wb_spec],
        out_shape=[jax.ShapeDtypeStruct((t, n_pad), BF16),
                   jax.ShapeDtypeStruct((d, n_pad), BF16), jax.ShapeDtypeStruct((d, n_pad), BF16)],
        scratch_shapes=[pltpu.VMEM((tf, LANES), F32)],
        compiler_params=_cparams(("arbitrary",)),
        name="ffn_up_first",
    )(xb, gain, w_gate, w_up)
    if t == tf:
        return act
    w_spec = pl.BlockSpec((d, tn), lambda i, j: (0, j))
    return pl.pallas_call(
        _ffn_up_rest_kernel,
        grid=((t - tf) // tm, n_pad // tn),
        in_specs=[pl.BlockSpec((tm, d), lambda i, j: (i, 0)), w_spec, w_spec,
                  pl.BlockSpec(memory_space=pl.ANY)],
        out_specs=pl.BlockSpec((tm, tn), lambda i, j: (i, j)),
        out_shape=jax.ShapeDtypeStruct((t, n_pad), BF16),
        scratch_shapes=[pltpu.VMEM((tm, LANES), F32)],
        input_output_aliases={3: 0},
        compiler_params=_cparams(("parallel", "arbitrary")),
        name="ffn_up",
    )(xb, wgb, wub, act)


def _mm_res_kernel(a_ref, w_ref, r_ref, o_ref, ob_ref):
    k = pl.program_id(2)

    @pl.when(k == 0)
    def _():
        o_ref[...] = r_ref[...] + jnp.dot(a_ref[...], w_ref[...], preferred_element_type=F32)

    @pl.when(k > 0)
    def _():
        o_ref[...] += jnp.dot(a_ref[...], w_ref[...], preferred_element_type=F32)

    @pl.when(k == pl.num_programs(2) - 1)
    def _():
        ob_ref[...] = o_ref[...].astype(BF16)


def _mm_res_first_kernel(a_ref, w32_ref, r_ref, o_ref, ob_ref, wb_ref, *, valid):
    k = pl.program_id(1)
    row = k * w32_ref.shape[0] + lax.broadcasted_iota(jnp.int32, w32_ref.shape, 0)
    wb_ref[...] = jnp.where(row < valid, w32_ref[...], 0.0).astype(BF16)

    @pl.when(k == 0)
    def _():
        o_ref[...] = r_ref[...] + jnp.dot(a_ref[...], wb_ref[...], preferred_element_type=F32)

    @pl.when(k > 0)
    def _():
        o_ref[...] += jnp.dot(a_ref[...], wb_ref[...], preferred_element_type=F32)

    @pl.when(k == pl.num_programs(1) - 1)
    def _():
        ob_ref[...] = o_ref[...].astype(BF16)


def _mm_res_rest_kernel(a_ref, w_ref, r_ref, dst_ref, dstb_ref, o_ref, ob_ref):
    del dst_ref, dstb_ref
    _mm_res_kernel(a_ref, w_ref, r_ref, o_ref, ob_ref)


def _mm_res(a, w, r, layer, tm, tn, tk, tn_first):
    t, kd = a.shape
    k_real, n = w.shape[1], w.shape[2]
    tf = min(FIRST_ROWS, t)
    fi = t // tf - 1
    last_blk = (k_real - 1) // tk
    of_spec = pl.BlockSpec((tf, tn_first), lambda j, k: (fi, j))
    x, xb, wb = pl.pallas_call(
        functools.partial(_mm_res_first_kernel, valid=k_real),
        grid=(n // tn_first, kd // tk),
        in_specs=[pl.BlockSpec((tf, tk), lambda j, k: (fi, k)),
                  pl.BlockSpec((None, tk, tn_first), lambda j, k: (layer, jnp.minimum(k, last_blk), j)),
                  of_spec],
        out_specs=[of_spec, of_spec, pl.BlockSpec((tk, tn_first), lambda j, k: (k, j))],
        out_shape=[jax.ShapeDtypeStruct((t, n), F32), jax.ShapeDtypeStruct((t, n), BF16),
                   jax.ShapeDtypeStruct((kd, n), BF16)],
        compiler_params=_cparams(("parallel", "arbitrary")),
        name="mm_res_first",
    )(a, w, r)
    if t == tf:
        return x, xb
    o_spec = pl.BlockSpec((tm, tn), lambda i, j, k: (i, j))
    any_spec = pl.BlockSpec(memory_space=pl.ANY)
    return pl.pallas_call(
        _mm_res_rest_kernel,
        grid=((t - tf) // tm, n // tn, kd // tk),
        in_specs=[pl.BlockSpec((tm, tk), lambda i, j, k: (i, k)),
                  pl.BlockSpec((tk, tn), lambda i, j, k: (k, j)),
                  o_spec, any_spec, any_spec],
        out_specs=[o_spec, o_spec],
        out_shape=[jax.ShapeDtypeStruct((t, n), F32), jax.ShapeDtypeStruct((t, n), BF16)],
        input_output_aliases={3: 0, 4: 1},
        compiler_params=_cparams(("parallel", "parallel", "arbitrary")),
        name="mm_res",
    )(a, wb, r, x, xb)


def _out_proj_kernel(a0_ref, a1_ref, a2_ref, a3_ref, w_ref, r_ref, o_ref, ob_ref):
    acc = r_ref[...]
    for g, a_ref in enumerate((a0_ref, a1_ref, a2_ref, a3_ref)):
        acc = acc + jnp.dot(a_ref[...], w_ref[g * GROUP:(g + 1) * GROUP, :], preferred_element_type=F32)
    o_ref[...] = acc
    ob_ref[...] = acc.astype(BF16)


def _out_proj(mix, w, r, layer, tm, tn):
    t = r.shape[0]
    n = w.shape[2]
    tm = min(tm, t)
    a_spec = pl.BlockSpec((tm, GROUP), lambda i, j: (i, 0))
    o_spec = pl.BlockSpec((tm, tn), lambda i, j: (i, j))
    return pl.pallas_call(
        _out_proj_kernel,
        grid=(t // tm, n // tn),
        in_specs=[a_spec, a_spec, a_spec, a_spec,
                  pl.BlockSpec((None, 4 * GROUP, tn), lambda i, j: (layer, 0, j)),
                  o_spec],
        out_specs=[o_spec, o_spec],
        out_shape=[jax.ShapeDtypeStruct((t, n), F32), jax.ShapeDtypeStruct((t, n), BF16)],
        compiler_params=_cparams(("parallel", "arbitrary")),
        name="out_proj",
    )(*mix, w, r)


def _moba_prep_kernel(k_ref, v_ref, g_ref, kn_ref, km_ref, vt_ref):
    g = g_ref[...]
    for h in range(N_HEADS):
        sl = slice(h * HEAD_DIM, (h + 1) * HEAD_DIM)
        kn = _rms_rows(k_ref[:, sl], g)
        kn_ref[:, sl] = kn.astype(BF16)
        for half in range(MOBA_TILE // MOBA_BLOCK):
            km_ref[half, :, sl] = jnp.mean(kn[half * MOBA_BLOCK:(half + 1) * MOBA_BLOCK], axis=0, keepdims=True)
        vt_ref[0, sl, :] = v_ref[:, sl].T.astype(BF16)


def _moba_prep(proj, gk):
    t = proj.shape[0]
    ntile = t // MOBA_TILE
    per = MOBA_TILE // MOBA_BLOCK
    return pl.pallas_call(
        _moba_prep_kernel,
        grid=(ntile,),
        in_specs=[pl.BlockSpec((MOBA_TILE, GROUP), lambda i: (i, CB_MK // 8)),
                  pl.BlockSpec((MOBA_TILE, GROUP), lambda i: (i, CB_MV // 8)),
                  pl.BlockSpec((1, HEAD_DIM), lambda i: (0, 0))],
        out_specs=[pl.BlockSpec((MOBA_TILE, GROUP), lambda i: (i, 0)),
                   pl.BlockSpec((per, 1, GROUP), lambda i: (i, 0, 0)),
                   pl.BlockSpec((1, GROUP, MOBA_TILE), lambda i: (i, 0, 0))],
        out_shape=[jax.ShapeDtypeStruct((t, GROUP), BF16),
                   jax.ShapeDtypeStruct((ntile * per, 1, GROUP), F32),
                   jax.ShapeDtypeStruct((ntile, GROUP, MOBA_TILE), BF16)],
        compiler_params=_cparams(("parallel",)),
        name="moba_prep",
    )(proj, proj, gk)


def _moba_kernel(q_ref, k_ref, vt_ref, km_ref, g_ref, o_ref, bias_ref, acc_ref, qaug_ref, s0_ref, s1_ref,
                 p0_ref, p1_ref, *, nb):
    ti = pl.program_id(2)
    blk, tile = MOBA_BLOCK, MOBA_TILE
    qn = _rms_rows(q_ref[...], g_ref[...])

    gate = lax.dot_general(km_ref[0], qn, (((1,), (1,)), ((), ())),
                           precision=lax.Precision.HIGHEST, preferred_element_type=F32)
    row = lax.broadcasted_iota(jnp.int32, gate.shape, 0)
    own = 2 * ti + (lax.broadcasted_iota(jnp.int32, gate.shape, 1) >= blk).astype(jnp.int32)
    rowf = row.astype(F32)
    gate = jnp.where(row < own, gate, -jnp.inf)
    bias = jnp.where(row == own, 0.0, NEG)
    for _ in range(MOBA_TOPK):
        m = jnp.max(gate, axis=0, keepdims=True)
        idx = jnp.min(jnp.where(gate == m, rowf, float(nb)), axis=0, keepdims=True)
        hit = jnp.logical_and(rowf == idx, m > -jnp.inf)
        bias = jnp.where(hit, 0.0, bias)
        gate = jnp.where(hit, -jnp.inf, gate)
    bias_ref[...] = bias

    qaug_ref[0:HEAD_DIM, :] = (qn * (HEAD_DIM ** -0.5 * LOG2E)).T.astype(BF16)
    qaug_ref[HEAD_DIM + BIAS_ROWS:, :] = jnp.zeros((HEAD_DIM - BIAS_ROWS, tile), BF16)
    er = lax.broadcasted_iota(jnp.int32, (tile, HEAD_DIM), 0)
    ec = lax.broadcasted_iota(jnp.int32, (tile, HEAD_DIM), 1)
    onehot = jnp.where(ec == er // blk, 1.0, 0.0).astype(BF16)
    brow = lax.broadcasted_iota(jnp.int32, (BIAS_ROWS, tile), 0)
    ones_rows = jnp.ones((BIAS_ROWS, tile), BF16)

    def scores(t, valid):
        b0 = jnp.where(valid, bias_ref[pl.ds(2 * t, 1), :], NEG)
        b1 = jnp.where(valid, bias_ref[pl.ds(2 * t + 1, 1), :], NEG)
        qaug_ref[HEAD_DIM:HEAD_DIM + BIAS_ROWS, :] = jnp.where(
            brow == 0, b0, jnp.where(brow == 1, b1, 0.0)).astype(BF16)
        kt = k_ref[pl.ds(pl.multiple_of(t * tile, tile), tile), :]
        return jnp.dot(jnp.concatenate([kt, onehot], axis=1), qaug_ref[...],
                       preferred_element_type=F32)

    s_refs, p_refs = (s0_ref, s1_ref), (p0_ref, p1_ref)
    kr = lax.broadcasted_iota(jnp.int32, (tile, tile), 0)
    qc = lax.broadcasted_iota(jnp.int32, (tile, tile), 1)
    s_refs[0][...] = jnp.where(kr <= qc, scores(ti, True), NEG)
    p_refs[1][...] = jnp.zeros((tile, tile), BF16)
    acc_ref[...] = jnp.zeros_like(acc_ref)

    def trip(n, par, carry):
        m, alpha_prev = carry
        tc = jnp.where(n == 1, ti, jnp.clip(n - 2, 0, ti))
        pv = jnp.dot(jnp.concatenate([vt_ref[tc], ones_rows], axis=0), p_refs[1 - par][...],
                     preferred_element_type=F32)
        s_refs[1 - par][...] = scores(jnp.minimum(n, ti), n < ti)
        s = s_refs[par][...]
        m_new = jnp.maximum(m, jnp.max(s, axis=0, keepdims=True))
        alpha = jnp.exp2(m - m_new)
        p_refs[par][...] = jnp.exp2(s - m_new).astype(BF16)
        acc_ref[...] = acc_ref[...] * alpha_prev + pv
        return m_new, alpha

    def body(j, carry):
        return trip(2 * j + 1, 1, trip(2 * j, 0, carry))

    init = (jnp.full((1, tile), NEG, F32), jnp.ones((1, tile), F32))
    lax.fori_loop(0, (ti + 3) // 2, body, init)
    o_ref[...] = (acc_ref[0:HEAD_DIM, :] / acc_ref[HEAD_DIM:HEAD_DIM + 1, :]).T.astype(o_ref.dtype)


def _moba(proj, kn, km, vt, gq, batch):
    t = proj.shape[0]
    s = t // batch
    nb = s // MOBA_BLOCK
    nt = s // MOBA_TILE
    km = km.reshape(batch, nb, GROUP)
    return pl.pallas_call(
        functools.partial(_moba_kernel, nb=nb),
        grid=(batch, N_HEADS, nt),
        in_specs=[pl.BlockSpec((MOBA_TILE, HEAD_DIM), lambda b, h, i: (b * nt + i, CB_MQ + h)),
                  pl.BlockSpec((s, HEAD_DIM), lambda b, h, i: (b, h)),
                  pl.BlockSpec((nt, HEAD_DIM, MOBA_TILE), lambda b, h, i: (b, h, 0)),
                  pl.BlockSpec((1, nb, HEAD_DIM), lambda b, h, i: (b, 0, h)),
                  pl.BlockSpec((1, HEAD_DIM), lambda b, h, i: (0, 0))],
        out_specs=pl.BlockSpec((MOBA_TILE, HEAD_DIM), lambda b, h, i: (b * nt + i, h)),
        out_shape=jax.ShapeDtypeStruct((t, GROUP), BF16),
        scratch_shapes=[pltpu.VMEM((nb, MOBA_TILE), F32),
                        pltpu.VMEM((HEAD_DIM + BIAS_ROWS, MOBA_TILE), F32),
                        pltpu.VMEM((2 * HEAD_DIM, MOBA_TILE), BF16),
                        pltpu.VMEM((MOBA_TILE, MOBA_TILE), F32), pltpu.VMEM((MOBA_TILE, MOBA_TILE), F32),
                        pltpu.VMEM((MOBA_TILE, MOBA_TILE), BF16), pltpu.VMEM((MOBA_TILE, MOBA_TILE), BF16)],
        compiler_params=_cparams(("parallel", "parallel", "arbitrary")),
        name="moba",
    )(proj, kn, vt, km, gq)


GDN_ROWS = 256
GDN_HPS = 8


def _conv_silu(x_ref, halo_ref, w_ref, first):
    r = x_ref.shape[0]
    halo = halo_ref[...] * jnp.where(first, 0.0, 1.0)
    xb = jnp.concatenate([halo, x_ref[...]], axis=0)
    w = w_ref[...]
    out = None
    for tap in range(GDN_CONV):
        sh = GDN_CONV - 1 - tap
        xs = xb if sh == 0 else pltpu.roll(xb, sh, axis=0)
        term = xs[8:8 + r] * w[tap:tap + 1, :]
        out = term if out is None else out + term
    return _silu(out)


def _gdn_heads(qs, ks, vs, gs, betas, states):
    r = qs[0].shape[0]
    c = GDN_CHUNK
    pair = 2 * c
    npair = r // pair
    nh = len(qs)
    row = lax.broadcasted_iota(jnp.int32, (pair, pair), 0)
    col = lax.broadcasted_iota(jnp.int32, (pair, pair), 1)
    same = (row // c) == (col // c)
    tril = jnp.logical_and(same, row >= col)
    strict = jnp.logical_and(same, row > col)
    eye = (row == col).astype(F32)
    rin = row % c
    units = [(h, pi) for pi in range(npair) for h in range(nh)]

    def rows(x, u):
        return x[u[0]][u[1] * pair:(u[1] + 1) * pair]

    gcum = {u: rows(gs, u) for u in units}
    decay = {u: jnp.exp(jnp.where(tril, gcum[u] - gcum[u].T, -jnp.inf)) for u in units}
    eg = {u: jnp.exp(gcum[u]) for u in units}
    g_end = {u: (gcum[u][c - 1:c, :], gcum[u][pair - 1:pair, :]) for u in units}
    kb = {u: rows(ks, u) * rows(betas, u) for u in units}
    vb = {u: rows(vs, u) * rows(betas, u) for u in units}
    lmat = {u: jnp.where(strict, _bdot_nt(kb[u], rows(ks, u)) * decay[u], 0.0) for u in units}
    qk = {u: _bdot_nt(rows(qs, u), rows(ks, u)) * decay[u] for u in units}
    tinv = {u: eye - lmat[u] for u in units}
    lpow = lmat
    span = 1
    while 2 * span < c:
        lpow = {u: _bdot(lpow[u], lpow[u]) for u in units}
        tinv = {u: tinv[u] + _bdot(tinv[u], lpow[u]) for u in units}
        span *= 2
    uw = {u: _bdot(tinv[u], jnp.concatenate([vb[u], kb[u] * eg[u]], axis=1)) for u in units}
    qd = {u: rows(qs, u) * eg[u] for u in units}
    kdt = {u: (rows(ks, u) * jnp.exp(jnp.where(row < c, g_end[u][0], g_end[u][1]) - gcum[u])).T for u in units}

    states = list(states)
    vns = {u: [] for u in units}
    o_st = {u: [] for u in units}
    for pi in range(npair):
        for ci in range(2):
            cs = slice(ci * c, (ci + 1) * c)
            for h in range(nh):
                u = (h, pi)
                ws = _bdot(jnp.concatenate([uw[u][cs, HEAD_DIM:], qd[u][cs]], axis=0), states[h])
                vn = uw[u][cs, :HEAD_DIM] - ws[:c]
                o_st[u].append(ws[c:])
                vns[u].append(vn)
                zero = jnp.zeros_like(vn)
                vn_pad = jnp.concatenate([vn, zero] if ci == 0 else [zero, vn], axis=0)
                states[h] = states[h] * jnp.exp(g_end[u][ci]) + _bdot(kdt[u], vn_pad)
    outs = []
    for h in range(nh):
        parts = [jnp.concatenate(o_st[(h, pi)], axis=0) + _bdot(qk[(h, pi)], jnp.concatenate(vns[(h, pi)], axis=0))
                 for pi in range(npair)]
        outs.append(jnp.concatenate(parts, axis=0))
    return outs, states


def _gdn_kernel(q_ref, k_ref, v_ref, qh_ref, kh_ref, vh_ref, wq_ref, wk_ref, wv_ref,
                gab_ref, z_ref, hp_ref, gn_ref, o_ref, s_ref):
    hg = pl.program_id(1)
    first = pl.program_id(2) == 0

    @pl.when(first)
    def _():
        s_ref[...] = jnp.zeros_like(s_ref)

    q2 = _conv_silu(q_ref, qh_ref, wq_ref, first)
    k2 = _conv_silu(k_ref, kh_ref, wk_ref, first)
    v2 = _conv_silu(v_ref, vh_ref, wv_ref, first)
    gab = gab_ref[...]
    lane = lax.broadcasted_iota(jnp.int32, gab.shape, 1)
    xg = gab + hp_ref[1:2, :]
    softplus = jnp.maximum(xg, 0.0) + jnp.log1p(jnp.exp(-jnp.abs(xg)))
    gcum_all = -jnp.exp(hp_ref[0:1, :]) * softplus
    beta_all = _sigmoid(gab)
    rin = lax.broadcasted_iota(jnp.int32, gab.shape, 0) % GDN_CHUNK
    sh = 1
    while sh < GDN_CHUNK:
        gcum_all = gcum_all + jnp.where(rin >= sh, pltpu.roll(gcum_all, sh, axis=0), 0.0)
        sh *= 2
    zeros = jnp.zeros_like(gab)

    qs, ks, vs, gs, betas = [], [], [], [], []
    for hh in range(GDN_HPS):
        h = hg * GDN_HPS + hh
        sl = slice(hh * HEAD_DIM, (hh + 1) * HEAD_DIM)
        q, k = q2[:, sl], k2[:, sl]
        qs.append(q * lax.rsqrt(jnp.sum(q * q, axis=-1, keepdims=True) + EPS) * (HEAD_DIM ** -0.5))
        ks.append(k * lax.rsqrt(jnp.sum(k * k, axis=-1, keepdims=True) + EPS))
        vs.append(v2[:, sl])
        gs.append(jnp.sum(jnp.where(lane == h, gcum_all, 0.0), axis=1, keepdims=True) + zeros)
        betas.append(jnp.sum(jnp.where(lane == N_HEADS + h, beta_all, 0.0), axis=1, keepdims=True) + zeros)

    outs, states = _gdn_heads(qs, ks, vs, gs, betas, [s_ref[hh] for hh in range(GDN_HPS)])
    for hh in range(GDN_HPS):
        sl = slice(hh * HEAD_DIM, (hh + 1) * HEAD_DIM)
        s_ref[hh] = states[hh]
        on = _rms_rows(outs[hh], gn_ref[...])
        o_ref[:, sl] = (on * _silu(z_ref[:, sl])).astype(o_ref.dtype)


def _gdn(proj, conv_w, hp, gn, batch):
    t = proj.shape[0]
    s = t // batch
    r = GDN_ROWS
    steps = s // r
    hb = r // 8
    wide = GDN_HPS * HEAD_DIM
    cpb = GDN_HPS

    def main(cb):
        return pl.BlockSpec((r, wide), lambda b, h, i: (b * steps + i, cb // cpb + h))

    def halo(cb):
        return pl.BlockSpec((8, wide), lambda b, h, i: (jnp.maximum((b * steps + i) * hb - 1, 0), cb // cpb + h))

    def wspec(off):
        return pl.BlockSpec((GDN_CONV, wide), lambda b, h, i: (0, off // cpb + h))

    return pl.pallas_call(
        _gdn_kernel,
        grid=(batch, N_HEADS // GDN_HPS, steps),
        in_specs=[main(CB_GQ), main(CB_GK), main(CB_GV), halo(CB_GQ), halo(CB_GK), halo(CB_GV),
                  wspec(0), wspec(N_HEADS), wspec(2 * N_HEADS),
                  pl.BlockSpec((r, LANES), lambda b, h, i: (b * steps + i, CB_GAB)),
                  main(CB_GZ),
                  pl.BlockSpec((2, LANES), lambda b, h, i: (0, 0)),
                  pl.BlockSpec((1, HEAD_DIM), lambda b, h, i: (0, 0))],
        out_specs=pl.BlockSpec((r, wide), lambda b, h, i: (b * steps + i, h)),
        out_shape=jax.ShapeDtypeStruct((t, GROUP), BF16),
        scratch_shapes=[pltpu.VMEM((GDN_HPS, HEAD_DIM, HEAD_DIM), F32)],
        compiler_params=_cparams(("parallel", "parallel", "arbitrary")),
        name="gdn",
    )(proj, proj, proj, proj, proj, proj, conv_w, conv_w, conv_w, proj, proj, hp, gn)


SC_ROWS = 512


def _sconv_kernel(b_ref, c_ref, x_ref, ch_ref, xh_ref, w_ref, o_ref, *, steps):
    first = pl.program_id(0) % steps == 0
    r = b_ref.shape[0]
    y = c_ref[...] * x_ref[...]
    yh = ch_ref[...] * xh_ref[...] * jnp.where(first, 0.0, 1.0)
    yb = jnp.concatenate([yh, y], axis=0)
    w = w_ref[...]
    out = None
    for tap in range(SC_CONV):
        sh = SC_CONV - 1 - tap
        ys = yb if sh == 0 else pltpu.roll(yb, sh, axis=0)
        term = ys[8:8 + r] * w[tap:tap + 1, :]
        out = term if out is None else out + term
    o_ref[...] = (b_ref[...] * out).astype(o_ref.dtype)


def _sconv(proj, w, batch):
    t = proj.shape[0]
    r = min(SC_ROWS, t // batch)
    steps = (t // batch) // r
    hb = r // 8

    def main(cb):
        return pl.BlockSpec((r, GROUP), lambda i: (i, cb // 8))

    def halo(cb):
        return pl.BlockSpec((8, GROUP), lambda i: (jnp.maximum(i * hb - 1, 0), cb // 8))

    return pl.pallas_call(
        functools.partial(_sconv_kernel, steps=steps),
        grid=(t // r,),
        in_specs=[main(CB_SCB), main(CB_SCC), main(CB_SCX), halo(CB_SCC), halo(CB_SCX),
                  pl.BlockSpec((SC_CONV, GROUP), lambda i: (0, 0))],
        out_specs=pl.BlockSpec((r, GROUP), lambda i: (i, 0)),
        out_shape=jax.ShapeDtypeStruct((t, GROUP), BF16),
        compiler_params=_cparams(("parallel",)),
        name="sconv",
    )(proj, proj, proj, proj, proj, w)


SWA_ROWS = 512


def _half_rms(x, g2):
    lane = lax.broadcasted_iota(jnp.int32, x.shape, 1)
    lo = lane < SWA_D
    x2 = x * x
    ms_lo = jnp.sum(jnp.where(lo, x2, 0.0), axis=-1, keepdims=True) * (1.0 / SWA_D)
    ms_hi = jnp.sum(jnp.where(lo, 0.0, x2), axis=-1, keepdims=True) * (1.0 / SWA_D)
    rs = jnp.where(lo, lax.rsqrt(ms_lo + EPS), lax.rsqrt(ms_hi + EPS))
    return x * rs * g2


def _swa_kernel(q_ref, k_ref, v_ref, kh_ref, vh_ref, gq_ref, gk_ref, sink_ref, o_ref, *, steps):
    first = pl.program_id(0) % steps == 0
    r = q_ref.shape[0]
    w = SWA_W
    nsub = r // w
    pairs = SWA_Q_HEADS // 2
    lane = lax.broadcasted_iota(jnp.int32, (r + w, LANES), 1)

    kn = _half_rms(jnp.concatenate([kh_ref[...], k_ref[...]], axis=0), gk_ref[...])
    kroll = pltpu.roll(kn, SWA_D, axis=1)
    kdup = (jnp.where(lane < SWA_D, kn, kroll).astype(BF16),
            jnp.where(lane < SWA_D, kroll, kn).astype(BF16))
    vt = jnp.concatenate([vh_ref[...], v_ref[...]], axis=0).T.astype(BF16)

    kr = lax.broadcasted_iota(jnp.int32, (2 * w, 2 * w), 0)
    qc = lax.broadcasted_iota(jnp.int32, (2 * w, 2 * w), 1) % w
    band = jnp.logical_and(kr > qc, kr <= qc + w)
    band0 = jnp.logical_and(band, kr >= jnp.where(first, w, 0))
    qlane = lax.broadcasted_iota(jnp.int32, (w, LANES), 1)

    heads = range(pairs)
    kvh = [c // (pairs // SWA_KV_HEADS) for c in heads]
    for sub in range(nsub):
        mask = band0 if sub == 0 else band
        ks = slice(sub * w, sub * w + 2 * w)
        rows = slice(sub * w, (sub + 1) * w)
        qts = []
        for c in heads:
            qn = _half_rms(q_ref[rows, c * LANES:(c + 1) * LANES], gq_ref[...]) * (SWA_D ** -0.5 * LOG2E)
            qa = jnp.where(qlane < SWA_D, qn, 0.0)
            qb = jnp.where(qlane < SWA_D, 0.0, qn)
            qts.append(jnp.concatenate([qa.T, qb.T], axis=1).astype(BF16))
        ss = [jnp.dot(kdup[kvh[c]][ks], qts[c], preferred_element_type=F32) for c in heads]
        pns = []
        for c in heads:
            s = jnp.where(mask, ss[c], NEG)
            sink = sink_ref[c:c + 1, :] * LOG2E
            m = jnp.maximum(jnp.max(s, axis=0, keepdims=True), sink)
            p = jnp.exp2(s - m)
            l = jnp.sum(p, axis=0, keepdims=True) + jnp.exp2(sink - m)
            pns.append((p * (1.0 / l)).astype(BF16))
        ots = [jnp.dot(vt[kvh[c] * SWA_D:(kvh[c] + 1) * SWA_D, ks], pns[c], preferred_element_type=F32)
               for c in heads]
        for c in heads:
            o = jnp.concatenate([ots[c][:, :w], ots[c][:, w:]], axis=0).T
            o_ref[rows, c * LANES:(c + 1) * LANES] = o.astype(o_ref.dtype)


def _swa(proj, gq2, gk2, sinkrow, batch):
    t = proj.shape[0]
    r = min(SWA_ROWS, t // batch)
    steps = (t // batch) // r
    hb = r // SWA_W

    def halo(cb):
        return pl.BlockSpec((SWA_W, LANES), lambda i: (jnp.maximum(i * hb - 1, 0), cb))

    return pl.pallas_call(
        functools.partial(_swa_kernel, steps=steps),
        grid=(t // r,),
        in_specs=[pl.BlockSpec((r, GROUP), lambda i: (i, CB_SQ // 8)),
                  pl.BlockSpec((r, LANES), lambda i: (i, CB_SK)),
                  pl.BlockSpec((r, LANES), lambda i: (i, CB_SV)),
                  halo(CB_SK), halo(CB_SV),
                  pl.BlockSpec((1, LANES), lambda i: (0, 0)),
                  pl.BlockSpec((1, LANES), lambda i: (0, 0)),
                  pl.BlockSpec((SWA_Q_HEADS // 2, 2 * SWA_W), lambda i: (0, 0))],
        out_specs=pl.BlockSpec((r, GROUP), lambda i: (i, 0)),
        out_shape=jax.ShapeDtypeStruct((t, GROUP), BF16),
        compiler_params=_cparams(("parallel",)),
        name="swa",
    )(proj, proj, proj, proj, proj, gq2, gk2, sinkrow)


RG_ROWS = 256
AB_ROWS = 2 * N_HEADS


def _regroup_kernel(x_ref, nx_ref, ab_ref, g_ref, o_ref, *, head_blocks, tail_blocks):
    j = pl.program_id(1)
    g = g_ref[...]

    @pl.when(j < head_blocks)
    def _():
        o_ref[...] = (x_ref[...] * g).astype(o_ref.dtype)

    @pl.when(jnp.logical_and(j >= head_blocks, j < head_blocks + tail_blocks))
    def _():
        o_ref[...] = (jnp.concatenate([x_ref[AB_ROWS:, :], nx_ref[...]], axis=0) * g).astype(o_ref.dtype)

    @pl.when(j == head_blocks + tail_blocks)
    def _():
        zeros = jnp.zeros((o_ref.shape[0] - AB_ROWS, o_ref.shape[1]), F32)
        o_ref[...] = jnp.concatenate([ab_ref[...] * g, zeros], axis=0).astype(o_ref.dtype)


def _regroup_w_in(w_in, gain):
    wt = jnp.swapaxes(w_in, 1, 2)
    nl, n_real, d = wt.shape
    head_blocks = A_COLS // RG_ROWS
    tail_blocks = (n_real - A_COLS - AB_ROWS) // RG_ROWS
    assert A_COLS + AB_ROWS + tail_blocks * RG_ROWS == n_real and (head_blocks + tail_blocks + 1) * RG_ROWS == NP_COLS
    per = RG_ROWS // AB_ROWS
    last_ab = n_real // AB_ROWS - 1
    return pl.pallas_call(
        functools.partial(_regroup_kernel, head_blocks=head_blocks, tail_blocks=tail_blocks),
        grid=(nl, NP_COLS // RG_ROWS),
        in_specs=[pl.BlockSpec((None, RG_ROWS, d), lambda l, j: (l, jnp.minimum(j, head_blocks + tail_blocks - 1), 0)),
                  pl.BlockSpec((None, AB_ROWS, d), lambda l, j: (l, jnp.minimum(per * (j + 1), last_ab), 0)),
                  pl.BlockSpec((None, AB_ROWS, d), lambda l, j: (l, A_COLS // AB_ROWS, 0)),
                  pl.BlockSpec((None, 1, d), lambda l, j: (l, 0, 0))],
        out_specs=pl.BlockSpec((None, RG_ROWS, d), lambda l, j: (l, j, 0)),
        out_shape=jax.ShapeDtypeStruct((nl, NP_COLS, d), BF16),
        compiler_params=_cparams(("parallel", "parallel")),
        name="regroup_w_in",
    )(wt, wt, wt, gain.astype(F32)[:, None, :])


def _layer(x, xb, batch, layer, w_in, moba_q_norm, moba_k_norm, gdn_conv, gdn_a_log, gdn_dt_bias,
           gdn_out_norm, sc_conv, swa_q_norm, swa_k_norm, swa_sinks, w_out, ffn_gain, w_gate, w_up, w_down):
    row = lambda a: a.reshape(1, -1).astype(F32)

    proj = _in_proj(xb, w_in, layer, tm=2048, tn=512)

    kn, km, vt = _moba_prep(proj, row(moba_k_norm))
    o_a = _moba(proj, kn, km, vt, row(moba_q_norm), batch)

    hp = jnp.pad(jnp.stack([gdn_a_log, gdn_dt_bias]).astype(F32), ((0, 0), (0, LANES - N_HEADS)))
    o_b = _gdn(proj, gdn_conv.astype(F32), hp, row(gdn_out_norm), batch)

    o_c = _sconv(proj, sc_conv.astype(F32), batch)

    sinkrow = jnp.repeat(swa_sinks.astype(F32), SWA_W).reshape(SWA_Q_HEADS // 2, 2 * SWA_W)
    o_d = _swa(proj, row(jnp.tile(swa_q_norm, 2)), row(jnp.tile(swa_k_norm, 2)), sinkrow, batch)

    x, xb = _out_proj((o_a, o_b, o_c, o_d), w_out, x, layer, tm=1024, tn=512)

    act = _ffn_up(xb, ffn_gain, w_gate, w_up, layer, D_FF_PAD, tm=1024, tn=512, tn_first=256)
    return _mm_res(act, w_down, x, layer, tm=1024, tn=1024, tk=D_FF_PAD // 4, tn_first=512)


def kernel(x, norm_mix, w_in, moba_q_norm, moba_k_norm, gdn_conv, gdn_a_log, gdn_dt_bias, gdn_out_norm, sc_conv, swa_q_norm, swa_k_norm, swa_sinks, w_out, norm_ffn, w_gate, w_up, w_down):
    batch, seq, d = x.shape
    w_in_b = _regroup_w_in(w_in, norm_mix)
    w_out_b = w_out.astype(BF16)
    ffn_gain = norm_ffn.astype(F32)[:, :, None]
    h = x.reshape(batch * seq, d)
    hb = h.astype(BF16)
    for l in range(norm_mix.shape[0]):
        h, hb = _layer(h, hb, batch, l, w_in_b, moba_q_norm[l], moba_k_norm[l], gdn_conv[l], gdn_a_log[l],
                       gdn_dt_bias[l], gdn_out_norm[l], sc_conv[l], swa_q_norm[l], swa_k_norm[l], swa_sinks[l],
                       w_out_b, ffn_gain, w_gate, w_up, w_down)
    return h.reshape(batch, seq, d)
```

```python
import functools

import jax
import jax.numpy as jnp
from jax import lax
from jax.experimental import pallas as pl
from jax.experimental.pallas import tpu as pltpu

F32 = jnp.float32
BF16 = jnp.bfloat16

EPS = 1e-6
LANES = 128
GROUP = 1024
HEAD_DIM = 128
N_HEADS = GROUP // HEAD_DIM
MOBA_BLOCK = 256
MOBA_TILE = 2 * MOBA_BLOCK
MOBA_TOPK = 3
GDN_CONV = 4
GDN_CHUNK = 64
SC_CONV = 3
SWA_D = 64
SWA_Q_HEADS = GROUP // SWA_D
SWA_KV_HEADS = 2
SWA_W = 128
NEG = -1e30
LOG2E = 1.4426950408889634
BIAS_ROWS = 16

CB_MQ, CB_MK, CB_MV = 0, 8, 16
CB_GQ, CB_GK, CB_GV, CB_GZ = 24, 32, 40, 48
CB_SCB, CB_SCC, CB_SCX = 56, 64, 72
CB_SQ, CB_SK, CB_SV, CB_GAB = 80, 88, 89, 90
NP_COLS = 92 * LANES
A_COLS = 6 * GROUP
D_FF_PAD = 11264

VMEM_LIMIT = 56 * 1024 * 1024


def _cparams(sem, vmem=VMEM_LIMIT):
    return pltpu.CompilerParams(dimension_semantics=sem, vmem_limit_bytes=vmem)


def _bdot(a, b):
    return jnp.dot(a.astype(BF16), b.astype(BF16), preferred_element_type=F32)


def _bdot_nt(a, b):
    return lax.dot_general(a.astype(BF16), b.astype(BF16), (((1,), (1,)), ((), ())),
                           preferred_element_type=F32)


def _sigmoid(x):
    return 1.0 / (1.0 + jnp.exp(-x))


def _silu(x):
    return x * _sigmoid(x)


def _rms_rows(x, g):
    ms = jnp.mean(x * x, axis=-1, keepdims=True)
    return x * lax.rsqrt(ms + EPS) * g


NORM_CHUNK = 256


def _row_scale_to(xb_ref, rs_ref):
    chunk = min(NORM_CHUNK, xb_ref.shape[0])

    def body(c, carry):
        rows = pl.ds(pl.multiple_of(c * chunk, chunk), chunk)
        x = xb_ref[rows, :].astype(F32)
        ms = jnp.mean(x * x, axis=-1, keepdims=True)
        rs_ref[rows, :] = jnp.broadcast_to(lax.rsqrt(ms + EPS), (chunk, LANES))
        return carry

    lax.fori_loop(0, xb_ref.shape[0] // chunk, body, 0)


def _scaled_dot(xb_ref, w_ref, rs_ref, w_transposed=False):
    dims = (((1,), (1,)), ((), ())) if w_transposed else (((1,), (0,)), ((), ()))
    acc = lax.dot_general(xb_ref[...], w_ref[...], dims, preferred_element_type=F32)
    rs = rs_ref[...]
    return jnp.concatenate([acc[:, c * LANES:(c + 1) * LANES] * rs for c in range(acc.shape[1] // LANES)], axis=1)


def _in_proj_kernel(xb_ref, w_ref, o_ref, rs_ref):
    @pl.when(pl.program_id(1) == 0)
    def _():
        _row_scale_to(xb_ref, rs_ref)

    o_ref[...] = _scaled_dot(xb_ref, w_ref, rs_ref, w_transposed=True).astype(o_ref.dtype)


def _in_proj(xb, w, layer, tm, tn):
    t, d = xb.shape
    n = w.shape[1]
    tm = min(tm, t)
    return pl.pallas_call(
        _in_proj_kernel,
        grid=(t // tm, n // tn),
        in_specs=[pl.BlockSpec((tm, d), lambda i, j: (i, 0)),
                  pl.BlockSpec((None, tn, d), lambda i, j: (layer, j, 0))],
        out_specs=pl.BlockSpec((tm, tn), lambda i, j: (i, j)),
        out_shape=jax.ShapeDtypeStruct((t, n), F32),
        scratch_shapes=[pltpu.VMEM((tm, LANES), F32)],
        compiler_params=_cparams(("parallel", "arbitrary")),
        name="in_proj",
    )(xb, w)


FIRST_ROWS = 1024


def _ffn_up_kernel(xb_ref, wg_ref, wu_ref, o_ref, rs_ref):
    @pl.when(pl.program_id(1) == 0)
    def _():
        _row_scale_to(xb_ref, rs_ref)

    a = _scaled_dot(xb_ref, wg_ref, rs_ref)
    b = _scaled_dot(xb_ref, wu_ref, rs_ref)
    o_ref[...] = (_silu(a) * b).astype(o_ref.dtype)


def _ffn_up_first_kernel(xb_ref, g_ref, wg32_ref, wu32_ref, o_ref, wgb_ref, wub_ref, rs_ref, *, valid):
    j = pl.program_id(0)

    @pl.when(j == 0)
    def _():
        _row_scale_to(xb_ref, rs_ref)

    col = j * wg32_ref.shape[1] + lax.broadcasted_iota(jnp.int32, wg32_ref.shape, 1)
    wgb_ref[...] = jnp.where(col < valid, wg32_ref[...] * g_ref[...], 0.0).astype(BF16)
    wub_ref[...] = jnp.where(col < valid, wu32_ref[...] * g_ref[...], 0.0).astype(BF16)
    a = _scaled_dot(xb_ref, wgb_ref, rs_ref)
    b = _scaled_dot(xb_ref, wub_ref, rs_ref)
    o_ref[...] = (_silu(a) * b).astype(o_ref.dtype)


def _ffn_up_rest_kernel(xb_ref, wg_ref, wu_ref, dst_ref, o_ref, rs_ref):
    del dst_ref
    _ffn_up_kernel(xb_ref, wg_ref, wu_ref, o_ref, rs_ref)


def _ffn_up(xb, gain, w_gate, w_up, layer, n_pad, tm, tn, tn_first):
    t, d = xb.shape
    n_real = w_gate.shape[2]
    tf = min(FIRST_ROWS, t)
    fi = t // tf - 1
    last_blk = (n_real - 1) // tn_first
    w32_spec = pl.BlockSpec((None, d, tn_first), lambda j: (layer, 0, jnp.minimum(j, last_blk)))
    wb_spec = pl.BlockSpec((d, tn_first), lambda j: (0, j))
    act, wgb, wub = pl.pallas_call(
        functools.partial(_ffn_up_first_kernel, valid=n_real),
        grid=(n_pad // tn_first,),
        in_specs=[pl.BlockSpec((tf, d), lambda j: (fi, 0)),
                  pl.BlockSpec((None, d, 1), lambda j: (layer, 0, 0)),
                  w32_spec, w32_spec],
        out_specs=[pl.BlockSpec((tf, tn_first), lambda j: (fi, j)), wb_spec, wb_spec],
        out_shape=[jax.ShapeDtypeStruct((t, n_pad), BF16),
                   jax.ShapeDtypeStruct((d, n_pad), BF16), jax.ShapeDtypeStruct((d, n_pad), BF16)],
        scratch_shapes=[pltpu.VMEM((tf, LANES), F32)],
        compiler_params=_cparams(("arbitrary",)),
        name="ffn_up_first",
    )(xb, gain, w_gate, w_up)
    if t == tf:
        return act
    w_spec = pl.BlockSpec((d, tn), lambda i, j: (0, j))
    return pl.pallas_call(
        _ffn_up_rest_kernel,
        grid=((t - tf) // tm, n_pad // tn),
        in_specs=[pl.BlockSpec((tm, d), lambda i, j: (i, 0)), w_spec, w_spec,
                  pl.BlockSpec(memory_space=pl.ANY)],
        out_specs=pl.BlockSpec((tm, tn), lambda i, j: (i, j)),
        out_shape=jax.ShapeDtypeStruct((t, n_pad), BF16),
        scratch_shapes=[pltpu.VMEM((tm, LANES), F32)],
        input_output_aliases={3: 0},
        compiler_params=_cparams(("parallel", "arbitrary")),
        name="ffn_up",
    )(xb, wgb, wub, act)


def _mm_res_kernel(a_ref, w_ref, r_ref, o_ref, ob_ref):
    k = pl.program_id(2)

    @pl.when(k == 0)
    def _():
        o_ref[...] = r_ref[...] + jnp.dot(a_ref[...], w_ref[...], preferred_element_type=F32)

    @pl.when(k > 0)
    def _():
        o_ref[...] += jnp.dot(a_ref[...], w_ref[...], preferred_element_type=F32)

    @pl.when(k == pl.num_programs(2) - 1)
    def _():
        ob_ref[...] = o_ref[...].astype(BF16)


def _mm_res_first_kernel(a_ref, w32_ref, r_ref, o_ref, ob_ref, wb_ref, *, valid):
    k = pl.program_id(1)
    row = k * w32_ref.shape[0] + lax.broadcasted_iota(jnp.int32, w32_ref.shape, 0)
    wb_ref[...] = jnp.where(row < valid, w32_ref[...], 0.0).astype(BF16)

    @pl.when(k == 0)
    def _():
        o_ref[...] = r_ref[...] + jnp.dot(a_ref[...], wb_ref[...], preferred_element_type=F32)

    @pl.when(k > 0)
    def _():
        o_ref[...] += jnp.dot(a_ref[...], wb_ref[...], preferred_element_type=F32)

    @pl.when(k == pl.num_programs(1) - 1)
    def _():
        ob_ref[...] = o_ref[...].astype(BF16)


def _mm_res_rest_kernel(a_ref, w_ref, r_ref, dst_ref, dstb_ref, o_ref, ob_ref):
    del dst_ref, dstb_ref
    _mm_res_kernel(a_ref, w_ref, r_ref, o_ref, ob_ref)


def _mm_res(a, w, r, layer, tm, tn, tk, tn_first):
    t, kd = a.shape
    k_real, n = w.shape[1], w.shape[2]
    tf = min(FIRST_ROWS, t)
    fi = t // tf - 1
    last_blk = (k_real - 1) // tk
    of_spec = pl.BlockSpec((tf, tn_first), lambda j, k: (fi, j))
    x, xb, wb = pl.pallas_call(
        functools.partial(_mm_res_first_kernel, valid=k_real),
        grid=(n // tn_first, kd // tk),
        in_specs=[pl.BlockSpec((tf, tk), lambda j, k: (fi, k)),
                  pl.BlockSpec((None, tk, tn_first), lambda j, k: (layer, jnp.minimum(k, last_blk), j)),
                  of_spec],
        out_specs=[of_spec, of_spec, pl.BlockSpec((tk, tn_first), lambda j, k: (k, j))],
        out_shape=[jax.ShapeDtypeStruct((t, n), F32), jax.ShapeDtypeStruct((t, n), BF16),
                   jax.ShapeDtypeStruct((kd, n), BF16)],
        compiler_params=_cparams(("parallel", "arbitrary")),
        name="mm_res_first",
    )(a, w, r)
    if t == tf:
        return x, xb
    o_spec = pl.BlockSpec((tm, tn), lambda i, j, k: (i, j))
    any_spec = pl.BlockSpec(memory_space=pl.ANY)
    return pl.pallas_call(
        _mm_res_rest_kernel,
        grid=((t - tf) // tm, n // tn, kd // tk),
        in_specs=[pl.BlockSpec((tm, tk), lambda i, j, k: (i, k)),
                  pl.BlockSpec((tk, tn), lambda i, j, k: (k, j)),
                  o_spec, any_spec, any_spec],
        out_specs=[o_spec, o_spec],
        out_shape=[jax.ShapeDtypeStruct((t, n), F32), jax.ShapeDtypeStruct((t, n), BF16)],
        input_output_aliases={3: 0, 4: 1},
        compiler_params=_cparams(("parallel", "parallel", "arbitrary")),
        name="mm_res",
    )(a, wb, r, x, xb)


def _out_proj_kernel(a0_ref, a1_ref, a2_ref, a3_ref, w_ref, r_ref, o_ref, ob_ref):
    acc = r_ref[...]
    for g, a_ref in enumerate((a0_ref, a1_ref, a2_ref, a3_ref)):
        acc = acc + jnp.dot(a_ref[...], w_ref[g * GROUP:(g + 1) * GROUP, :], preferred_element_type=F32)
    o_ref[...] = acc
    ob_ref[...] = acc.astype(BF16)


def _out_proj_first_kernel(a0_ref, a1_ref, a2_ref, a3_ref, w32_ref, r_ref, o_ref, ob_ref, wb_ref):
    wb_ref[...] = w32_ref[...].astype(BF16)
    _out_proj_kernel(a0_ref, a1_ref, a2_ref, a3_ref, wb_ref, r_ref, o_ref, ob_ref)


def _out_proj_rest_kernel(a0_ref, a1_ref, a2_ref, a3_ref, w_ref, r_ref, dst_ref, dstb_ref, o_ref, ob_ref):
    del dst_ref, dstb_ref
    _out_proj_kernel(a0_ref, a1_ref, a2_ref, a3_ref, w_ref, r_ref, o_ref, ob_ref)


def _out_proj(mix, w, r, layer, tm, tn, tn_first):
    t = r.shape[0]
    kd, n = w.shape[1], w.shape[2]
    tf = min(FIRST_ROWS, t)
    fi = t // tf - 1
    af_spec = pl.BlockSpec((tf, GROUP), lambda j: (fi, 0))
    of_spec = pl.BlockSpec((tf, tn_first), lambda j: (fi, j))
    x, xb, wb = pl.pallas_call(
        _out_proj_first_kernel,
        grid=(n // tn_first,),
        in_specs=[af_spec, af_spec, af_spec, af_spec,
                  pl.BlockSpec((None, kd, tn_first), lambda j: (layer, 0, j)),
                  of_spec],
        out_specs=[of_spec, of_spec, pl.BlockSpec((kd, tn_first), lambda j: (0, j))],
        out_shape=[jax.ShapeDtypeStruct((t, n), F32), jax.ShapeDtypeStruct((t, n), BF16),
                   jax.ShapeDtypeStruct((kd, n), BF16)],
        compiler_params=_cparams(("arbitrary",)),
        name="out_proj_first",
    )(*mix, w, r)
    if t == tf:
        return x, xb
    a_spec = pl.BlockSpec((tm, GROUP), lambda i, j: (i, 0))
    o_spec = pl.BlockSpec((tm, tn), lambda i, j: (i, j))
    any_spec = pl.BlockSpec(memory_space=pl.ANY)
    return pl.pallas_call(
        _out_proj_rest_kernel,
        grid=((t - tf) // tm, n // tn),
        in_specs=[a_spec, a_spec, a_spec, a_spec,
                  pl.BlockSpec((kd, tn), lambda i, j: (0, j)),
                  o_spec, any_spec, any_spec],
        out_specs=[o_spec, o_spec],
        out_shape=[jax.ShapeDtypeStruct((t, n), F32), jax.ShapeDtypeStruct((t, n), BF16)],
        input_output_aliases={6: 0, 7: 1},
        compiler_params=_cparams(("parallel", "arbitrary")),
        name="out_proj",
    )(*mix, wb, r, x, xb)


def _moba_prep_kernel(k_ref, v_ref, g_ref, kn_ref, km_ref, vt_ref):
    g = g_ref[...]
    for h in range(N_HEADS):
        sl = slice(h * HEAD_DIM, (h + 1) * HEAD_DIM)
        kn = _rms_rows(k_ref[:, sl], g)
        kn_ref[:, sl] = kn.astype(BF16)
        for half in range(MOBA_TILE // MOBA_BLOCK):
            km_ref[half, :, sl] = jnp.mean(kn[half * MOBA_BLOCK:(half + 1) * MOBA_BLOCK], axis=0, keepdims=True)
        vt_ref[0, sl, :] = v_ref[:, sl].T.astype(BF16)


def _moba_prep(proj, gk):
    t = proj.shape[0]
    ntile = t // MOBA_TILE
    per = MOBA_TILE // MOBA_BLOCK
    return pl.pallas_call(
        _moba_prep_kernel,
        grid=(ntile,),
        in_specs=[pl.BlockSpec((MOBA_TILE, GROUP), lambda i: (i, CB_MK // 8)),
                  pl.BlockSpec((MOBA_TILE, GROUP), lambda i: (i, CB_MV // 8)),
                  pl.BlockSpec((1, HEAD_DIM), lambda i: (0, 0))],
        out_specs=[pl.BlockSpec((MOBA_TILE, GROUP), lambda i: (i, 0)),
                   pl.BlockSpec((per, 1, GROUP), lambda i: (i, 0, 0)),
                   pl.BlockSpec((1, GROUP, MOBA_TILE), lambda i: (i, 0, 0))],
        out_shape=[jax.ShapeDtypeStruct((t, GROUP), BF16),
                   jax.ShapeDtypeStruct((ntile * per, 1, GROUP), F32),
                   jax.ShapeDtypeStruct((ntile, GROUP, MOBA_TILE), BF16)],
        compiler_params=_cparams(("parallel",)),
        name="moba_prep",
    )(proj, proj, gk)


def _moba_kernel(q_ref, k_ref, vt_ref, km_ref, g_ref, o_ref, bias_ref, acc_ref, qaug_ref, s0_ref, s1_ref,
                 p0_ref, p1_ref, *, nb):
    ti = pl.program_id(2)
    blk, tile = MOBA_BLOCK, MOBA_TILE
    qn = _rms_rows(q_ref[...], g_ref[...])

    gate = lax.dot_general(km_ref[0], qn, (((1,), (1,)), ((), ())),
                           precision=lax.Precision.HIGHEST, preferred_element_type=F32)
    row = lax.broadcasted_iota(jnp.int32, gate.shape, 0)
    own = 2 * ti + (lax.broadcasted_iota(jnp.int32, gate.shape, 1) >= blk).astype(jnp.int32)
    rowf = row.astype(F32)
    gate = jnp.where(row < own, gate, -jnp.inf)
    bias = jnp.where(row == own, 0.0, NEG)
    for _ in range(MOBA_TOPK):
        m = jnp.max(gate, axis=0, keepdims=True)
        idx = jnp.min(jnp.where(gate == m, rowf, float(nb)), axis=0, keepdims=True)
        hit = jnp.logical_and(rowf == idx, m > -jnp.inf)
        bias = jnp.where(hit, 0.0, bias)
        gate = jnp.where(hit, -jnp.inf, gate)
    bias_ref[...] = bias

    qaug_ref[0:HEAD_DIM, :] = (qn * (HEAD_DIM ** -0.5 * LOG2E)).T.astype(BF16)
    qaug_ref[HEAD_DIM + BIAS_ROWS:, :] = jnp.zeros((HEAD_DIM - BIAS_ROWS, tile), BF16)
    er = lax.broadcasted_iota(jnp.int32, (tile, HEAD_DIM), 0)
    ec = lax.broadcasted_iota(jnp.int32, (tile, HEAD_DIM), 1)
    onehot = jnp.where(ec == er // blk, 1.0, 0.0).astype(BF16)
    brow = lax.broadcasted_iota(jnp.int32, (BIAS_ROWS, tile), 0)
    ones_rows = jnp.ones((BIAS_ROWS, tile), BF16)

    def scores(t, valid):
        b0 = jnp.where(valid, bias_ref[pl.ds(2 * t, 1), :], NEG)
        b1 = jnp.where(valid, bias_ref[pl.ds(2 * t + 1, 1), :], NEG)
        qaug_ref[HEAD_DIM:HEAD_DIM + BIAS_ROWS, :] = jnp.where(
            brow == 0, b0, jnp.where(brow == 1, b1, 0.0)).astype(BF16)
        kt = k_ref[pl.ds(pl.multiple_of(t * tile, tile), tile), :]
        return jnp.dot(jnp.concatenate([kt, onehot], axis=1), qaug_ref[...],
                       preferred_element_type=F32)

    s_refs, p_refs = (s0_ref, s1_ref), (p0_ref, p1_ref)
    kr = lax.broadcasted_iota(jnp.int32, (tile, tile), 0)
    qc = lax.broadcasted_iota(jnp.int32, (tile, tile), 1)
    s_refs[0][...] = jnp.where(kr <= qc, scores(ti, True), NEG)
    p_refs[1][...] = jnp.zeros((tile, tile), BF16)
    acc_ref[...] = jnp.zeros_like(acc_ref)

    def trip(n, par, carry):
        m, alpha_prev = carry
        tc = jnp.where(n == 1, ti, jnp.clip(n - 2, 0, ti))
        pv = jnp.dot(jnp.concatenate([vt_ref[tc], ones_rows], axis=0), p_refs[1 - par][...],
                     preferred_element_type=F32)
        s_refs[1 - par][...] = scores(jnp.minimum(n, ti), n < ti)
        s = s_refs[par][...]
        m_new = jnp.maximum(m, jnp.max(s, axis=0, keepdims=True))
        alpha = jnp.exp2(m - m_new)
        p_refs[par][...] = jnp.exp2(s - m_new).astype(BF16)
        acc_ref[...] = acc_ref[...] * alpha_prev + pv
        return m_new, alpha

    def body(j, carry):
        return trip(2 * j + 1, 1, trip(2 * j, 0, carry))

    init = (jnp.full((1, tile), NEG, F32), jnp.ones((1, tile), F32))
    lax.fori_loop(0, (ti + 3) // 2, body, init)
    o_ref[...] = (acc_ref[0:HEAD_DIM, :] / acc_ref[HEAD_DIM:HEAD_DIM + 1, :]).T.astype(o_ref.dtype)


def _moba(proj, kn, km, vt, gq, batch):
    t = proj.shape[0]
    s = t // batch
    nb = s // MOBA_BLOCK
    nt = s // MOBA_TILE
    km = km.reshape(batch, nb, GROUP)
    return pl.pallas_call(
        functools.partial(_moba_kernel, nb=nb),
        grid=(batch, N_HEADS, nt),
        in_specs=[pl.BlockSpec((MOBA_TILE, HEAD_DIM), lambda b, h, i: (b * nt + i, CB_MQ + h)),
                  pl.BlockSpec((s, HEAD_DIM), lambda b, h, i: (b, h)),
                  pl.BlockSpec((nt, HEAD_DIM, MOBA_TILE), lambda b, h, i: (b, h, 0)),
                  pl.BlockSpec((1, nb, HEAD_DIM), lambda b, h, i: (b, 0, h)),
                  pl.BlockSpec((1, HEAD_DIM), lambda b, h, i: (0, 0))],
        out_specs=pl.BlockSpec((MOBA_TILE, HEAD_DIM), lambda b, h, i: (b * nt + i, h)),
        out_shape=jax.ShapeDtypeStruct((t, GROUP), BF16),
        scratch_shapes=[pltpu.VMEM((nb, MOBA_TILE), F32),
                        pltpu.VMEM((HEAD_DIM + BIAS_ROWS, MOBA_TILE), F32),
                        pltpu.VMEM((2 * HEAD_DIM, MOBA_TILE), BF16),
                        pltpu.VMEM((MOBA_TILE, MOBA_TILE), F32), pltpu.VMEM((MOBA_TILE, MOBA_TILE), F32),
                        pltpu.VMEM((MOBA_TILE, MOBA_TILE), BF16), pltpu.VMEM((MOBA_TILE, MOBA_TILE), BF16)],
        compiler_params=_cparams(("parallel", "parallel", "arbitrary")),
        name="moba",
    )(proj, kn, vt, km, gq)


GDN_ROWS = 256
GDN_BASE = 8
GDN_HPS = 8


def _conv_silu(x_ref, halo_ref, w_ref, first):
    r = x_ref.shape[0]
    halo = halo_ref[...] * jnp.where(first, 0.0, 1.0)
    xb = jnp.concatenate([halo, x_ref[...]], axis=0)
    w = w_ref[...]
    out = None
    for tap in range(GDN_CONV):
        sh = GDN_CONV - 1 - tap
        xs = xb if sh == 0 else pltpu.roll(xb, sh, axis=0)
        term = xs[8:8 + r] * w[tap:tap + 1, :]
        out = term if out is None else out + term
    return _silu(out)


def _gdn_heads(qs, ks, vs, gs, betas, states):
    r = qs[0].shape[0]
    c = GDN_CHUNK
    pair = 2 * c
    npair = r // pair
    nh = len(qs)
    row = lax.broadcasted_iota(jnp.int32, (pair, pair), 0)
    col = lax.broadcasted_iota(jnp.int32, (pair, pair), 1)
    same = (row // c) == (col // c)
    tril = jnp.logical_and(same, row >= col)
    strict = jnp.logical_and(same, row > col)
    eye = (row == col).astype(F32)
    rin = row % c
    units = [(h, pi) for pi in range(npair) for h in range(nh)]

    def rows(x, u):
        return x[u[0]][u[1] * pair:(u[1] + 1) * pair]

    gcum = {u: rows(gs, u) for u in units}
    decay = {u: jnp.exp(jnp.where(tril, gcum[u] - gcum[u].T, -jnp.inf)) for u in units}
    eg = {u: jnp.exp(gcum[u]) for u in units}
    g_end = {u: (gcum[u][c - 1:c, :], gcum[u][pair - 1:pair, :]) for u in units}
    kb = {u: rows(ks, u) * rows(betas, u) for u in units}
    vb = {u: rows(vs, u) * rows(betas, u) for u in units}
    lmat = {u: jnp.where(strict, _bdot_nt(kb[u], rows(ks, u)) * decay[u], 0.0) for u in units}
    qk = {u: _bdot_nt(rows(qs, u), rows(ks, u)) * decay[u] for u in units}
    def blocks(size):
        return (row // size) == (col // size)

    lbase = {u: jnp.where(blocks(GDN_BASE), lmat[u], 0.0) for u in units}
    tinv = {u: eye - lbase[u] for u in units}
    lpow = lbase
    span = 1
    while 2 * span < GDN_BASE:
        lpow = {u: _bdot(lpow[u], lpow[u]) for u in units}
        tinv = {u: tinv[u] + _bdot(tinv[u], lpow[u]) for u in units}
        span *= 2
    size = GDN_BASE
    while size < c:
        below = jnp.logical_and(blocks(2 * size), jnp.logical_not(blocks(size)))
        tc = {u: _bdot(tinv[u], jnp.where(below, lmat[u], 0.0)) for u in units}
        tinv = {u: tinv[u] - _bdot(tc[u], tinv[u]) for u in units}
        size *= 2
    uw = {u: _bdot(tinv[u], jnp.concatenate([vb[u], kb[u] * eg[u]], axis=1)) for u in units}
    qd = {u: rows(qs, u) * eg[u] for u in units}
    kdt = {u: (rows(ks, u) * jnp.exp(jnp.where(row < c, g_end[u][0], g_end[u][1]) - gcum[u])).T for u in units}

    states = list(states)
    vns = {u: [] for u in units}
    o_st = {u: [] for u in units}
    for pi in range(npair):
        for ci in range(2):
            cs = slice(ci * c, (ci + 1) * c)
            for h in range(nh):
                u = (h, pi)
                ws = _bdot(jnp.concatenate([uw[u][cs, HEAD_DIM:], qd[u][cs]], axis=0), states[h])
                vn = uw[u][cs, :HEAD_DIM] - ws[:c]
                o_st[u].append(ws[c:])
                vns[u].append(vn)
                zero = jnp.zeros_like(vn)
                vn_pad = jnp.concatenate([vn, zero] if ci == 0 else [zero, vn], axis=0)
                states[h] = states[h] * jnp.exp(g_end[u][ci]) + _bdot(kdt[u], vn_pad)
    outs = []
    for h in range(nh):
        parts = [jnp.concatenate(o_st[(h, pi)], axis=0) + _bdot(qk[(h, pi)], jnp.concatenate(vns[(h, pi)], axis=0))
                 for pi in range(npair)]
        outs.append(jnp.concatenate(parts, axis=0))
    return outs, states


def _gdn_kernel(q_ref, k_ref, v_ref, qh_ref, kh_ref, vh_ref, wq_ref, wk_ref, wv_ref,
                gab_ref, z_ref, hp_ref, gn_ref, o_ref, s_ref):
    hg = pl.program_id(1)
    first = pl.program_id(2) == 0

    @pl.when(first)
    def _():
        s_ref[...] = jnp.zeros_like(s_ref)

    q2 = _conv_silu(q_ref, qh_ref, wq_ref, first)
    k2 = _conv_silu(k_ref, kh_ref, wk_ref, first)
    v2 = _conv_silu(v_ref, vh_ref, wv_ref, first)
    gab = gab_ref[...]
    lane = lax.broadcasted_iota(jnp.int32, gab.shape, 1)
    xg = gab + hp_ref[1:2, :]
    softplus = jnp.maximum(xg, 0.0) + jnp.log1p(jnp.exp(-jnp.abs(xg)))
    gcum_all = -jnp.exp(hp_ref[0:1, :]) * softplus
    beta_all = _sigmoid(gab)
    rin = lax.broadcasted_iota(jnp.int32, gab.shape, 0) % GDN_CHUNK
    sh = 1
    while sh < GDN_CHUNK:
        gcum_all = gcum_all + jnp.where(rin >= sh, pltpu.roll(gcum_all, sh, axis=0), 0.0)
        sh *= 2
    zeros = jnp.zeros_like(gab)

    qs, ks, vs, gs, betas = [], [], [], [], []
    for hh in range(GDN_HPS):
        h = hg * GDN_HPS + hh
        sl = slice(hh * HEAD_DIM, (hh + 1) * HEAD_DIM)
        q, k = q2[:, sl], k2[:, sl]
        qs.append(q * lax.rsqrt(jnp.sum(q * q, axis=-1, keepdims=True) + EPS) * (HEAD_DIM ** -0.5))
        ks.append(k * lax.rsqrt(jnp.sum(k * k, axis=-1, keepdims=True) + EPS))
        vs.append(v2[:, sl])
        gs.append(jnp.sum(jnp.where(lane == h, gcum_all, 0.0), axis=1, keepdims=True) + zeros)
        betas.append(jnp.sum(jnp.where(lane == N_HEADS + h, beta_all, 0.0), axis=1, keepdims=True) + zeros)

    outs, states = _gdn_heads(qs, ks, vs, gs, betas, [s_ref[hh] for hh in range(GDN_HPS)])
    for hh in range(GDN_HPS):
        sl = slice(hh * HEAD_DIM, (hh + 1) * HEAD_DIM)
        s_ref[hh] = states[hh]
        on = _rms_rows(outs[hh], gn_ref[...])
        o_ref[:, sl] = (on * _silu(z_ref[:, sl])).astype(o_ref.dtype)


def _gdn(proj, conv_w, hp, gn, batch):
    t = proj.shape[0]
    s = t // batch
    r = GDN_ROWS
    steps = s // r
    hb = r // 8
    wide = GDN_HPS * HEAD_DIM
    cpb = GDN_HPS

    def main(cb):
        return pl.BlockSpec((r, wide), lambda b, h, i: (b * steps + i, cb // cpb + h))

    def halo(cb):
        return pl.BlockSpec((8, wide), lambda b, h, i: (jnp.maximum((b * steps + i) * hb - 1, 0), cb // cpb + h))

    def wspec(off):
        return pl.BlockSpec((GDN_CONV, wide), lambda b, h, i: (0, off // cpb + h))

    return pl.pallas_call(
        _gdn_kernel,
        grid=(batch, N_HEADS // GDN_HPS, steps),
        in_specs=[main(CB_GQ), main(CB_GK), main(CB_GV), halo(CB_GQ), halo(CB_GK), halo(CB_GV),
                  wspec(0), wspec(N_HEADS), wspec(2 * N_HEADS),
                  pl.BlockSpec((r, LANES), lambda b, h, i: (b * steps + i, CB_GAB)),
                  main(CB_GZ),
                  pl.BlockSpec((2, LANES), lambda b, h, i: (0, 0)),
                  pl.BlockSpec((1, HEAD_DIM), lambda b, h, i: (0, 0))],
        out_specs=pl.BlockSpec((r, wide), lambda b, h, i: (b * steps + i, h)),
        out_shape=jax.ShapeDtypeStruct((t, GROUP), BF16),
        scratch_shapes=[pltpu.VMEM((GDN_HPS, HEAD_DIM, HEAD_DIM), F32)],
        compiler_params=_cparams(("parallel", "parallel", "arbitrary")),
        name="gdn",
    )(proj, proj, proj, proj, proj, proj, conv_w, conv_w, conv_w, proj, proj, hp, gn)


SC_ROWS = 512


def _sconv_kernel(b_ref, c_ref, x_ref, ch_ref, xh_ref, w_ref, o_ref, *, steps):
    first = pl.program_id(0) % steps == 0
    r = b_ref.shape[0]
    y = c_ref[...] * x_ref[...]
    yh = ch_ref[...] * xh_ref[...] * jnp.where(first, 0.0, 1.0)
    yb = jnp.concatenate([yh, y], axis=0)
    w = w_ref[...]
    out = None
    for tap in range(SC_CONV):
        sh = SC_CONV - 1 - tap
        ys = yb if sh == 0 else pltpu.roll(yb, sh, axis=0)
        term = ys[8:8 + r] * w[tap:tap + 1, :]
        out = term if out is None else out + term
    o_ref[...] = (b_ref[...] * out).astype(o_ref.dtype)


def _sconv(proj, w, batch):
    t = proj.shape[0]
    r = min(SC_ROWS, t // batch)
    steps = (t // batch) // r
    hb = r // 8

    def main(cb):
        return pl.BlockSpec((r, GROUP), lambda i: (i, cb // 8))

    def halo(cb):
        return pl.BlockSpec((8, GROUP), lambda i: (jnp.maximum(i * hb - 1, 0), cb // 8))

    return pl.pallas_call(
        functools.partial(_sconv_kernel, steps=steps),
        grid=(t // r,),
        in_specs=[main(CB_SCB), main(CB_SCC), main(CB_SCX), halo(CB_SCC), halo(CB_SCX),
                  pl.BlockSpec((SC_CONV, GROUP), lambda i: (0, 0))],
        out_specs=pl.BlockSpec((r, GROUP), lambda i: (i, 0)),
        out_shape=jax.ShapeDtypeStruct((t, GROUP), BF16),
        compiler_params=_cparams(("parallel",)),
        name="sconv",
    )(proj, proj, proj, proj, proj, w)


SWA_ROWS = 512


def _half_rms(x, g2):
    lane = lax.broadcasted_iota(jnp.int32, x.shape, 1)
    lo = lane < SWA_D
    x2 = x * x
    ms_lo = jnp.sum(jnp.where(lo, x2, 0.0), axis=-1, keepdims=True) * (1.0 / SWA_D)
    ms_hi = jnp.sum(jnp.where(lo, 0.0, x2), axis=-1, keepdims=True) * (1.0 / SWA_D)
    rs = jnp.where(lo, lax.rsqrt(ms_lo + EPS), lax.rsqrt(ms_hi + EPS))
    return x * rs * g2


def _swa_kernel(q_ref, k_ref, v_ref, kh_ref, vh_ref, gq_ref, gk_ref, sink_ref, o_ref, *, steps):
    first = pl.program_id(0) % steps == 0
    r = q_ref.shape[0]
    w = SWA_W
    nsub = r // w
    pairs = SWA_Q_HEADS // 2
    lane = lax.broadcasted_iota(jnp.int32, (r + w, LANES), 1)

    kn = _half_rms(jnp.concatenate([kh_ref[...], k_ref[...]], axis=0), gk_ref[...])
    kroll = pltpu.roll(kn, SWA_D, axis=1)
    kdup = (jnp.where(lane < SWA_D, kn, kroll).astype(BF16),
            jnp.where(lane < SWA_D, kroll, kn).astype(BF16))
    vt = jnp.concatenate([vh_ref[...], v_ref[...]], axis=0).T.astype(BF16)

    kr = lax.broadcasted_iota(jnp.int32, (2 * w, 2 * w), 0)
    qc = lax.broadcasted_iota(jnp.int32, (2 * w, 2 * w), 1) % w
    band = jnp.logical_and(kr > qc, kr <= qc + w)
    band0 = jnp.logical_and(band, kr >= jnp.where(first, w, 0))
    qlane = lax.broadcasted_iota(jnp.int32, (w, LANES), 1)

    heads = range(pairs)
    kvh = [c // (pairs // SWA_KV_HEADS) for c in heads]
    for sub in range(nsub):
        mask = band0 if sub == 0 else band
        ks = slice(sub * w, sub * w + 2 * w)
        rows = slice(sub * w, (sub + 1) * w)
        qts = []
        for c in heads:
            qn = _half_rms(q_ref[rows, c * LANES:(c + 1) * LANES], gq_ref[...]) * (SWA_D ** -0.5 * LOG2E)
            qa = jnp.where(qlane < SWA_D, qn, 0.0)
            qb = jnp.where(qlane < SWA_D, 0.0, qn)
            qts.append(jnp.concatenate([qa.T, qb.T], axis=1).astype(BF16))
        ss = [jnp.dot(kdup[kvh[c]][ks], qts[c], preferred_element_type=F32) for c in heads]
        pns = []
        for c in heads:
            s = jnp.where(mask, ss[c], NEG)
            sink = sink_ref[c:c + 1, :] * LOG2E
            m = jnp.maximum(jnp.max(s, axis=0, keepdims=True), sink)
            p = jnp.exp2(s - m)
            l = jnp.sum(p, axis=0, keepdims=True) + jnp.exp2(sink - m)
            pns.append((p * (1.0 / l)).astype(BF16))
        ots = [jnp.dot(vt[kvh[c] * SWA_D:(kvh[c] + 1) * SWA_D, ks], pns[c], preferred_element_type=F32)
               for c in heads]
        for c in heads:
            o = jnp.concatenate([ots[c][:, :w], ots[c][:, w:]], axis=0).T
            o_ref[rows, c * LANES:(c + 1) * LANES] = o.astype(o_ref.dtype)


def _swa(proj, gq2, gk2, sinkrow, batch):
    t = proj.shape[0]
    r = min(SWA_ROWS, t // batch)
    steps = (t // batch) // r
    hb = r // SWA_W

    def halo(cb):
        return pl.BlockSpec((SWA_W, LANES), lambda i: (jnp.maximum(i * hb - 1, 0), cb))

    return pl.pallas_call(
        functools.partial(_swa_kernel, steps=steps),
        grid=(t // r,),
        in_specs=[pl.BlockSpec((r, GROUP), lambda i: (i, CB_SQ // 8)),
                  pl.BlockSpec((r, LANES), lambda i: (i, CB_SK)),
                  pl.BlockSpec((r, LANES), lambda i: (i, CB_SV)),
                  halo(CB_SK), halo(CB_SV),
                  pl.BlockSpec((1, LANES), lambda i: (0, 0)),
                  pl.BlockSpec((1, LANES), lambda i: (0, 0)),
                  pl.BlockSpec((SWA_Q_HEADS // 2, 2 * SWA_W), lambda i: (0, 0))],
        out_specs=pl.BlockSpec((r, GROUP), lambda i: (i, 0)),
        out_shape=jax.ShapeDtypeStruct((t, GROUP), BF16),
        compiler_params=_cparams(("parallel",)),
        name="swa",
    )(proj, proj, proj, proj, proj, gq2, gk2, sinkrow)


RG_ROWS = 256
AB_ROWS = 2 * N_HEADS


def _regroup_kernel(x_ref, nx_ref, ab_ref, g_ref, o_ref, *, head_blocks, tail_blocks):
    j = pl.program_id(1)
    g = g_ref[...]

    @pl.when(j < head_blocks)
    def _():
        o_ref[...] = (x_ref[...] * g).astype(o_ref.dtype)

    @pl.when(jnp.logical_and(j >= head_blocks, j < head_blocks + tail_blocks))
    def _():
        o_ref[...] = (jnp.concatenate([x_ref[AB_ROWS:, :], nx_ref[...]], axis=0) * g).astype(o_ref.dtype)

    @pl.when(j == head_blocks + tail_blocks)
    def _():
        zeros = jnp.zeros((o_ref.shape[0] - AB_ROWS, o_ref.shape[1]), F32)
        o_ref[...] = jnp.concatenate([ab_ref[...] * g, zeros], axis=0).astype(o_ref.dtype)


def _regroup_w_in(w_in, gain):
    wt = jnp.swapaxes(w_in, 1, 2)
    nl, n_real, d = wt.shape
    head_blocks = A_COLS // RG_ROWS
    tail_blocks = (n_real - A_COLS - AB_ROWS) // RG_ROWS
    assert A_COLS + AB_ROWS + tail_blocks * RG_ROWS == n_real and (head_blocks + tail_blocks + 1) * RG_ROWS == NP_COLS
    per = RG_ROWS // AB_ROWS
    last_ab = n_real // AB_ROWS - 1
    return pl.pallas_call(
        functools.partial(_regroup_kernel, head_blocks=head_blocks, tail_blocks=tail_blocks),
        grid=(nl, NP_COLS // RG_ROWS),
        in_specs=[pl.BlockSpec((None, RG_ROWS, d), lambda l, j: (l, jnp.minimum(j, head_blocks + tail_blocks - 1), 0)),
                  pl.BlockSpec((None, AB_ROWS, d), lambda l, j: (l, jnp.minimum(per * (j + 1), last_ab), 0)),
                  pl.BlockSpec((None, AB_ROWS, d), lambda l, j: (l, A_COLS // AB_ROWS, 0)),
                  pl.BlockSpec((None, 1, d), lambda l, j: (l, 0, 0))],
        out_specs=pl.BlockSpec((None, RG_ROWS, d), lambda l, j: (l, j, 0)),
        out_shape=jax.ShapeDtypeStruct((nl, NP_COLS, d), BF16),
        compiler_params=_cparams(("parallel", "parallel")),
        name="regroup_w_in",
    )(wt, wt, wt, gain.astype(F32)[:, None, :])


def _layer(x, xb, batch, layer, w_in, moba_q_norm, moba_k_norm, gdn_conv, gdn_a_log, gdn_dt_bias,
           gdn_out_norm, sc_conv, swa_q_norm, swa_k_norm, swa_sinks, w_out, ffn_gain, w_gate, w_up, w_down):
    row = lambda a: a.reshape(1, -1).astype(F32)

    proj = _in_proj(xb, w_in, layer, tm=2048, tn=512)

    kn, km, vt = _moba_prep(proj, row(moba_k_norm))
    o_a = _moba(proj, kn, km, vt, row(moba_q_norm), batch)

    hp = jnp.pad(jnp.stack([gdn_a_log, gdn_dt_bias]).astype(F32), ((0, 0), (0, LANES - N_HEADS)))
    o_b = _gdn(proj, gdn_conv.astype(F32), hp, row(gdn_out_norm), batch)

    o_c = _sconv(proj, sc_conv.astype(F32), batch)

    sinkrow = jnp.repeat(swa_sinks.astype(F32), SWA_W).reshape(SWA_Q_HEADS // 2, 2 * SWA_W)
    o_d = _swa(proj, row(jnp.tile(swa_q_norm, 2)), row(jnp.tile(swa_k_norm, 2)), sinkrow, batch)

    x, xb = _out_proj((o_a, o_b, o_c, o_d), w_out, x, layer, tm=1024, tn=512, tn_first=256)

    act = _ffn_up(xb, ffn_gain, w_gate, w_up, layer, D_FF_PAD, tm=1024, tn=512, tn_first=256)
    return _mm_res(act, w_down, x, layer, tm=1024, tn=1024, tk=D_FF_PAD // 4, tn_first=512)


def kernel(x, norm_mix, w_in, moba_q_norm, moba_k_norm, gdn_conv, gdn_a_log, gdn_dt_bias, gdn_out_norm, sc_conv, swa_q_norm, swa_k_norm, swa_sinks, w_out, norm_ffn, w_gate, w_up, w_down):
    batch, seq, d = x.shape
    w_in_b = _regroup_w_in(w_in, norm_mix)
    ffn_gain = norm_ffn.astype(F32)[:, :, None]
    h = x.reshape(batch * seq, d)
    hb = h.astype(BF16)
    for l in range(norm_mix.shape[0]):
        h, hb = _layer(h, hb, batch, l, w_in_b, moba_q_norm[l], moba_k_norm[l], gdn_conv[l], gdn_a_log[l],
                       gdn_dt_bias[l], gdn_out_norm[l], sc_conv[l], swa_q_norm[l], swa_k_norm[l], swa_sinks[l],
                       w_out, ffn_gain, w_gate, w_up, w_down)
    return h.reshape(batch, seq, d)
```

```python
import functools

import jax
import jax.numpy as jnp
from jax import lax
from jax.experimental import pallas as pl
from jax.experimental.pallas import tpu as pltpu

F32 = jnp.float32
BF16 = jnp.bfloat16

EPS = 1e-6
LANES = 128
GROUP = 1024
HEAD_DIM = 128
N_HEADS = GROUP // HEAD_DIM
MOBA_BLOCK = 256
MOBA_TILE = 2 * MOBA_BLOCK
MOBA_TOPK = 3
GDN_CONV = 4
GDN_CHUNK = 64
SC_CONV = 3
SWA_D = 64
SWA_Q_HEADS = GROUP // SWA_D
SWA_KV_HEADS = 2
SWA_W = 128
NEG = -1e30
LOG2E = 1.4426950408889634
BIAS_ROWS = 16

CB_MQ, CB_MK, CB_MV = 0, 8, 16
CB_GQ, CB_GK, CB_GV, CB_GZ = 24, 32, 40, 48
CB_SCB, CB_SCC, CB_SCX = 56, 64, 72
CB_SQ, CB_SK, CB_SV, CB_GAB = 80, 88, 89, 90
NP_COLS = 92 * LANES
A_COLS = 6 * GROUP
D_FF_PAD = 11264

VMEM_LIMIT = 56 * 1024 * 1024


def _cparams(sem, vmem=VMEM_LIMIT):
    return pltpu.CompilerParams(dimension_semantics=sem, vmem_limit_bytes=vmem)


def _bdot(a, b):
    return jnp.dot(a.astype(BF16), b.astype(BF16), preferred_element_type=F32)


def _bdot_nt(a, b):
    return lax.dot_general(a.astype(BF16), b.astype(BF16), (((1,), (1,)), ((), ())),
                           preferred_element_type=F32)


def _sigmoid(x):
    return 1.0 / (1.0 + jnp.exp(-x))


def _silu(x):
    return x * _sigmoid(x)


def _rms_rows(x, g):
    ms = jnp.mean(x * x, axis=-1, keepdims=True)
    return x * lax.rsqrt(ms + EPS) * g


NORM_CHUNK = 256


def _row_scale_to(xb_ref, rs_ref):
    chunk = min(NORM_CHUNK, xb_ref.shape[0])

    def body(c, carry):
        rows = pl.ds(pl.multiple_of(c * chunk, chunk), chunk)
        x = xb_ref[rows, :].astype(F32)
        ms = jnp.mean(x * x, axis=-1, keepdims=True)
        rs_ref[rows, :] = jnp.broadcast_to(lax.rsqrt(ms + EPS), (chunk, LANES))
        return carry

    lax.fori_loop(0, xb_ref.shape[0] // chunk, body, 0)


def _scaled_dot(xb_ref, w_ref, rs_ref, w_transposed=False):
    dims = (((1,), (1,)), ((), ())) if w_transposed else (((1,), (0,)), ((), ()))
    acc = lax.dot_general(xb_ref[...], w_ref[...], dims, preferred_element_type=F32)
    rs = rs_ref[...]
    return jnp.concatenate([acc[:, c * LANES:(c + 1) * LANES] * rs for c in range(acc.shape[1] // LANES)], axis=1)


def _in_proj_kernel(xb_ref, w_ref, o_ref, rs_ref):
    @pl.when(pl.program_id(1) == 0)
    def _():
        _row_scale_to(xb_ref, rs_ref)

    o_ref[...] = _scaled_dot(xb_ref, w_ref, rs_ref, w_transposed=True).astype(o_ref.dtype)


def _in_proj(xb, w, layer, tm, tn):
    t, d = xb.shape
    n = w.shape[1]
    tm = min(tm, t)
    return pl.pallas_call(
        _in_proj_kernel,
        grid=(t // tm, n // tn),
        in_specs=[pl.BlockSpec((tm, d), lambda i, j: (i, 0)),
                  pl.BlockSpec((None, tn, d), lambda i, j: (layer, j, 0))],
        out_specs=pl.BlockSpec((tm, tn), lambda i, j: (i, j)),
        out_shape=jax.ShapeDtypeStruct((t, n), F32),
        scratch_shapes=[pltpu.VMEM((tm, LANES), F32)],
        compiler_params=_cparams(("parallel", "arbitrary")),
        name="in_proj",
    )(xb, w)


def _ffn_up_kernel(xb_ref, wg_ref, wu_ref, o_ref, rs_ref):
    @pl.when(pl.program_id(1) == 0)
    def _():
        _row_scale_to(xb_ref, rs_ref)

    a = _scaled_dot(xb_ref, wg_ref, rs_ref)
    b = _scaled_dot(xb_ref, wu_ref, rs_ref)
    o_ref[...] = (_silu(a) * b).astype(o_ref.dtype)


def _ffn_up_first_kernel(xb_ref, g_ref, wg32_ref, wu32_ref, o_ref, wgb_ref, wub_ref, rs_ref, *, valid):
    j = pl.program_id(0)

    @pl.when(j == 0)
    def _():
        _row_scale_to(xb_ref, rs_ref)

    col = j * wg32_ref.shape[1] + lax.broadcasted_iota(jnp.int32, wg32_ref.shape, 1)
    wgb_ref[...] = jnp.where(col < valid, wg32_ref[...] * g_ref[...], 0.0).astype(BF16)
    wub_ref[...] = jnp.where(col < valid, wu32_ref[...] * g_ref[...], 0.0).astype(BF16)
    a = _scaled_dot(xb_ref, wgb_ref, rs_ref)
    b = _scaled_dot(xb_ref, wub_ref, rs_ref)
    o_ref[...] = (_silu(a) * b).astype(o_ref.dtype)


def _ffn_up_rest_kernel(xb_ref, wg_ref, wu_ref, first_ref, o_ref, rs_ref):
    last = pl.num_programs(0) - 1

    @pl.when(pl.program_id(0) < last)
    def _():
        _ffn_up_kernel(xb_ref, wg_ref, wu_ref, o_ref, rs_ref)

    @pl.when(pl.program_id(0) == last)
    def _():
        o_ref[...] = first_ref[...]


def _ffn_up(xb, gain, w_gate, w_up, layer, n_pad, tm, tn, tn_first):
    t, d = xb.shape
    n_real = w_gate.shape[2]
    tm = min(tm, t)
    nrt = t // tm
    last_blk = (n_real - 1) // tn_first
    w32_spec = pl.BlockSpec((None, d, tn_first), lambda j: (layer, 0, jnp.minimum(j, last_blk)))
    wb_spec = pl.BlockSpec((d, tn_first), lambda j: (0, j))
    act_first, wgb, wub = pl.pallas_call(
        functools.partial(_ffn_up_first_kernel, valid=n_real),
        grid=(n_pad // tn_first,),
        in_specs=[pl.BlockSpec((tm, d), lambda j: (nrt - 1, 0)),
                  pl.BlockSpec((None, d, 1), lambda j: (layer, 0, 0)),
                  w32_spec, w32_spec],
        out_specs=[pl.BlockSpec((tm, tn_first), lambda j: (0, j)), wb_spec, wb_spec],
        out_shape=[jax.ShapeDtypeStruct((tm, n_pad), BF16),
                   jax.ShapeDtypeStruct((d, n_pad), BF16), jax.ShapeDtypeStruct((d, n_pad), BF16)],
        scratch_shapes=[pltpu.VMEM((tm, LANES), F32)],
        compiler_params=_cparams(("arbitrary",)),
        name="ffn_up_first",
    )(xb, gain, w_gate, w_up)
    if nrt == 1:
        return act_first
    w_spec = pl.BlockSpec((d, tn), lambda i, j: (0, j))
    return pl.pallas_call(
        _ffn_up_rest_kernel,
        grid=(nrt, n_pad // tn),
        in_specs=[pl.BlockSpec((tm, d), lambda i, j: (i, 0)), w_spec, w_spec,
                  pl.BlockSpec((tm, tn), lambda i, j: (0, jnp.where(i == nrt - 1, j, 0)))],
        out_specs=pl.BlockSpec((tm, tn), lambda i, j: (i, j)),
        out_shape=jax.ShapeDtypeStruct((t, n_pad), BF16),
        scratch_shapes=[pltpu.VMEM((tm, LANES), F32)],
        compiler_params=_cparams(("parallel", "arbitrary")),
        name="ffn_up",
    )(xb, wgb, wub, act_first)


def _mm_res_kernel(a_ref, w_ref, r_ref, o_ref, ob_ref):
    k = pl.program_id(2)

    @pl.when(k == 0)
    def _():
        o_ref[...] = r_ref[...] + jnp.dot(a_ref[...], w_ref[...], preferred_element_type=F32)

    @pl.when(k > 0)
    def _():
        o_ref[...] += jnp.dot(a_ref[...], w_ref[...], preferred_element_type=F32)

    @pl.when(k == pl.num_programs(2) - 1)
    def _():
        ob_ref[...] = o_ref[...].astype(BF16)


def _mm_res_first_kernel(a_ref, w32_ref, r_ref, o_ref, ob_ref, wb_ref, *, valid):
    k = pl.program_id(1)
    row = k * w32_ref.shape[0] + lax.broadcasted_iota(jnp.int32, w32_ref.shape, 0)
    wb_ref[...] = jnp.where(row < valid, w32_ref[...], 0.0).astype(BF16)

    @pl.when(k == 0)
    def _():
        o_ref[...] = r_ref[...] + jnp.dot(a_ref[...], wb_ref[...], preferred_element_type=F32)

    @pl.when(k > 0)
    def _():
        o_ref[...] += jnp.dot(a_ref[...], wb_ref[...], preferred_element_type=F32)

    @pl.when(k == pl.num_programs(1) - 1)
    def _():
        ob_ref[...] = o_ref[...].astype(BF16)


def _mm_res_rest_kernel(a_ref, w_ref, r_ref, xf_ref, xbf_ref, o_ref, ob_ref):
    last = pl.num_programs(0) - 1

    @pl.when(pl.program_id(0) < last)
    def _():
        _mm_res_kernel(a_ref, w_ref, r_ref, o_ref, ob_ref)

    @pl.when(jnp.logical_and(pl.program_id(0) == last, pl.program_id(2) == 0))
    def _():
        o_ref[...] = xf_ref[...]
        ob_ref[...] = xbf_ref[...]


def _mm_res(a, w, r, layer, tm, tn, tk, tn_first):
    t, kd = a.shape
    k_real, n = w.shape[1], w.shape[2]
    tm = min(tm, t)
    nrt = t // tm
    last_blk = (k_real - 1) // tk
    of_spec = pl.BlockSpec((tm, tn_first), lambda j, k: (0, j))
    x_first, xb_first, wb = pl.pallas_call(
        functools.partial(_mm_res_first_kernel, valid=k_real),
        grid=(n // tn_first, kd // tk),
        in_specs=[pl.BlockSpec((tm, tk), lambda j, k: (nrt - 1, k)),
                  pl.BlockSpec((None, tk, tn_first), lambda j, k: (layer, jnp.minimum(k, last_blk), j)),
                  pl.BlockSpec((tm, tn_first), lambda j, k: (nrt - 1, j))],
        out_specs=[of_spec, of_spec, pl.BlockSpec((tk, tn_first), lambda j, k: (k, j))],
        out_shape=[jax.ShapeDtypeStruct((tm, n), F32), jax.ShapeDtypeStruct((tm, n), BF16),
                   jax.ShapeDtypeStruct((kd, n), BF16)],
        compiler_params=_cparams(("parallel", "arbitrary")),
        name="mm_res_first",
    )(a, w, r)
    if nrt == 1:
        return x_first, xb_first

    def pin(i, v):
        return jnp.where(i == nrt - 1, 0, v)

    o_spec = pl.BlockSpec((tm, tn), lambda i, j, k: (i, j))
    f_spec = pl.BlockSpec((tm, tn), lambda i, j, k: (0, jnp.where(i == nrt - 1, j, 0)))
    return pl.pallas_call(
        _mm_res_rest_kernel,
        grid=(nrt, n // tn, kd // tk),
        in_specs=[pl.BlockSpec((tm, tk), lambda i, j, k: (i, pin(i, k))),
                  pl.BlockSpec((tk, tn), lambda i, j, k: (pin(i, k), pin(i, j))),
                  pl.BlockSpec((tm, tn), lambda i, j, k: (i, pin(i, j))),
                  f_spec, f_spec],
        out_specs=[o_spec, o_spec],
        out_shape=[jax.ShapeDtypeStruct((t, n), F32), jax.ShapeDtypeStruct((t, n), BF16)],
        compiler_params=_cparams(("parallel", "parallel", "arbitrary")),
        name="mm_res",
    )(a, wb, r, x_first, xb_first)


def _out_proj_kernel(a0_ref, a1_ref, a2_ref, a3_ref, w_ref, r_ref, o_ref, ob_ref):
    acc = r_ref[...]
    for g, a_ref in enumerate((a0_ref, a1_ref, a2_ref, a3_ref)):
        acc = acc + jnp.dot(a_ref[...], w_ref[g * GROUP:(g + 1) * GROUP, :], preferred_element_type=F32)
    o_ref[...] = acc
    ob_ref[...] = acc.astype(BF16)


def _out_proj_first_kernel(a0_ref, a1_ref, a2_ref, a3_ref, w32_ref, r_ref, o_ref, ob_ref, wb_ref):
    wb_ref[...] = w32_ref[...].astype(BF16)
    _out_proj_kernel(a0_ref, a1_ref, a2_ref, a3_ref, wb_ref, r_ref, o_ref, ob_ref)


def _out_proj_rest_kernel(a0_ref, a1_ref, a2_ref, a3_ref, w_ref, r_ref, xf_ref, xbf_ref, o_ref, ob_ref):
    last = pl.num_programs(0) - 1

    @pl.when(pl.program_id(0) < last)
    def _():
        _out_proj_kernel(a0_ref, a1_ref, a2_ref, a3_ref, w_ref, r_ref, o_ref, ob_ref)

    @pl.when(pl.program_id(0) == last)
    def _():
        o_ref[...] = xf_ref[...]
        ob_ref[...] = xbf_ref[...]


def _out_proj(mix, w, r, layer, tm, tn, tn_first):
    t = r.shape[0]
    kd, n = w.shape[1], w.shape[2]
    tm = min(tm, t)
    nrt = t // tm
    af_spec = pl.BlockSpec((tm, GROUP), lambda j: (nrt - 1, 0))
    of_spec = pl.BlockSpec((tm, tn_first), lambda j: (0, j))
    x_first, xb_first, wb = pl.pallas_call(
        _out_proj_first_kernel,
        grid=(n // tn_first,),
        in_specs=[af_spec, af_spec, af_spec, af_spec,
                  pl.BlockSpec((None, kd, tn_first), lambda j: (layer, 0, j)),
                  pl.BlockSpec((tm, tn_first), lambda j: (nrt - 1, j))],
        out_specs=[of_spec, of_spec, pl.BlockSpec((kd, tn_first), lambda j: (0, j))],
        out_shape=[jax.ShapeDtypeStruct((tm, n), F32), jax.ShapeDtypeStruct((tm, n), BF16),
                   jax.ShapeDtypeStruct((kd, n), BF16)],
        compiler_params=_cparams(("arbitrary",)),
        name="out_proj_first",
    )(*mix, w, r)
    if nrt == 1:
        return x_first, xb_first
    a_spec = pl.BlockSpec((tm, GROUP), lambda i, j: (i, 0))
    o_spec = pl.BlockSpec((tm, tn), lambda i, j: (i, j))
    f_spec = pl.BlockSpec((tm, tn), lambda i, j: (0, jnp.where(i == nrt - 1, j, 0)))
    return pl.pallas_call(
        _out_proj_rest_kernel,
        grid=(nrt, n // tn),
        in_specs=[a_spec, a_spec, a_spec, a_spec,
                  pl.BlockSpec((kd, tn), lambda i, j: (0, jnp.where(i == nrt - 1, 0, j))),
                  o_spec, f_spec, f_spec],
        out_specs=[o_spec, o_spec],
        out_shape=[jax.ShapeDtypeStruct((t, n), F32), jax.ShapeDtypeStruct((t, n), BF16)],
        compiler_params=_cparams(("parallel", "arbitrary")),
        name="out_proj",
    )(*mix, wb, r, x_first, xb_first)


def _moba_prep_kernel(k_ref, v_ref, g_ref, kn_ref, km_ref, vt_ref):
    g = g_ref[...]
    for h in range(N_HEADS):
        sl = slice(h * HEAD_DIM, (h + 1) * HEAD_DIM)
        kn = _rms_rows(k_ref[:, sl], g)
        kn_ref[:, sl] = kn.astype(BF16)
        for half in range(MOBA_TILE // MOBA_BLOCK):
            km_ref[half, :, sl] = jnp.mean(kn[half * MOBA_BLOCK:(half + 1) * MOBA_BLOCK], axis=0, keepdims=True)
        vt_ref[0, sl, :] = v_ref[:, sl].T.astype(BF16)


def _moba_prep(proj, gk):
    t = proj.shape[0]
    ntile = t // MOBA_TILE
    per = MOBA_TILE // MOBA_BLOCK
    return pl.pallas_call(
        _moba_prep_kernel,
        grid=(ntile,),
        in_specs=[pl.BlockSpec((MOBA_TILE, GROUP), lambda i: (i, CB_MK // 8)),
                  pl.BlockSpec((MOBA_TILE, GROUP), lambda i: (i, CB_MV // 8)),
                  pl.BlockSpec((1, HEAD_DIM), lambda i: (0, 0))],
        out_specs=[pl.BlockSpec((MOBA_TILE, GROUP), lambda i: (i, 0)),
                   pl.BlockSpec((per, 1, GROUP), lambda i: (i, 0, 0)),
                   pl.BlockSpec((1, GROUP, MOBA_TILE), lambda i: (i, 0, 0))],
        out_shape=[jax.ShapeDtypeStruct((t, GROUP), BF16),
                   jax.ShapeDtypeStruct((ntile * per, 1, GROUP), F32),
                   jax.ShapeDtypeStruct((ntile, GROUP, MOBA_TILE), BF16)],
        compiler_params=_cparams(("parallel",)),
        name="moba_prep",
    )(proj, proj, gk)


def _moba_kernel(q_ref, k_ref, vt_ref, km_ref, g_ref, o_ref, bias_ref, acc_ref, qaug_ref, s0_ref, s1_ref,
                 p0_ref, p1_ref, *, nb):
    ti = pl.program_id(2)
    blk, tile = MOBA_BLOCK, MOBA_TILE
    qn = _rms_rows(q_ref[...], g_ref[...])

    gate = lax.dot_general(km_ref[0], qn, (((1,), (1,)), ((), ())),
                           precision=lax.Precision.HIGHEST, preferred_element_type=F32)
    row = lax.broadcasted_iota(jnp.int32, gate.shape, 0)
    own = 2 * ti + (lax.broadcasted_iota(jnp.int32, gate.shape, 1) >= blk).astype(jnp.int32)
    rowf = row.astype(F32)
    gate = jnp.where(row < own, gate, -jnp.inf)
    bias = jnp.where(row == own, 0.0, NEG)
    for _ in range(MOBA_TOPK):
        m = jnp.max(gate, axis=0, keepdims=True)
        idx = jnp.min(jnp.where(gate == m, rowf, float(nb)), axis=0, keepdims=True)
        hit = jnp.logical_and(rowf == idx, m > -jnp.inf)
        bias = jnp.where(hit, 0.0, bias)
        gate = jnp.where(hit, -jnp.inf, gate)
    bias_ref[...] = bias

    qaug_ref[0:HEAD_DIM, :] = (qn * (HEAD_DIM ** -0.5 * LOG2E)).T.astype(BF16)
    qaug_ref[HEAD_DIM + BIAS_ROWS:, :] = jnp.zeros((HEAD_DIM - BIAS_ROWS, tile), BF16)
    er = lax.broadcasted_iota(jnp.int32, (tile, HEAD_DIM), 0)
    ec = lax.broadcasted_iota(jnp.int32, (tile, HEAD_DIM), 1)
    onehot = jnp.where(ec == er // blk, 1.0, 0.0).astype(BF16)
    brow = lax.broadcasted_iota(jnp.int32, (BIAS_ROWS, tile), 0)
    ones_rows = jnp.ones((BIAS_ROWS, tile), BF16)

    def scores(t, valid):
        b0 = jnp.where(valid, bias_ref[pl.ds(2 * t, 1), :], NEG)
        b1 = jnp.where(valid, bias_ref[pl.ds(2 * t + 1, 1), :], NEG)
        qaug_ref[HEAD_DIM:HEAD_DIM + BIAS_ROWS, :] = jnp.where(
            brow == 0, b0, jnp.where(brow == 1, b1, 0.0)).astype(BF16)
        kt = k_ref[pl.ds(pl.multiple_of(t * tile, tile), tile), :]
        return jnp.dot(jnp.concatenate([kt, onehot], axis=1), qaug_ref[...],
                       preferred_element_type=F32)

    s_refs, p_refs = (s0_ref, s1_ref), (p0_ref, p1_ref)
    kr = lax.broadcasted_iota(jnp.int32, (tile, tile), 0)
    qc = lax.broadcasted_iota(jnp.int32, (tile, tile), 1)
    s_refs[0][...] = jnp.where(kr <= qc, scores(ti, True), NEG)
    p_refs[1][...] = jnp.zeros((tile, tile), BF16)
    acc_ref[...] = jnp.zeros_like(acc_ref)

    def trip(n, par, carry):
        m, alpha_prev = carry
        tc = jnp.where(n == 1, ti, jnp.clip(n - 2, 0, ti))
        pv = jnp.dot(jnp.concatenate([vt_ref[tc], ones_rows], axis=0), p_refs[1 - par][...],
                     preferred_element_type=F32)
        s_refs[1 - par][...] = scores(jnp.minimum(n, ti), n < ti)
        s = s_refs[par][...]
        m_new = jnp.maximum(m, jnp.max(s, axis=0, keepdims=True))
        alpha = jnp.exp2(m - m_new)
        p_refs[par][...] = jnp.exp2(s - m_new).astype(BF16)
        acc_ref[...] = acc_ref[...] * alpha_prev + pv
        return m_new, alpha

    def body(j, carry):
        return trip(2 * j + 1, 1, trip(2 * j, 0, carry))

    init = (jnp.full((1, tile), NEG, F32), jnp.ones((1, tile), F32))
    lax.fori_loop(0, (ti + 3) // 2, body, init)
    o_ref[...] = (acc_ref[0:HEAD_DIM, :] / acc_ref[HEAD_DIM:HEAD_DIM + 1, :]).T.astype(o_ref.dtype)


def _moba(proj, kn, km, vt, gq, batch):
    t = proj.shape[0]
    s = t // batch
    nb = s // MOBA_BLOCK
    nt = s // MOBA_TILE
    km = km.reshape(batch, nb, GROUP)
    return pl.pallas_call(
        functools.partial(_moba_kernel, nb=nb),
        grid=(batch, N_HEADS, nt),
        in_specs=[pl.BlockSpec((MOBA_TILE, HEAD_DIM), lambda b, h, i: (b * nt + i, CB_MQ + h)),
                  pl.BlockSpec((s, HEAD_DIM), lambda b, h, i: (b, h)),
                  pl.BlockSpec((nt, HEAD_DIM, MOBA_TILE), lambda b, h, i: (b, h, 0)),
                  pl.BlockSpec((1, nb, HEAD_DIM), lambda b, h, i: (b, 0, h)),
                  pl.BlockSpec((1, HEAD_DIM), lambda b, h, i: (0, 0))],
        out_specs=pl.BlockSpec((MOBA_TILE, HEAD_DIM), lambda b, h, i: (b * nt + i, h)),
        out_shape=jax.ShapeDtypeStruct((t, GROUP), BF16),
        scratch_shapes=[pltpu.VMEM((nb, MOBA_TILE), F32),
                        pltpu.VMEM((HEAD_DIM + BIAS_ROWS, MOBA_TILE), F32),
                        pltpu.VMEM((2 * HEAD_DIM, MOBA_TILE), BF16),
                        pltpu.VMEM((MOBA_TILE, MOBA_TILE), F32), pltpu.VMEM((MOBA_TILE, MOBA_TILE), F32),
                        pltpu.VMEM((MOBA_TILE, MOBA_TILE), BF16), pltpu.VMEM((MOBA_TILE, MOBA_TILE), BF16)],
        compiler_params=_cparams(("parallel", "parallel", "arbitrary")),
        name="moba",
    )(proj, kn, vt, km, gq)


GDN_ROWS = 256
GDN_BASE = 8
GDN_HPS = 8


def _conv_silu(x_ref, halo_ref, w_ref, first):
    r = x_ref.shape[0]
    halo = halo_ref[...] * jnp.where(first, 0.0, 1.0)
    xb = jnp.concatenate([halo, x_ref[...]], axis=0)
    w = w_ref[...]
    out = None
    for tap in range(GDN_CONV):
        sh = GDN_CONV - 1 - tap
        xs = xb if sh == 0 else pltpu.roll(xb, sh, axis=0)
        term = xs[8:8 + r] * w[tap:tap + 1, :]
        out = term if out is None else out + term
    return _silu(out)


def _gdn_heads(qs, ks, vs, gs, betas, states):
    r = qs[0].shape[0]
    c = GDN_CHUNK
    pair = 2 * c
    npair = r // pair
    nh = len(qs)
    row = lax.broadcasted_iota(jnp.int32, (pair, pair), 0)
    col = lax.broadcasted_iota(jnp.int32, (pair, pair), 1)
    same = (row // c) == (col // c)
    tril = jnp.logical_and(same, row >= col)
    strict = jnp.logical_and(same, row > col)
    eye = (row == col).astype(F32)
    rin = row % c
    units = [(h, pi) for pi in range(npair) for h in range(nh)]

    def rows(x, u):
        return x[u[0]][u[1] * pair:(u[1] + 1) * pair]

    gcum = {u: rows(gs, u) for u in units}
    decay = {u: jnp.exp(jnp.where(tril, gcum[u] - gcum[u].T, -jnp.inf)) for u in units}
    eg = {u: jnp.exp(gcum[u]) for u in units}
    g_end = {u: (gcum[u][c - 1:c, :], gcum[u][pair - 1:pair, :]) for u in units}
    kb = {u: rows(ks, u) * rows(betas, u) for u in units}
    vb = {u: rows(vs, u) * rows(betas, u) for u in units}
    lmat = {u: jnp.where(strict, _bdot_nt(kb[u], rows(ks, u)) * decay[u], 0.0) for u in units}
    qk = {u: _bdot_nt(rows(qs, u), rows(ks, u)) * decay[u] for u in units}
    def blocks(size):
        return (row // size) == (col // size)

    lbase = {u: jnp.where(blocks(GDN_BASE), lmat[u], 0.0) for u in units}
    tinv = {u: eye - lbase[u] for u in units}
    lpow = lbase
    span = 1
    while 2 * span < GDN_BASE:
        lpow = {u: _bdot(lpow[u], lpow[u]) for u in units}
        tinv = {u: tinv[u] + _bdot(tinv[u], lpow[u]) for u in units}
        span *= 2
    size = GDN_BASE
    while size < c:
        below = jnp.logical_and(blocks(2 * size), jnp.logical_not(blocks(size)))
        tc = {u: _bdot(tinv[u], jnp.where(below, lmat[u], 0.0)) for u in units}
        tinv = {u: tinv[u] - _bdot(tc[u], tinv[u]) for u in units}
        size *= 2
    uw = {u: _bdot(tinv[u], jnp.concatenate([vb[u], kb[u] * eg[u]], axis=1)) for u in units}
    qd = {u: rows(qs, u) * eg[u] for u in units}
    kdt = {u: (rows(ks, u) * jnp.exp(jnp.where(row < c, g_end[u][0], g_end[u][1]) - gcum[u])).T for u in units}

    states = list(states)
    vns = {u: [] for u in units}
    o_st = {u: [] for u in units}
    for pi in range(npair):
        for ci in range(2):
            cs = slice(ci * c, (ci + 1) * c)
            for h in range(nh):
                u = (h, pi)
                ws = _bdot(jnp.concatenate([uw[u][cs, HEAD_DIM:], qd[u][cs]], axis=0), states[h])
                vn = uw[u][cs, :HEAD_DIM] - ws[:c]
                o_st[u].append(ws[c:])
                vns[u].append(vn)
                zero = jnp.zeros_like(vn)
                vn_pad = jnp.concatenate([vn, zero] if ci == 0 else [zero, vn], axis=0)
                states[h] = states[h] * jnp.exp(g_end[u][ci]) + _bdot(kdt[u], vn_pad)
    outs = []
    for h in range(nh):
        parts = [jnp.concatenate(o_st[(h, pi)], axis=0) + _bdot(qk[(h, pi)], jnp.concatenate(vns[(h, pi)], axis=0))
                 for pi in range(npair)]
        outs.append(jnp.concatenate(parts, axis=0))
    return outs, states


def _gdn_kernel(q_ref, k_ref, v_ref, qh_ref, kh_ref, vh_ref, wq_ref, wk_ref, wv_ref,
                gab_ref, z_ref, hp_ref, gn_ref, o_ref, s_ref):
    hg = pl.program_id(1)
    first = pl.program_id(2) == 0

    @pl.when(first)
    def _():
        s_ref[...] = jnp.zeros_like(s_ref)

    q2 = _conv_silu(q_ref, qh_ref, wq_ref, first)
    k2 = _conv_silu(k_ref, kh_ref, wk_ref, first)
    v2 = _conv_silu(v_ref, vh_ref, wv_ref, first)
    gab = gab_ref[...]
    lane = lax.broadcasted_iota(jnp.int32, gab.shape, 1)
    xg = gab + hp_ref[1:2, :]
    softplus = jnp.maximum(xg, 0.0) + jnp.log1p(jnp.exp(-jnp.abs(xg)))
    gcum_all = -jnp.exp(hp_ref[0:1, :]) * softplus
    beta_all = _sigmoid(gab)
    rin = lax.broadcasted_iota(jnp.int32, gab.shape, 0) % GDN_CHUNK
    sh = 1
    while sh < GDN_CHUNK:
        gcum_all = gcum_all + jnp.where(rin >= sh, pltpu.roll(gcum_all, sh, axis=0), 0.0)
        sh *= 2
    zeros = jnp.zeros_like(gab)

    qs, ks, vs, gs, betas = [], [], [], [], []
    for hh in range(GDN_HPS):
        h = hg * GDN_HPS + hh
        sl = slice(hh * HEAD_DIM, (hh + 1) * HEAD_DIM)
        q, k = q2[:, sl], k2[:, sl]
        qs.append(q * lax.rsqrt(jnp.sum(q * q, axis=-1, keepdims=True) + EPS) * (HEAD_DIM ** -0.5))
        ks.append(k * lax.rsqrt(jnp.sum(k * k, axis=-1, keepdims=True) + EPS))
        vs.append(v2[:, sl])
        gs.append(jnp.sum(jnp.where(lane == h, gcum_all, 0.0), axis=1, keepdims=True) + zeros)
        betas.append(jnp.sum(jnp.where(lane == N_HEADS + h, beta_all, 0.0), axis=1, keepdims=True) + zeros)

    outs, states = _gdn_heads(qs, ks, vs, gs, betas, [s_ref[hh] for hh in range(GDN_HPS)])
    for hh in range(GDN_HPS):
        sl = slice(hh * HEAD_DIM, (hh + 1) * HEAD_DIM)
        s_ref[hh] = states[hh]
        on = _rms_rows(outs[hh], gn_ref[...])
        o_ref[:, sl] = (on * _silu(z_ref[:, sl])).astype(o_ref.dtype)


def _gdn(proj, conv_w, hp, gn, batch):
    t = proj.shape[0]
    s = t // batch
    r = GDN_ROWS
    steps = s // r
    hb = r // 8
    wide = GDN_HPS * HEAD_DIM
    cpb = GDN_HPS

    def main(cb):
        return pl.BlockSpec((r, wide), lambda b, h, i: (b * steps + i, cb // cpb + h))

    def halo(cb):
        return pl.BlockSpec((8, wide), lambda b, h, i: (jnp.maximum((b * steps + i) * hb - 1, 0), cb // cpb + h))

    def wspec(off):
        return pl.BlockSpec((GDN_CONV, wide), lambda b, h, i: (0, off // cpb + h))

    return pl.pallas_call(
        _gdn_kernel,
        grid=(batch, N_HEADS // GDN_HPS, steps),
        in_specs=[main(CB_GQ), main(CB_GK), main(CB_GV), halo(CB_GQ), halo(CB_GK), halo(CB_GV),
                  wspec(0), wspec(N_HEADS), wspec(2 * N_HEADS),
                  pl.BlockSpec((r, LANES), lambda b, h, i: (b * steps + i, CB_GAB)),
                  main(CB_GZ),
                  pl.BlockSpec((2, LANES), lambda b, h, i: (0, 0)),
                  pl.BlockSpec((1, HEAD_DIM), lambda b, h, i: (0, 0))],
        out_specs=pl.BlockSpec((r, wide), lambda b, h, i: (b * steps + i, h)),
        out_shape=jax.ShapeDtypeStruct((t, GROUP), BF16),
        scratch_shapes=[pltpu.VMEM((GDN_HPS, HEAD_DIM, HEAD_DIM), F32)],
        compiler_params=_cparams(("parallel", "parallel", "arbitrary")),
        name="gdn",
    )(proj, proj, proj, proj, proj, proj, conv_w, conv_w, conv_w, proj, proj, hp, gn)


SC_ROWS = 512


def _sconv_kernel(b_ref, c_ref, x_ref, ch_ref, xh_ref, w_ref, o_ref, *, steps):
    first = pl.program_id(0) % steps == 0
    r = b_ref.shape[0]
    y = c_ref[...] * x_ref[...]
    yh = ch_ref[...] * xh_ref[...] * jnp.where(first, 0.0, 1.0)
    yb = jnp.concatenate([yh, y], axis=0)
    w = w_ref[...]
    out = None
    for tap in range(SC_CONV):
        sh = SC_CONV - 1 - tap
        ys = yb if sh == 0 else pltpu.roll(yb, sh, axis=0)
        term = ys[8:8 + r] * w[tap:tap + 1, :]
        out = term if out is None else out + term
    o_ref[...] = (b_ref[...] * out).astype(o_ref.dtype)


def _sconv(proj, w, batch):
    t = proj.shape[0]
    r = min(SC_ROWS, t // batch)
    steps = (t // batch) // r
    hb = r // 8

    def main(cb):
        return pl.BlockSpec((r, GROUP), lambda i: (i, cb // 8))

    def halo(cb):
        return pl.BlockSpec((8, GROUP), lambda i: (jnp.maximum(i * hb - 1, 0), cb // 8))

    return pl.pallas_call(
        functools.partial(_sconv_kernel, steps=steps),
        grid=(t // r,),
        in_specs=[main(CB_SCB), main(CB_SCC), main(CB_SCX), halo(CB_SCC), halo(CB_SCX),
                  pl.BlockSpec((SC_CONV, GROUP), lambda i: (0, 0))],
        out_specs=pl.BlockSpec((r, GROUP), lambda i: (i, 0)),
        out_shape=jax.ShapeDtypeStruct((t, GROUP), BF16),
        compiler_params=_cparams(("parallel",)),
        name="sconv",
    )(proj, proj, proj, proj, proj, w)


SWA_ROWS = 512


def _half_rms(x, g2):
    lane = lax.broadcasted_iota(jnp.int32, x.shape, 1)
    lo = lane < SWA_D
    x2 = x * x
    ms_lo = jnp.sum(jnp.where(lo, x2, 0.0), axis=-1, keepdims=True) * (1.0 / SWA_D)
    ms_hi = jnp.sum(jnp.where(lo, 0.0, x2), axis=-1, keepdims=True) * (1.0 / SWA_D)
    rs = jnp.where(lo, lax.rsqrt(ms_lo + EPS), lax.rsqrt(ms_hi + EPS))
    return x * rs * g2


def _swa_kernel(q_ref, k_ref, v_ref, kh_ref, vh_ref, gq_ref, gk_ref, sink_ref, o_ref, *, steps):
    first = pl.program_id(0) % steps == 0
    r = q_ref.shape[0]
    w = SWA_W
    nsub = r // w
    pairs = SWA_Q_HEADS // 2
    lane = lax.broadcasted_iota(jnp.int32, (r + w, LANES), 1)

    kn = _half_rms(jnp.concatenate([kh_ref[...], k_ref[...]], axis=0), gk_ref[...])
    kroll = pltpu.roll(kn, SWA_D, axis=1)
    kdup = (jnp.where(lane < SWA_D, kn, kroll).astype(BF16),
            jnp.where(lane < SWA_D, kroll, kn).astype(BF16))
    vt = jnp.concatenate([vh_ref[...], v_ref[...]], axis=0).T.astype(BF16)

    kr = lax.broadcasted_iota(jnp.int32, (2 * w, 2 * w), 0)
    qc = lax.broadcasted_iota(jnp.int32, (2 * w, 2 * w), 1) % w
    band = jnp.logical_and(kr > qc, kr <= qc + w)
    band0 = jnp.logical_and(band, kr >= jnp.where(first, w, 0))
    qlane = lax.broadcasted_iota(jnp.int32, (w, LANES), 1)

    heads = range(pairs)
    kvh = [c // (pairs // SWA_KV_HEADS) for c in heads]
    for sub in range(nsub):
        mask = band0 if sub == 0 else band
        ks = slice(sub * w, sub * w + 2 * w)
        rows = slice(sub * w, (sub + 1) * w)
        qts = []
        for c in heads:
            qn = _half_rms(q_ref[rows, c * LANES:(c + 1) * LANES], gq_ref[...]) * (SWA_D ** -0.5 * LOG2E)
            qa = jnp.where(qlane < SWA_D, qn, 0.0)
            qb = jnp.where(qlane < SWA_D, 0.0, qn)
            qts.append(jnp.concatenate([qa.T, qb.T], axis=1).astype(BF16))
        ss = [jnp.dot(kdup[kvh[c]][ks], qts[c], preferred_element_type=F32) for c in heads]
        pns = []
        for c in heads:
            s = jnp.where(mask, ss[c], NEG)
            sink = sink_ref[c:c + 1, :] * LOG2E
            m = jnp.maximum(jnp.max(s, axis=0, keepdims=True), sink)
            p = jnp.exp2(s - m)
            l = jnp.sum(p, axis=0, keepdims=True) + jnp.exp2(sink - m)
            pns.append((p * (1.0 / l)).astype(BF16))
        ots = [jnp.dot(vt[kvh[c] * SWA_D:(kvh[c] + 1) * SWA_D, ks], pns[c], preferred_element_type=F32)
               for c in heads]
        for c in heads:
            o = jnp.concatenate([ots[c][:, :w], ots[c][:, w:]], axis=0).T
            o_ref[rows, c * LANES:(c + 1) * LANES] = o.astype(o_ref.dtype)


def _swa(proj, gq2, gk2, sinkrow, batch):
    t = proj.shape[0]
    r = min(SWA_ROWS, t // batch)
    steps = (t // batch) // r
    hb = r // SWA_W

    def halo(cb):
        return pl.BlockSpec((SWA_W, LANES), lambda i: (jnp.maximum(i * hb - 1, 0), cb))

    return pl.pallas_call(
        functools.partial(_swa_kernel, steps=steps),
        grid=(t // r,),
        in_specs=[pl.BlockSpec((r, GROUP), lambda i: (i, CB_SQ // 8)),
                  pl.BlockSpec((r, LANES), lambda i: (i, CB_SK)),
                  pl.BlockSpec((r, LANES), lambda i: (i, CB_SV)),
                  halo(CB_SK), halo(CB_SV),
                  pl.BlockSpec((1, LANES), lambda i: (0, 0)),
                  pl.BlockSpec((1, LANES), lambda i: (0, 0)),
                  pl.BlockSpec((SWA_Q_HEADS // 2, 2 * SWA_W), lambda i: (0, 0))],
        out_specs=pl.BlockSpec((r, GROUP), lambda i: (i, 0)),
        out_shape=jax.ShapeDtypeStruct((t, GROUP), BF16),
        compiler_params=_cparams(("parallel",)),
        name="swa",
    )(proj, proj, proj, proj, proj, gq2, gk2, sinkrow)


RG_ROWS = 256
AB_ROWS = 2 * N_HEADS


def _regroup_kernel(x_ref, nx_ref, ab_ref, g_ref, o_ref, *, head_blocks, tail_blocks):
    j = pl.program_id(1)
    g = g_ref[...]

    @pl.when(j < head_blocks)
    def _():
        o_ref[...] = (x_ref[...] * g).astype(o_ref.dtype)

    @pl.when(jnp.logical_and(j >= head_blocks, j < head_blocks + tail_blocks))
    def _():
        o_ref[...] = (jnp.concatenate([x_ref[AB_ROWS:, :], nx_ref[...]], axis=0) * g).astype(o_ref.dtype)

    @pl.when(j == head_blocks + tail_blocks)
    def _():
        zeros = jnp.zeros((o_ref.shape[0] - AB_ROWS, o_ref.shape[1]), F32)
        o_ref[...] = jnp.concatenate([ab_ref[...] * g, zeros], axis=0).astype(o_ref.dtype)


def _regroup_w_in(w_in, gain):
    wt = jnp.swapaxes(w_in, 1, 2)
    nl, n_real, d = wt.shape
    head_blocks = A_COLS // RG_ROWS
    tail_blocks = (n_real - A_COLS - AB_ROWS) // RG_ROWS
    assert A_COLS + AB_ROWS + tail_blocks * RG_ROWS == n_real and (head_blocks + tail_blocks + 1) * RG_ROWS == NP_COLS
    per = RG_ROWS // AB_ROWS
    last_ab = n_real // AB_ROWS - 1
    return pl.pallas_call(
        functools.partial(_regroup_kernel, head_blocks=head_blocks, tail_blocks=tail_blocks),
        grid=(nl, NP_COLS // RG_ROWS),
        in_specs=[pl.BlockSpec((None, RG_ROWS, d), lambda l, j: (l, jnp.minimum(j, head_blocks + tail_blocks - 1), 0)),
                  pl.BlockSpec((None, AB_ROWS, d), lambda l, j: (l, jnp.minimum(per * (j + 1), last_ab), 0)),
                  pl.BlockSpec((None, AB_ROWS, d), lambda l, j: (l, A_COLS // AB_ROWS, 0)),
                  pl.BlockSpec((None, 1, d), lambda l, j: (l, 0, 0))],
        out_specs=pl.BlockSpec((None, RG_ROWS, d), lambda l, j: (l, j, 0)),
        out_shape=jax.ShapeDtypeStruct((nl, NP_COLS, d), BF16),
        compiler_params=_cparams(("parallel", "parallel")),
        name="regroup_w_in",
    )(wt, wt, wt, gain.astype(F32)[:, None, :])


def _layer(x, xb, batch, layer, w_in, moba_q_norm, moba_k_norm, gdn_conv, gdn_a_log, gdn_dt_bias,
           gdn_out_norm, sc_conv, swa_q_norm, swa_k_norm, swa_sinks, w_out, ffn_gain, w_gate, w_up, w_down):
    row = lambda a: a.reshape(1, -1).astype(F32)

    proj = _in_proj(xb, w_in, layer, tm=2048, tn=512)

    kn, km, vt = _moba_prep(proj, row(moba_k_norm))
    o_a = _moba(proj, kn, km, vt, row(moba_q_norm), batch)

    hp = jnp.pad(jnp.stack([gdn_a_log, gdn_dt_bias]).astype(F32), ((0, 0), (0, LANES - N_HEADS)))
    o_b = _gdn(proj, gdn_conv.astype(F32), hp, row(gdn_out_norm), batch)

    o_c = _sconv(proj, sc_conv.astype(F32), batch)

    sinkrow = jnp.repeat(swa_sinks.astype(F32), SWA_W).reshape(SWA_Q_HEADS // 2, 2 * SWA_W)
    o_d = _swa(proj, row(jnp.tile(swa_q_norm, 2)), row(jnp.tile(swa_k_norm, 2)), sinkrow, batch)

    x, xb = _out_proj((o_a, o_b, o_c, o_d), w_out, x, layer, tm=1024, tn=512, tn_first=256)

    act = _ffn_up(xb, ffn_gain, w_gate, w_up, layer, D_FF_PAD, tm=1024, tn=512, tn_first=256)
    return _mm_res(act, w_down, x, layer, tm=1024, tn=1024, tk=D_FF_PAD // 4, tn_first=512)


def kernel(x, norm_mix, w_in, moba_q_norm, moba_k_norm, gdn_conv, gdn_a_log, gdn_dt_bias, gdn_out_norm, sc_conv, swa_q_norm, swa_k_norm, swa_sinks, w_out, norm_ffn, w_gate, w_up, w_down):
    batch, seq, d = x.shape
    w_in_b = _regroup_w_in(w_in, norm_mix)
    ffn_gain = norm_ffn.astype(F32)[:, :, None]
    h = x.reshape(batch * seq, d)
    hb = h.astype(BF16)
    for l in range(norm_mix.shape[0]):
        h, hb = _layer(h, hb, batch, l, w_in_b, moba_q_norm[l], moba_k_norm[l], gdn_conv[l], gdn_a_log[l],
                       gdn_dt_bias[l], gdn_out_norm[l], sc_conv[l], swa_q_norm[l], swa_k_norm[l], swa_sinks[l],
                       w_out, ffn_gain, w_gate, w_up, w_down)
    return h.reshape(batch, seq, d)
```

```python
import functools

import jax
import jax.numpy as jnp
from jax import lax
from jax.experimental import pallas as pl
from jax.experimental.pallas import tpu as pltpu

F32 = jnp.float32
BF16 = jnp.bfloat16

EPS = 1e-6
LANES = 128
GROUP = 1024
HEAD_DIM = 128
N_HEADS = GROUP // HEAD_DIM
MOBA_BLOCK = 256
MOBA_TILE = 2 * MOBA_BLOCK
MOBA_TOPK = 3
GDN_CONV = 4
GDN_CHUNK = 64
SC_CONV = 3
SWA_D = 64
SWA_Q_HEADS = GROUP // SWA_D
SWA_KV_HEADS = 2
SWA_W = 128
NEG = -1e30
LOG2E = 1.4426950408889634
BIAS_ROWS = 16

CB_MQ, CB_MK, CB_MV = 0, 8, 16
CB_GQ, CB_GK, CB_GV, CB_GZ = 24, 32, 40, 48
CB_SCB, CB_SCC, CB_SCX = 56, 64, 72
CB_SQ, CB_SK, CB_SV, CB_GAB = 80, 88, 89, 90
NP_COLS = 92 * LANES
A_COLS = 6 * GROUP
D_FF_PAD = 11264

VMEM_LIMIT = 56 * 1024 * 1024


def _cparams(sem, vmem=VMEM_LIMIT):
    return pltpu.CompilerParams(dimension_semantics=sem, vmem_limit_bytes=vmem)


def _bdot(a, b):
    return jnp.dot(a.astype(BF16), b.astype(BF16), preferred_element_type=F32)


def _bdot_nt(a, b):
    return lax.dot_general(a.astype(BF16), b.astype(BF16), (((1,), (1,)), ((), ())),
                           preferred_element_type=F32)


def _sigmoid(x):
    return 1.0 / (1.0 + jnp.exp(-x))


def _silu(x):
    return x * _sigmoid(x)


def _rms_rows(x, g):
    ms = jnp.mean(x * x, axis=-1, keepdims=True)
    return x * lax.rsqrt(ms + EPS) * g


NORM_CHUNK = 256


def _row_scale_to(xb_ref, rs_ref):
    chunk = min(NORM_CHUNK, xb_ref.shape[0])

    def body(c, carry):
        rows = pl.ds(pl.multiple_of(c * chunk, chunk), chunk)
        x = xb_ref[rows, :].astype(F32)
        ms = jnp.mean(x * x, axis=-1, keepdims=True)
        rs_ref[rows, :] = jnp.broadcast_to(lax.rsqrt(ms + EPS), (chunk, LANES))
        return carry

    lax.fori_loop(0, xb_ref.shape[0] // chunk, body, 0)


def _scaled_dot(xb_ref, w_ref, rs_ref, w_transposed=False):
    dims = (((1,), (1,)), ((), ())) if w_transposed else (((1,), (0,)), ((), ()))
    acc = lax.dot_general(xb_ref[...], w_ref[...], dims, preferred_element_type=F32)
    rs = rs_ref[...]
    return jnp.concatenate([acc[:, c * LANES:(c + 1) * LANES] * rs for c in range(acc.shape[1] // LANES)], axis=1)


def _in_proj_kernel(xb_ref, w_ref, o_ref, rs_ref):
    @pl.when(pl.program_id(1) == 0)
    def _():
        _row_scale_to(xb_ref, rs_ref)

    o_ref[...] = _scaled_dot(xb_ref, w_ref, rs_ref, w_transposed=True).astype(o_ref.dtype)


def _in_proj(xb, w, layer, tm, tn):
    t, d = xb.shape
    n = w.shape[1]
    tm = min(tm, t)
    return pl.pallas_call(
        _in_proj_kernel,
        grid=(t // tm, n // tn),
        in_specs=[pl.BlockSpec((tm, d), lambda i, j: (i, 0)),
                  pl.BlockSpec((None, tn, d), lambda i, j: (layer, j, 0))],
        out_specs=pl.BlockSpec((tm, tn), lambda i, j: (i, j)),
        out_shape=jax.ShapeDtypeStruct((t, n), F32),
        scratch_shapes=[pltpu.VMEM((tm, LANES), F32)],
        compiler_params=_cparams(("parallel", "arbitrary")),
        name="in_proj",
    )(xb, w)


def _ffn_up_kernel(xb_ref, wg_ref, wu_ref, o_ref, rs_ref):
    @pl.when(pl.program_id(1) == 0)
    def _():
        _row_scale_to(xb_ref, rs_ref)

    a = _scaled_dot(xb_ref, wg_ref, rs_ref)
    b = _scaled_dot(xb_ref, wu_ref, rs_ref)
    o_ref[...] = (_silu(a) * b).astype(o_ref.dtype)


def _ffn_up_first_kernel(xb_ref, g_ref, wg32_ref, wu32_ref, o_ref, wgb_ref, wub_ref, rs_ref, *, valid):
    j = pl.program_id(0)

    @pl.when(j == 0)
    def _():
        _row_scale_to(xb_ref, rs_ref)

    col = j * wg32_ref.shape[1] + lax.broadcasted_iota(jnp.int32, wg32_ref.shape, 1)
    wgb_ref[...] = jnp.where(col < valid, wg32_ref[...] * g_ref[...], 0.0).astype(BF16)
    wub_ref[...] = jnp.where(col < valid, wu32_ref[...] * g_ref[...], 0.0).astype(BF16)
    a = _scaled_dot(xb_ref, wgb_ref, rs_ref)
    b = _scaled_dot(xb_ref, wub_ref, rs_ref)
    o_ref[...] = (_silu(a) * b).astype(o_ref.dtype)


def _ffn_up_rest_kernel(xb_ref, wg_ref, wu_ref, first_ref, o_ref, rs_ref):
    last = pl.num_programs(0) - 1

    @pl.when(pl.program_id(0) < last)
    def _():
        _ffn_up_kernel(xb_ref, wg_ref, wu_ref, o_ref, rs_ref)

    @pl.when(pl.program_id(0) == last)
    def _():
        o_ref[...] = first_ref[...]


def _ffn_up(xb, gain, w_gate, w_up, layer, n_pad, tm, tn, tn_first):
    t, d = xb.shape
    n_real = w_gate.shape[2]
    tm = min(tm, t)
    nrt = t // tm
    last_blk = (n_real - 1) // tn_first
    w32_spec = pl.BlockSpec((None, d, tn_first), lambda j: (layer, 0, jnp.minimum(j, last_blk)))
    wb_spec = pl.BlockSpec((d, tn_first), lambda j: (0, j))
    act_first, wgb, wub = pl.pallas_call(
        functools.partial(_ffn_up_first_kernel, valid=n_real),
        grid=(n_pad // tn_first,),
        in_specs=[pl.BlockSpec((tm, d), lambda j: (nrt - 1, 0)),
                  pl.BlockSpec((None, d, 1), lambda j: (layer, 0, 0)),
                  w32_spec, w32_spec],
        out_specs=[pl.BlockSpec((tm, tn_first), lambda j: (0, j)), wb_spec, wb_spec],
        out_shape=[jax.ShapeDtypeStruct((tm, n_pad), BF16),
                   jax.ShapeDtypeStruct((d, n_pad), BF16), jax.ShapeDtypeStruct((d, n_pad), BF16)],
        scratch_shapes=[pltpu.VMEM((tm, LANES), F32)],
        compiler_params=_cparams(("arbitrary",)),
        name="ffn_up_first",
    )(xb, gain, w_gate, w_up)
    if nrt == 1:
        return act_first
    w_spec = pl.BlockSpec((d, tn), lambda i, j: (0, jnp.where(i == nrt - 1, 0, j)))
    return pl.pallas_call(
        _ffn_up_rest_kernel,
        grid=(nrt, n_pad // tn),
        in_specs=[pl.BlockSpec((tm, d), lambda i, j: (i, 0)), w_spec, w_spec,
                  pl.BlockSpec((tm, tn), lambda i, j: (0, jnp.where(i == nrt - 1, j, 0)))],
        out_specs=pl.BlockSpec((tm, tn), lambda i, j: (i, j)),
        out_shape=jax.ShapeDtypeStruct((t, n_pad), BF16),
        scratch_shapes=[pltpu.VMEM((tm, LANES), F32)],
        compiler_params=_cparams(("parallel", "arbitrary")),
        name="ffn_up",
    )(xb, wgb, wub, act_first)


def _mm_res_kernel(a_ref, w_ref, r_ref, o_ref, ob_ref):
    k = pl.program_id(2)

    @pl.when(k == 0)
    def _():
        o_ref[...] = r_ref[...] + jnp.dot(a_ref[...], w_ref[...], preferred_element_type=F32)

    @pl.when(k > 0)
    def _():
        o_ref[...] += jnp.dot(a_ref[...], w_ref[...], preferred_element_type=F32)

    @pl.when(k == pl.num_programs(2) - 1)
    def _():
        ob_ref[...] = o_ref[...].astype(BF16)


def _mm_res_first_kernel(a_ref, w32_ref, r_ref, o_ref, ob_ref, wb_ref, *, valid):
    k = pl.program_id(1)
    row = k * w32_ref.shape[0] + lax.broadcasted_iota(jnp.int32, w32_ref.shape, 0)
    wb_ref[...] = jnp.where(row < valid, w32_ref[...], 0.0).astype(BF16)

    @pl.when(k == 0)
    def _():
        o_ref[...] = r_ref[...] + jnp.dot(a_ref[...], wb_ref[...], preferred_element_type=F32)

    @pl.when(k > 0)
    def _():
        o_ref[...] += jnp.dot(a_ref[...], wb_ref[...], preferred_element_type=F32)

    @pl.when(k == pl.num_programs(1) - 1)
    def _():
        ob_ref[...] = o_ref[...].astype(BF16)


def _mm_res_rest_kernel(a_ref, w_ref, r_ref, xf_ref, xbf_ref, o_ref, ob_ref):
    last = pl.num_programs(0) - 1

    @pl.when(pl.program_id(0) < last)
    def _():
        _mm_res_kernel(a_ref, w_ref, r_ref, o_ref, ob_ref)

    @pl.when(jnp.logical_and(pl.program_id(0) == last, pl.program_id(2) == 0))
    def _():
        o_ref[...] = xf_ref[...]
        ob_ref[...] = xbf_ref[...]


def _mm_res(a, w, r, layer, tm, tn, tk, tn_first):
    t, kd = a.shape
    k_real, n = w.shape[1], w.shape[2]
    tm = min(tm, t)
    nrt = t // tm
    last_blk = (k_real - 1) // tk
    of_spec = pl.BlockSpec((tm, tn_first), lambda j, k: (0, j))
    x_first, xb_first, wb = pl.pallas_call(
        functools.partial(_mm_res_first_kernel, valid=k_real),
        grid=(n // tn_first, kd // tk),
        in_specs=[pl.BlockSpec((tm, tk), lambda j, k: (nrt - 1, k)),
                  pl.BlockSpec((None, tk, tn_first), lambda j, k: (layer, jnp.minimum(k, last_blk), j)),
                  pl.BlockSpec((tm, tn_first), lambda j, k: (nrt - 1, j))],
        out_specs=[of_spec, of_spec, pl.BlockSpec((tk, tn_first), lambda j, k: (k, j))],
        out_shape=[jax.ShapeDtypeStruct((tm, n), F32), jax.ShapeDtypeStruct((tm, n), BF16),
                   jax.ShapeDtypeStruct((kd, n), BF16)],
        compiler_params=_cparams(("parallel", "arbitrary")),
        name="mm_res_first",
    )(a, w, r)
    if nrt == 1:
        return x_first, xb_first

    def pin(i, v):
        return jnp.where(i == nrt - 1, 0, v)

    o_spec = pl.BlockSpec((tm, tn), lambda i, j, k: (i, j))
    f_spec = pl.BlockSpec((tm, tn), lambda i, j, k: (0, jnp.where(i == nrt - 1, j, 0)))
    return pl.pallas_call(
        _mm_res_rest_kernel,
        grid=(nrt, n // tn, kd // tk),
        in_specs=[pl.BlockSpec((tm, tk), lambda i, j, k: (i, pin(i, k))),
                  pl.BlockSpec((tk, tn), lambda i, j, k: (pin(i, k), pin(i, j))),
                  pl.BlockSpec((tm, tn), lambda i, j, k: (i, pin(i, j))),
                  f_spec, f_spec],
        out_specs=[o_spec, o_spec],
        out_shape=[jax.ShapeDtypeStruct((t, n), F32), jax.ShapeDtypeStruct((t, n), BF16)],
        compiler_params=_cparams(("parallel", "parallel", "arbitrary")),
        name="mm_res",
    )(a, wb, r, x_first, xb_first)


def _out_proj_kernel(a0_ref, a1_ref, a2_ref, a3_ref, w_ref, r_ref, o_ref, ob_ref):
    acc = r_ref[...]
    for g, a_ref in enumerate((a0_ref, a1_ref, a2_ref, a3_ref)):
        acc = acc + jnp.dot(a_ref[...], w_ref[g * GROUP:(g + 1) * GROUP, :], preferred_element_type=F32)
    o_ref[...] = acc
    ob_ref[...] = acc.astype(BF16)


def _out_proj_first_kernel(a0_ref, a1_ref, a2_ref, a3_ref, w32_ref, r_ref, o_ref, ob_ref, wb_ref):
    wb_ref[...] = w32_ref[...].astype(BF16)
    _out_proj_kernel(a0_ref, a1_ref, a2_ref, a3_ref, wb_ref, r_ref, o_ref, ob_ref)


def _out_proj_rest_kernel(a0_ref, a1_ref, a2_ref, a3_ref, w_ref, r_ref, xf_ref, xbf_ref, o_ref, ob_ref):
    last = pl.num_programs(0) - 1

    @pl.when(pl.program_id(0) < last)
    def _():
        _out_proj_kernel(a0_ref, a1_ref, a2_ref, a3_ref, w_ref, r_ref, o_ref, ob_ref)

    @pl.when(pl.program_id(0) == last)
    def _():
        o_ref[...] = xf_ref[...]
        ob_ref[...] = xbf_ref[...]


def _out_proj(mix, w, r, layer, tm, tn, tn_first):
    t = r.shape[0]
    kd, n = w.shape[1], w.shape[2]
    tm = min(tm, t)
    nrt = t // tm
    af_spec = pl.BlockSpec((tm, GROUP), lambda j: (nrt - 1, 0))
    of_spec = pl.BlockSpec((tm, tn_first), lambda j: (0, j))
    x_first, xb_first, wb = pl.pallas_call(
        _out_proj_first_kernel,
        grid=(n // tn_first,),
        in_specs=[af_spec, af_spec, af_spec, af_spec,
                  pl.BlockSpec((None, kd, tn_first), lambda j: (layer, 0, j)),
                  pl.BlockSpec((tm, tn_first), lambda j: (nrt - 1, j))],
        out_specs=[of_spec, of_spec, pl.BlockSpec((kd, tn_first), lambda j: (0, j))],
        out_shape=[jax.ShapeDtypeStruct((tm, n), F32), jax.ShapeDtypeStruct((tm, n), BF16),
                   jax.ShapeDtypeStruct((kd, n), BF16)],
        compiler_params=_cparams(("arbitrary",)),
        name="out_proj_first",
    )(*mix, w, r)
    if nrt == 1:
        return x_first, xb_first
    a_spec = pl.BlockSpec((tm, GROUP), lambda i, j: (i, 0))
    o_spec = pl.BlockSpec((tm, tn), lambda i, j: (i, j))
    f_spec = pl.BlockSpec((tm, tn), lambda i, j: (0, jnp.where(i == nrt - 1, j, 0)))
    return pl.pallas_call(
        _out_proj_rest_kernel,
        grid=(nrt, n // tn),
        in_specs=[a_spec, a_spec, a_spec, a_spec,
                  pl.BlockSpec((kd, tn), lambda i, j: (0, jnp.where(i == nrt - 1, 0, j))),
                  o_spec, f_spec, f_spec],
        out_specs=[o_spec, o_spec],
        out_shape=[jax.ShapeDtypeStruct((t, n), F32), jax.ShapeDtypeStruct((t, n), BF16)],
        compiler_params=_cparams(("parallel", "arbitrary")),
        name="out_proj",
    )(*mix, wb, r, x_first, xb_first)


def _moba_prep_kernel(k_ref, v_ref, g_ref, kn_ref, km_ref, vt_ref):
    g = g_ref[...]
    for h in range(N_HEADS):
        sl = slice(h * HEAD_DIM, (h + 1) * HEAD_DIM)
        kn = _rms_rows(k_ref[:, sl], g)
        kn_ref[:, sl] = kn.astype(BF16)
        for half in range(MOBA_TILE // MOBA_BLOCK):
            km_ref[half, :, sl] = jnp.mean(kn[half * MOBA_BLOCK:(half + 1) * MOBA_BLOCK], axis=0, keepdims=True)
        vt_ref[0, sl, :] = v_ref[:, sl].T.astype(BF16)


def _moba_prep(proj, gk):
    t = proj.shape[0]
    ntile = t // MOBA_TILE
    per = MOBA_TILE // MOBA_BLOCK
    return pl.pallas_call(
        _moba_prep_kernel,
        grid=(ntile,),
        in_specs=[pl.BlockSpec((MOBA_TILE, GROUP), lambda i: (i, CB_MK // 8)),
                  pl.BlockSpec((MOBA_TILE, GROUP), lambda i: (i, CB_MV // 8)),
                  pl.BlockSpec((1, HEAD_DIM), lambda i: (0, 0))],
        out_specs=[pl.BlockSpec((MOBA_TILE, GROUP), lambda i: (i, 0)),
                   pl.BlockSpec((per, 1, GROUP), lambda i: (i, 0, 0)),
                   pl.BlockSpec((1, GROUP, MOBA_TILE), lambda i: (i, 0, 0))],
        out_shape=[jax.ShapeDtypeStruct((t, GROUP), BF16),
                   jax.ShapeDtypeStruct((ntile * per, 1, GROUP), F32),
                   jax.ShapeDtypeStruct((ntile, GROUP, MOBA_TILE), BF16)],
        compiler_params=_cparams(("parallel",)),
        name="moba_prep",
    )(proj, proj, gk)


def _moba_kernel(q_ref, k_ref, vt_ref, km_ref, g_ref, o_ref, bias_ref, acc_ref, qaug_ref, s0_ref, s1_ref,
                 p0_ref, p1_ref, *, nb):
    ti = pl.program_id(2)
    blk, tile = MOBA_BLOCK, MOBA_TILE
    qn = _rms_rows(q_ref[...], g_ref[...])

    gate = lax.dot_general(km_ref[0], qn, (((1,), (1,)), ((), ())),
                           precision=lax.Precision.HIGHEST, preferred_element_type=F32)
    row = lax.broadcasted_iota(jnp.int32, gate.shape, 0)
    own = 2 * ti + (lax.broadcasted_iota(jnp.int32, gate.shape, 1) >= blk).astype(jnp.int32)
    rowf = row.astype(F32)
    gate = jnp.where(row < own, gate, -jnp.inf)
    bias = jnp.where(row == own, 0.0, NEG)
    for _ in range(MOBA_TOPK):
        m = jnp.max(gate, axis=0, keepdims=True)
        idx = jnp.min(jnp.where(gate == m, rowf, float(nb)), axis=0, keepdims=True)
        hit = jnp.logical_and(rowf == idx, m > -jnp.inf)
        bias = jnp.where(hit, 0.0, bias)
        gate = jnp.where(hit, -jnp.inf, gate)
    bias_ref[...] = bias

    qaug_ref[0:HEAD_DIM, :] = (qn * (HEAD_DIM ** -0.5 * LOG2E)).T.astype(BF16)
    qaug_ref[HEAD_DIM + BIAS_ROWS:, :] = jnp.zeros((HEAD_DIM - BIAS_ROWS, tile), BF16)
    er = lax.broadcasted_iota(jnp.int32, (tile, HEAD_DIM), 0)
    ec = lax.broadcasted_iota(jnp.int32, (tile, HEAD_DIM), 1)
    onehot = jnp.where(ec == er // blk, 1.0, 0.0).astype(BF16)
    brow = lax.broadcasted_iota(jnp.int32, (BIAS_ROWS, tile), 0)
    ones_rows = jnp.ones((BIAS_ROWS, tile), BF16)

    def scores(t, valid):
        b0 = jnp.where(valid, bias_ref[pl.ds(2 * t, 1), :], NEG)
        b1 = jnp.where(valid, bias_ref[pl.ds(2 * t + 1, 1), :], NEG)
        qaug_ref[HEAD_DIM:HEAD_DIM + BIAS_ROWS, :] = jnp.where(
            brow == 0, b0, jnp.where(brow == 1, b1, 0.0)).astype(BF16)
        kt = k_ref[pl.ds(pl.multiple_of(t * tile, tile), tile), :]
        return jnp.dot(jnp.concatenate([kt, onehot], axis=1), qaug_ref[...],
                       preferred_element_type=F32)

    s_refs, p_refs = (s0_ref, s1_ref), (p0_ref, p1_ref)
    kr = lax.broadcasted_iota(jnp.int32, (tile, tile), 0)
    qc = lax.broadcasted_iota(jnp.int32, (tile, tile), 1)
    s_refs[0][...] = jnp.where(kr <= qc, scores(ti, True), NEG)
    p_refs[1][...] = jnp.zeros((tile, tile), BF16)
    acc_ref[...] = jnp.zeros_like(acc_ref)

    def trip(n, par, carry):
        m, alpha_prev = carry
        tc = jnp.where(n == 1, ti, jnp.clip(n - 2, 0, ti))
        pv = jnp.dot(jnp.concatenate([vt_ref[tc], ones_rows], axis=0), p_refs[1 - par][...],
                     preferred_element_type=F32)
        s_refs[1 - par][...] = scores(jnp.minimum(n, ti), n < ti)
        s = s_refs[par][...]
        m_new = jnp.maximum(m, jnp.max(s, axis=0, keepdims=True))
        alpha = jnp.exp2(m - m_new)
        p_refs[par][...] = jnp.exp2(s - m_new).astype(BF16)
        acc_ref[...] = acc_ref[...] * alpha_prev + pv
        return m_new, alpha

    def body(j, carry):
        return trip(2 * j + 1, 1, trip(2 * j, 0, carry))

    init = (jnp.full((1, tile), NEG, F32), jnp.ones((1, tile), F32))
    lax.fori_loop(0, (ti + 3) // 2, body, init)
    o_ref[...] = (acc_ref[0:HEAD_DIM, :] / acc_ref[HEAD_DIM:HEAD_DIM + 1, :]).T.astype(o_ref.dtype)


def _moba(proj, kn, km, vt, gq, batch):
    t = proj.shape[0]
    s = t // batch
    nb = s // MOBA_BLOCK
    nt = s // MOBA_TILE
    km = km.reshape(batch, nb, GROUP)
    return pl.pallas_call(
        functools.partial(_moba_kernel, nb=nb),
        grid=(batch, N_HEADS, nt),
        in_specs=[pl.BlockSpec((MOBA_TILE, HEAD_DIM), lambda b, h, i: (b * nt + i, CB_MQ + h)),
                  pl.BlockSpec((s, HEAD_DIM), lambda b, h, i: (b, h)),
                  pl.BlockSpec((nt, HEAD_DIM, MOBA_TILE), lambda b, h, i: (b, h, 0)),
                  pl.BlockSpec((1, nb, HEAD_DIM), lambda b, h, i: (b, 0, h)),
                  pl.BlockSpec((1, HEAD_DIM), lambda b, h, i: (0, 0))],
        out_specs=pl.BlockSpec((MOBA_TILE, HEAD_DIM), lambda b, h, i: (b * nt + i, h)),
        out_shape=jax.ShapeDtypeStruct((t, GROUP), BF16),
        scratch_shapes=[pltpu.VMEM((nb, MOBA_TILE), F32),
                        pltpu.VMEM((HEAD_DIM + BIAS_ROWS, MOBA_TILE), F32),
                        pltpu.VMEM((2 * HEAD_DIM, MOBA_TILE), BF16),
                        pltpu.VMEM((MOBA_TILE, MOBA_TILE), F32), pltpu.VMEM((MOBA_TILE, MOBA_TILE), F32),
                        pltpu.VMEM((MOBA_TILE, MOBA_TILE), BF16), pltpu.VMEM((MOBA_TILE, MOBA_TILE), BF16)],
        compiler_params=_cparams(("parallel", "parallel", "arbitrary")),
        name="moba",
    )(proj, kn, vt, km, gq)


GDN_ROWS = 256
GDN_BASE = 8
GDN_HPS = 8


def _conv_silu(x_ref, halo_ref, w_ref, first):
    r = x_ref.shape[0]
    halo = halo_ref[...] * jnp.where(first, 0.0, 1.0)
    xb = jnp.concatenate([halo, x_ref[...]], axis=0)
    w = w_ref[...]
    out = None
    for tap in range(GDN_CONV):
        sh = GDN_CONV - 1 - tap
        xs = xb if sh == 0 else pltpu.roll(xb, sh, axis=0)
        term = xs[8:8 + r] * w[tap:tap + 1, :]
        out = term if out is None else out + term
    return _silu(out)


def _gdn_heads(qs, ks, vs, gs, betas, states):
    r = qs[0].shape[0]
    c = GDN_CHUNK
    pair = 2 * c
    npair = r // pair
    nh = len(qs)
    row = lax.broadcasted_iota(jnp.int32, (pair, pair), 0)
    col = lax.broadcasted_iota(jnp.int32, (pair, pair), 1)
    same = (row // c) == (col // c)
    tril = jnp.logical_and(same, row >= col)
    strict = jnp.logical_and(same, row > col)
    eye = (row == col).astype(F32)
    rin = row % c
    units = [(h, pi) for pi in range(npair) for h in range(nh)]

    def rows(x, u):
        return x[u[0]][u[1] * pair:(u[1] + 1) * pair]

    gcum = {u: rows(gs, u) for u in units}
    decay = {u: jnp.exp(jnp.where(tril, gcum[u] - gcum[u].T, -jnp.inf)) for u in units}
    eg = {u: jnp.exp(gcum[u]) for u in units}
    g_end = {u: (gcum[u][c - 1:c, :], gcum[u][pair - 1:pair, :]) for u in units}
    kb = {u: rows(ks, u) * rows(betas, u) for u in units}
    vb = {u: rows(vs, u) * rows(betas, u) for u in units}
    lmat = {u: jnp.where(strict, _bdot_nt(kb[u], rows(ks, u)) * decay[u], 0.0) for u in units}
    qk = {u: _bdot_nt(rows(qs, u), rows(ks, u)) * decay[u] for u in units}
    def blocks(size):
        return (row // size) == (col // size)

    lbase = {u: jnp.where(blocks(GDN_BASE), lmat[u], 0.0) for u in units}
    tinv = {u: eye - lbase[u] for u in units}
    lpow = lbase
    span = 1
    while 2 * span < GDN_BASE:
        lpow = {u: _bdot(lpow[u], lpow[u]) for u in units}
        tinv = {u: tinv[u] + _bdot(tinv[u], lpow[u]) for u in units}
        span *= 2
    size = GDN_BASE
    while size < c:
        below = jnp.logical_and(blocks(2 * size), jnp.logical_not(blocks(size)))
        tc = {u: _bdot(tinv[u], jnp.where(below, lmat[u], 0.0)) for u in units}
        tinv = {u: tinv[u] - _bdot(tc[u], tinv[u]) for u in units}
        size *= 2
    uw = {u: _bdot(tinv[u], jnp.concatenate([vb[u], kb[u] * eg[u]], axis=1)) for u in units}
    qd = {u: rows(qs, u) * eg[u] for u in units}
    kdt = {u: (rows(ks, u) * jnp.exp(jnp.where(row < c, g_end[u][0], g_end[u][1]) - gcum[u])).T for u in units}

    states = list(states)
    vns = {u: [] for u in units}
    o_st = {u: [] for u in units}
    for pi in range(npair):
        for ci in range(2):
            cs = slice(ci * c, (ci + 1) * c)
            for h in range(nh):
                u = (h, pi)
                ws = _bdot(jnp.concatenate([uw[u][cs, HEAD_DIM:], qd[u][cs]], axis=0), states[h])
                vn = uw[u][cs, :HEAD_DIM] - ws[:c]
                o_st[u].append(ws[c:])
                vns[u].append(vn)
                zero = jnp.zeros_like(vn)
                vn_pad = jnp.concatenate([vn, zero] if ci == 0 else [zero, vn], axis=0)
                states[h] = states[h] * jnp.exp(g_end[u][ci]) + _bdot(kdt[u], vn_pad)
    outs = []
    for h in range(nh):
        parts = [jnp.concatenate(o_st[(h, pi)], axis=0) + _bdot(qk[(h, pi)], jnp.concatenate(vns[(h, pi)], axis=0))
                 for pi in range(npair)]
        outs.append(jnp.concatenate(parts, axis=0))
    return outs, states


def _gdn_kernel(q_ref, k_ref, v_ref, qh_ref, kh_ref, vh_ref, wq_ref, wk_ref, wv_ref,
                gab_ref, z_ref, hp_ref, gn_ref, o_ref, s_ref):
    hg = pl.program_id(1)
    first = pl.program_id(2) == 0

    @pl.when(first)
    def _():
        s_ref[...] = jnp.zeros_like(s_ref)

    q2 = _conv_silu(q_ref, qh_ref, wq_ref, first)
    k2 = _conv_silu(k_ref, kh_ref, wk_ref, first)
    v2 = _conv_silu(v_ref, vh_ref, wv_ref, first)
    gab = gab_ref[...]
    lane = lax.broadcasted_iota(jnp.int32, gab.shape, 1)
    xg = gab + hp_ref[1:2, :]
    softplus = jnp.maximum(xg, 0.0) + jnp.log1p(jnp.exp(-jnp.abs(xg)))
    gcum_all = -jnp.exp(hp_ref[0:1, :]) * softplus
    beta_all = _sigmoid(gab)
    rin = lax.broadcasted_iota(jnp.int32, gab.shape, 0) % GDN_CHUNK
    sh = 1
    while sh < GDN_CHUNK:
        gcum_all = gcum_all + jnp.where(rin >= sh, pltpu.roll(gcum_all, sh, axis=0), 0.0)
        sh *= 2
    zeros = jnp.zeros_like(gab)

    qs, ks, vs, gs, betas = [], [], [], [], []
    for hh in range(GDN_HPS):
        h = hg * GDN_HPS + hh
        sl = slice(hh * HEAD_DIM, (hh + 1) * HEAD_DIM)
        q, k = q2[:, sl], k2[:, sl]
        qs.append(q * lax.rsqrt(jnp.sum(q * q, axis=-1, keepdims=True) + EPS) * (HEAD_DIM ** -0.5))
        ks.append(k * lax.rsqrt(jnp.sum(k * k, axis=-1, keepdims=True) + EPS))
        vs.append(v2[:, sl])
        gs.append(jnp.sum(jnp.where(lane == h, gcum_all, 0.0), axis=1, keepdims=True) + zeros)
        betas.append(jnp.sum(jnp.where(lane == N_HEADS + h, beta_all, 0.0), axis=1, keepdims=True) + zeros)

    outs, states = _gdn_heads(qs, ks, vs, gs, betas, [s_ref[hh] for hh in range(GDN_HPS)])
    for hh in range(GDN_HPS):
        sl = slice(hh * HEAD_DIM, (hh + 1) * HEAD_DIM)
        s_ref[hh] = states[hh]
        on = _rms_rows(outs[hh], gn_ref[...])
        o_ref[:, sl] = (on * _silu(z_ref[:, sl])).astype(o_ref.dtype)


def _gdn(proj, conv_w, hp, gn, batch):
    t = proj.shape[0]
    s = t // batch
    r = GDN_ROWS
    steps = s // r
    hb = r // 8
    wide = GDN_HPS * HEAD_DIM
    cpb = GDN_HPS

    def main(cb):
        return pl.BlockSpec((r, wide), lambda b, h, i: (b * steps + i, cb // cpb + h))

    def halo(cb):
        return pl.BlockSpec((8, wide), lambda b, h, i: (jnp.maximum((b * steps + i) * hb - 1, 0), cb // cpb + h))

    def wspec(off):
        return pl.BlockSpec((GDN_CONV, wide), lambda b, h, i: (0, off // cpb + h))

    return pl.pallas_call(
        _gdn_kernel,
        grid=(batch, N_HEADS // GDN_HPS, steps),
        in_specs=[main(CB_GQ), main(CB_GK), main(CB_GV), halo(CB_GQ), halo(CB_GK), halo(CB_GV),
                  wspec(0), wspec(N_HEADS), wspec(2 * N_HEADS),
                  pl.BlockSpec((r, LANES), lambda b, h, i: (b * steps + i, CB_GAB)),
                  main(CB_GZ),
                  pl.BlockSpec((2, LANES), lambda b, h, i: (0, 0)),
                  pl.BlockSpec((1, HEAD_DIM), lambda b, h, i: (0, 0))],
        out_specs=pl.BlockSpec((r, wide), lambda b, h, i: (b * steps + i, h)),
        out_shape=jax.ShapeDtypeStruct((t, GROUP), BF16),
        scratch_shapes=[pltpu.VMEM((GDN_HPS, HEAD_DIM, HEAD_DIM), F32)],
        compiler_params=_cparams(("parallel", "parallel", "arbitrary")),
        name="gdn",
    )(proj, proj, proj, proj, proj, proj, conv_w, conv_w, conv_w, proj, proj, hp, gn)


SC_ROWS = 512


def _sconv_kernel(b_ref, c_ref, x_ref, ch_ref, xh_ref, w_ref, o_ref, *, steps):
    first = pl.program_id(0) % steps == 0
    r = b_ref.shape[0]
    y = c_ref[...] * x_ref[...]
    yh = ch_ref[...] * xh_ref[...] * jnp.where(first, 0.0, 1.0)
    yb = jnp.concatenate([yh, y], axis=0)
    w = w_ref[...]
    out = None
    for tap in range(SC_CONV):
        sh = SC_CONV - 1 - tap
        ys = yb if sh == 0 else pltpu.roll(yb, sh, axis=0)
        term = ys[8:8 + r] * w[tap:tap + 1, :]
        out = term if out is None else out + term
    o_ref[...] = (b_ref[...] * out).astype(o_ref.dtype)


def _sconv(proj, w, batch):
    t = proj.shape[0]
    r = min(SC_ROWS, t // batch)
    steps = (t // batch) // r
    hb = r // 8

    def main(cb):
        return pl.BlockSpec((r, GROUP), lambda i: (i, cb // 8))

    def halo(cb):
        return pl.BlockSpec((8, GROUP), lambda i: (jnp.maximum(i * hb - 1, 0), cb // 8))

    return pl.pallas_call(
        functools.partial(_sconv_kernel, steps=steps),
        grid=(t // r,),
        in_specs=[main(CB_SCB), main(CB_SCC), main(CB_SCX), halo(CB_SCC), halo(CB_SCX),
                  pl.BlockSpec((SC_CONV, GROUP), lambda i: (0, 0))],
        out_specs=pl.BlockSpec((r, GROUP), lambda i: (i, 0)),
        out_shape=jax.ShapeDtypeStruct((t, GROUP), BF16),
        compiler_params=_cparams(("parallel",)),
        name="sconv",
    )(proj, proj, proj, proj, proj, w)


SWA_ROWS = 512


def _half_rms(x, g2):
    lane = lax.broadcasted_iota(jnp.int32, x.shape, 1)
    lo = lane < SWA_D
    x2 = x * x
    ms_lo = jnp.sum(jnp.where(lo, x2, 0.0), axis=-1, keepdims=True) * (1.0 / SWA_D)
    ms_hi = jnp.sum(jnp.where(lo, 0.0, x2), axis=-1, keepdims=True) * (1.0 / SWA_D)
    rs = jnp.where(lo, lax.rsqrt(ms_lo + EPS), lax.rsqrt(ms_hi + EPS))
    return x * rs * g2


def _swa_kernel(q_ref, k_ref, v_ref, kh_ref, vh_ref, gq_ref, gk_ref, sink_ref, o_ref, *, steps):
    first = pl.program_id(0) % steps == 0
    r = q_ref.shape[0]
    w = SWA_W
    nsub = r // w
    pairs = SWA_Q_HEADS // 2
    lane = lax.broadcasted_iota(jnp.int32, (r + w, LANES), 1)

    kn = _half_rms(jnp.concatenate([kh_ref[...], k_ref[...]], axis=0), gk_ref[...])
    kroll = pltpu.roll(kn, SWA_D, axis=1)
    kdup = (jnp.where(lane < SWA_D, kn, kroll).astype(BF16),
            jnp.where(lane < SWA_D, kroll, kn).astype(BF16))
    vt = jnp.concatenate([vh_ref[...], v_ref[...]], axis=0).T.astype(BF16)

    kr = lax.broadcasted_iota(jnp.int32, (2 * w, 2 * w), 0)
    qc = lax.broadcasted_iota(jnp.int32, (2 * w, 2 * w), 1) % w
    band = jnp.logical_and(kr > qc, kr <= qc + w)
    band0 = jnp.logical_and(band, kr >= jnp.where(first, w, 0))
    qlane = lax.broadcasted_iota(jnp.int32, (w, LANES), 1)

    heads = range(pairs)
    kvh = [c // (pairs // SWA_KV_HEADS) for c in heads]
    for sub in range(nsub):
        mask = band0 if sub == 0 else band
        ks = slice(sub * w, sub * w + 2 * w)
        rows = slice(sub * w, (sub + 1) * w)
        qts = []
        for c in heads:
            qn = _half_rms(q_ref[rows, c * LANES:(c + 1) * LANES], gq_ref[...]) * (SWA_D ** -0.5 * LOG2E)
            qa = jnp.where(qlane < SWA_D, qn, 0.0)
            qb = jnp.where(qlane < SWA_D, 0.0, qn)
            qts.append(jnp.concatenate([qa.T, qb.T], axis=1).astype(BF16))
        ss = [jnp.dot(kdup[kvh[c]][ks], qts[c], preferred_element_type=F32) for c in heads]
        pns = []
        for c in heads:
            s = jnp.where(mask, ss[c], NEG)
            sink = sink_ref[c:c + 1, :] * LOG2E
            m = jnp.maximum(jnp.max(s, axis=0, keepdims=True), sink)
            p = jnp.exp2(s - m)
            l = jnp.sum(p, axis=0, keepdims=True) + jnp.exp2(sink - m)
            pns.append((p * (1.0 / l)).astype(BF16))
        ots = [jnp.dot(vt[kvh[c] * SWA_D:(kvh[c] + 1) * SWA_D, ks], pns[c], preferred_element_type=F32)
               for c in heads]
        for c in heads:
            o = jnp.concatenate([ots[c][:, :w], ots[c][:, w:]], axis=0).T
            o_ref[rows, c * LANES:(c + 1) * LANES] = o.astype(o_ref.dtype)


def _swa(proj, gq2, gk2, sinkrow, batch):
    t = proj.shape[0]
    r = min(SWA_ROWS, t // batch)
    steps = (t // batch) // r
    hb = r // SWA_W

    def halo(cb):
        return pl.BlockSpec((SWA_W, LANES), lambda i: (jnp.maximum(i * hb - 1, 0), cb))

    return pl.pallas_call(
        functools.partial(_swa_kernel, steps=steps),
        grid=(t // r,),
        in_specs=[pl.BlockSpec((r, GROUP), lambda i: (i, CB_SQ // 8)),
                  pl.BlockSpec((r, LANES), lambda i: (i, CB_SK)),
                  pl.BlockSpec((r, LANES), lambda i: (i, CB_SV)),
                  halo(CB_SK), halo(CB_SV),
                  pl.BlockSpec((1, LANES), lambda i: (0, 0)),
                  pl.BlockSpec((1, LANES), lambda i: (0, 0)),
                  pl.BlockSpec((SWA_Q_HEADS // 2, 2 * SWA_W), lambda i: (0, 0))],
        out_specs=pl.BlockSpec((r, GROUP), lambda i: (i, 0)),
        out_shape=jax.ShapeDtypeStruct((t, GROUP), BF16),
        compiler_params=_cparams(("parallel",)),
        name="swa",
    )(proj, proj, proj, proj, proj, gq2, gk2, sinkrow)


RG_ROWS = 256
AB_ROWS = 2 * N_HEADS


def _regroup_kernel(x_ref, nx_ref, ab_ref, g_ref, o_ref, *, head_blocks, tail_blocks):
    j = pl.program_id(1)
    g = g_ref[...]

    @pl.when(j < head_blocks)
    def _():
        o_ref[...] = (x_ref[...] * g).astype(o_ref.dtype)

    @pl.when(jnp.logical_and(j >= head_blocks, j < head_blocks + tail_blocks))
    def _():
        o_ref[...] = (jnp.concatenate([x_ref[AB_ROWS:, :], nx_ref[...]], axis=0) * g).astype(o_ref.dtype)

    @pl.when(j == head_blocks + tail_blocks)
    def _():
        zeros = jnp.zeros((o_ref.shape[0] - AB_ROWS, o_ref.shape[1]), F32)
        o_ref[...] = jnp.concatenate([ab_ref[...] * g, zeros], axis=0).astype(o_ref.dtype)


def _regroup_w_in(w_in, gain):
    wt = jnp.swapaxes(w_in, 1, 2)
    nl, n_real, d = wt.shape
    head_blocks = A_COLS // RG_ROWS
    tail_blocks = (n_real - A_COLS - AB_ROWS) // RG_ROWS
    assert A_COLS + AB_ROWS + tail_blocks * RG_ROWS == n_real and (head_blocks + tail_blocks + 1) * RG_ROWS == NP_COLS
    per = RG_ROWS // AB_ROWS
    last_ab = n_real // AB_ROWS - 1
    return pl.pallas_call(
        functools.partial(_regroup_kernel, head_blocks=head_blocks, tail_blocks=tail_blocks),
        grid=(nl, NP_COLS // RG_ROWS),
        in_specs=[pl.BlockSpec((None, RG_ROWS, d), lambda l, j: (l, jnp.minimum(j, head_blocks + tail_blocks - 1), 0)),
                  pl.BlockSpec((None, AB_ROWS, d), lambda l, j: (l, jnp.minimum(per * (j + 1), last_ab), 0)),
                  pl.BlockSpec((None, AB_ROWS, d), lambda l, j: (l, A_COLS // AB_ROWS, 0)),
                  pl.BlockSpec((None, 1, d), lambda l, j: (l, 0, 0))],
        out_specs=pl.BlockSpec((None, RG_ROWS, d), lambda l, j: (l, j, 0)),
        out_shape=jax.ShapeDtypeStruct((nl, NP_COLS, d), BF16),
        compiler_params=_cparams(("parallel", "parallel")),
        name="regroup_w_in",
    )(wt, wt, wt, gain.astype(F32)[:, None, :])


def _layer(x, xb, batch, layer, w_in, moba_q_norm, moba_k_norm, gdn_conv, gdn_a_log, gdn_dt_bias,
           gdn_out_norm, sc_conv, swa_q_norm, swa_k_norm, swa_sinks, w_out, ffn_gain, w_gate, w_up, w_down):
    row = lambda a: a.reshape(1, -1).astype(F32)

    proj = _in_proj(xb, w_in, layer, tm=2048, tn=512)

    kn, km, vt = _moba_prep(proj, row(moba_k_norm))
    o_a = _moba(proj, kn, km, vt, row(moba_q_norm), batch)

    hp = jnp.pad(jnp.stack([gdn_a_log, gdn_dt_bias]).astype(F32), ((0, 0), (0, LANES - N_HEADS)))
    o_b = _gdn(proj, gdn_conv.astype(F32), hp, row(gdn_out_norm), batch)

    o_c = _sconv(proj, sc_conv.astype(F32), batch)

    sinkrow = jnp.repeat(swa_sinks.astype(F32), SWA_W).reshape(SWA_Q_HEADS // 2, 2 * SWA_W)
    o_d = _swa(proj, row(jnp.tile(swa_q_norm, 2)), row(jnp.tile(swa_k_norm, 2)), sinkrow, batch)

    x, xb = _out_proj((o_a, o_b, o_c, o_d), w_out, x, layer, tm=1024, tn=512, tn_first=256)

    act = _ffn_up(xb, ffn_gain, w_gate, w_up, layer, D_FF_PAD, tm=1024, tn=512, tn_first=256)
    return _mm_res(act, w_down, x, layer, tm=1024, tn=1024, tk=D_FF_PAD // 4, tn_first=512)


def kernel(x, norm_mix, w_in, moba_q_norm, moba_k_norm, gdn_conv, gdn_a_log, gdn_dt_bias, gdn_out_norm, sc_conv, swa_q_norm, swa_k_norm, swa_sinks, w_out, norm_ffn, w_gate, w_up, w_down):
    batch, seq, d = x.shape
    w_in_b = _regroup_w_in(w_in, norm_mix)
    ffn_gain = norm_ffn.astype(F32)[:, :, None]
    h = x.reshape(batch * seq, d)
    hb = h.astype(BF16)
    for l in range(norm_mix.shape[0]):
        h, hb = _layer(h, hb, batch, l, w_in_b, moba_q_norm[l], moba_k_norm[l], gdn_conv[l], gdn_a_log[l],
                       gdn_dt_bias[l], gdn_out_norm[l], sc_conv[l], swa_q_norm[l], swa_k_norm[l], swa_sinks[l],
                       w_out, ffn_gain, w_gate, w_up, w_down)
    return h.reshape(batch, seq, d)
```
